```python
import math
import jax
import jax.numpy as jnp
from jax import lax
import numpy as np

D_MODEL = 1024
BATCH = 2
SEQ = 8192
DEPTH = 4
DEC_BATCH = 32
DEC_SEQ = 8
PAST_LEN = 8192
PAGE_SIZE = 128

N_MIXERS = 3
N_A = (DEPTH + 2) // 3
N_B = (DEPTH + 1) // 3
N_C = DEPTH // 3
EPS = 1e-6
D_PLE = 256
E_A = 2 * D_MODEL
NH_A = 4
DV_A = E_A // NH_A
DQK_A = DV_A // 2
CONV_W = 4
CHUNK_A = 64
NH_B = 16
DH_B = D_MODEL // NH_B
AW_B = NH_B * DH_B
NH_IDX = 8
D_IDX = 64
TOPK_MAX = 256
QBLOCK = 128
ROPE_THETA = 10000.0
B_IN_SIZES = (AW_B, AW_B, AW_B, AW_B, NH_IDX * D_IDX, D_IDX, NH_IDX)
B_IN = sum(B_IN_SIZES)
E_C = 2 * D_MODEL
POOL_WINDOWS = (2, 4, 8, 16)
G_C = E_C // len(POOL_WINDOWS)
POOL_BUF = max(POOL_WINDOWS) - 1

kernel_name = 'hybrid_mlstm_dsa_pool_step'


def rmsnorm(x, g):
    xf = x.astype(jnp.float32)
    y = xf * lax.rsqrt(jnp.mean(xf * xf, axis=-1, keepdims=True) + EPS)
    return (y * g.astype(jnp.float32)).astype(x.dtype)


def rope(x, pos):
    half = x.shape[-1] // 2
    inv = ROPE_THETA ** (-jnp.arange(half, dtype=jnp.float32) / half)
    ang = pos.astype(jnp.float32)[:, None] * inv[None, :]
    cos = jnp.cos(ang)[None, :, None, :]
    sin = jnp.sin(ang)[None, :, None, :]
    xf = x.astype(jnp.float32)
    x1, x2 = xf[..., :half], xf[..., half:]
    return jnp.concatenate([x1 * cos - x2 * sin, x2 * cos + x1 * sin], axis=-1).astype(x.dtype)


def causal_conv(xb, buf, w, b):
    T = xb.shape[1]
    ext = jnp.concatenate([buf.astype(xb.dtype), xb], axis=1)
    out = b + ext[:, 0:T] * w[0]
    for j in range(1, CONV_W):
        out = out + ext[:, j:j + T] * w[j]
    return out, ext[:, T:]


def mlstm_chunkwise(q, k, v, li, lf, C0, n0, m0):
    dt = q.dtype
    B, T, H, dk = q.shape
    dv = v.shape[-1]
    L = math.gcd(T, CHUNK_A)
    NC = T // L

    def to_chunks(a):
        a = a.reshape((B, NC, L, H) + a.shape[3:])
        return jnp.moveaxis(a, (1, 3), (0, 2))

    f32 = jnp.float32
    xs = (to_chunks(q.astype(f32) * (dk ** -0.5)), to_chunks(k.astype(f32)), to_chunks(v.astype(f32)),
          to_chunks(li.astype(f32)), to_chunks(lf.astype(f32)))
    causal = jnp.tril(jnp.ones((L, L), dtype=bool))

    def step(carry, xc):
        C, n, m = carry
        qc, kc, vc, lic, lfc = xc
        b = jnp.cumsum(lfc, axis=-1)
        Dm = jnp.where(causal, b[..., :, None] - b[..., None, :] + lic[..., None, :], -jnp.inf)
        m_inter = b + m[..., None]
        m_t = jnp.maximum(m_inter, jnp.max(Dm, axis=-1))
        A = jnp.einsum('bhtk,bhsk->bhts', qc, kc) * jnp.exp(Dm - m_t[..., None])
        inter = jnp.exp(m_inter - m_t)
        num = jnp.einsum('bhts,bhsv->bhtv', A, vc) + inter[..., None] * jnp.einsum('bhtk,bhkv->bhtv', qc, C)
        den = jnp.sum(A, axis=-1) + inter * jnp.einsum('bhtk,bhk->bht', qc, n)
        h = num / jnp.maximum(jnp.abs(den), jnp.exp(-m_t))[..., None]
        bL = b[..., -1]
        g = bL[..., None] - b + lic
        m_new = jnp.maximum(bL + m, jnp.max(g, axis=-1))
        wgt = jnp.exp(g - m_new[..., None])
        decay = jnp.exp(bL + m - m_new)
        C_new = decay[..., None, None] * C + jnp.einsum('bhs,bhsk,bhsv->bhkv', wgt, kc, vc)
        n_new = decay[..., None] * n + jnp.einsum('bhs,bhsk->bhk', wgt, kc)
        return (C_new, n_new, m_new), h

    carry0 = (C0.astype(f32), n0.astype(f32), m0.astype(f32))
    (C, n, m), hs = lax.scan(step, carry0, xs)
    h = jnp.moveaxis(hs, (0, 2), (1, 3)).reshape(B, T, H, dv)
    return h.astype(dt), C.astype(C0.dtype), n.astype(n0.dtype), m.astype(m0.dtype)


def mlstm_layer(u, C0, n0, m0, conv_buf, prm, j):
    B, T, _ = u.shape
    xb, z = jnp.split(u @ prm['a_w_up'][j], 2, axis=-1)
    xc_pre, new_buf = causal_conv(xb, conv_buf, prm['a_conv_w'][j], prm['a_conv_b'][j])
    xc = jax.nn.silu(xc_pre)
    q = (xc @ prm['a_w_q'][j]).reshape(B, T, NH_A, DQK_A)
    k = (xc @ prm['a_w_k'][j]).reshape(B, T, NH_A, DQK_A)
    v = (xb @ prm['a_w_v'][j]).reshape(B, T, NH_A, DV_A)
    gates = (xc @ prm['a_w_if'][j] + prm['a_b_if'][j]).astype(jnp.float32)
    li = gates[..., :NH_A]
    lf = jax.nn.log_sigmoid(gates[..., NH_A:])
    h, C, n, m = mlstm_chunkwise(q, k, v, li, lf, C0, n0, m0)
    h = rmsnorm(h, prm['a_norm_g'][j].reshape(NH_A, DV_A)).reshape(B, T, E_A)
    o = jax.nn.sigmoid(xb @ prm['a_w_o'][j] + prm['a_b_o'][j])
    out = (o * h + prm['a_skip'][j] * xc) * jax.nn.silu(z)
    return out @ prm['a_w_down'][j], (C, n, m, new_buf)


def dsa_project(u, pos, w_in):
    B, T, _ = u.shape
    offs = [int(o) for o in np.cumsum(B_IN_SIZES)[:-1]]
    q, k, v, gate, qi, ki, wi = jnp.split(u @ w_in, offs, axis=-1)
    q = rope(q.reshape(B, T, NH_B, DH_B), pos)
    k = rope(k.reshape(B, T, NH_B, DH_B), pos)
    v = v.reshape(B, T, NH_B, DH_B)
    qi = rope(qi.reshape(B, T, NH_IDX, D_IDX), pos)
    ki = rope(ki.reshape(B, T, 1, D_IDX), pos)[:, :, 0]
    wi = wi * (NH_IDX ** -0.5 * D_IDX ** -0.5)
    return q, k, v, gate, qi, ki, wi


def dsa_attend(q, qi, wi, qpos, ki_all, gather_kv, k_top):
    B, T = q.shape[:2]
    QB = math.gcd(T, QBLOCK)
    NB = T // QB
    L = ki_all.shape[1]
    key_pos = jnp.arange(L, dtype=jnp.int32)

    def blocks(a):
        return jnp.moveaxis(a.reshape((B, NB, QB) + a.shape[2:]), 1, 0)

    def one_block(xs):
        qb, qib, wib, pb = xs
        rel = jax.nn.relu(jnp.einsum('bthd,bsd->bths', qib, ki_all).astype(jnp.float32))
        score = jnp.einsum('bths,bth->bts', rel, wib.astype(jnp.float32))
        score = jnp.where(key_pos[None, None, :] <= pb[None, :, None], score, -jnp.inf)
        _, idx = lax.top_k(score, k_top)
        valid = idx <= pb[None, :, None]
        ks, vs = gather_kv(idx)
        logits = jnp.einsum('bthd,btjhd->bthj', qb, ks).astype(jnp.float32) * (DH_B ** -0.5)
        logits = jnp.where(valid[:, :, None, :], logits, -jnp.inf)
        prob = jax.nn.softmax(logits, axis=-1).astype(vs.dtype)
        return jnp.einsum('bthj,btjhd->bthd', prob, vs)

    out = lax.map(one_block, (blocks(q), blocks(qi), blocks(wi), qpos.reshape(NB, QB)))
    return jnp.moveaxis(out, 0, 1).reshape(q.shape)


def dsa_layer(u, pos, ctx, prm, j):
    B, T, _ = u.shape
    q, k, v, gate, qi, ki, wi = dsa_project(u, pos, prm['b_w_in'][j])
    bidx = jnp.arange(B)[:, None, None]
    if ctx is None:
        ki_all = ki

        def gather_kv(idx):
            return k[bidx, idx], v[bidx, idx]
    else:
        cache_k, cache_v, cache_kidx, page_table = ctx
        P = page_table.shape[1] * PAGE_SIZE
        ki_past = cache_kidx[j][page_table].reshape(B, P, D_IDX).astype(ki.dtype)
        ki_all = jnp.concatenate([ki_past, ki], axis=1)
        k_pool = cache_k[j].reshape(-1, NH_B, DH_B)
        v_pool = cache_v[j].reshape(-1, NH_B, DH_B)

        def gather_kv(idx):
            in_cache = (idx < P)[..., None, None]
            pidx = jnp.minimum(idx, P - 1)
            phys = page_table[bidx, pidx // PAGE_SIZE] * PAGE_SIZE + pidx % PAGE_SIZE
            nidx = jnp.clip(idx - P, 0, T - 1)
            ks = jnp.where(in_cache, k_pool[phys].astype(k.dtype), k[bidx, nidx])
            vs = jnp.where(in_cache, v_pool[phys].astype(v.dtype), v[bidx, nidx])
            return ks, vs
    L = ki_all.shape[1]
    k_top = min(TOPK_MAX, L // 4)
    attn = dsa_attend(q, qi, wi, pos, ki_all, gather_kv, k_top)
    out = attn.reshape(B, T, AW_B) * jax.nn.silu(gate)
    return out @ prm['b_w_out'][j], (k, v, ki)


def pool_layer(u, buf, pos, prm, j):
    xb, z = jnp.split(u @ prm['c_w_up'][j], 2, axis=-1)
    B, T, _ = xb.shape
    ext = jnp.concatenate([buf.astype(xb.dtype), xb], axis=1)
    cs = jnp.concatenate([jnp.zeros((B, 1, E_C), jnp.float32), jnp.cumsum(ext.astype(jnp.float32), axis=1)], axis=1)
    outs = []
    for g, w in enumerate(POOL_WINDOWS):
        sl = slice(g * G_C, (g + 1) * G_C)
        win = cs[:, POOL_BUF + 1:POOL_BUF + 1 + T, sl] - cs[:, POOL_BUF + 1 - w:POOL_BUF + 1 - w + T, sl]
        cnt = jnp.minimum(w, pos + 1).astype(jnp.float32)
        mean = (win / cnt[None, :, None]).astype(xb.dtype)
        outs.append((mean - xb[..., sl]) @ prm['c_w_grp'][j, g])
    y = jnp.concatenate(outs, axis=-1) * prm['c_scale'][j]
    return (y * jax.nn.silu(z)) @ prm['c_w_down'][j], ext[:, T:]


def run_group(x, p, pos, mlstm_state, dsa_ctx, pool_state, prm):
    m_C, m_n, m_m, m_conv = mlstm_state
    new_a, new_b, new_c = [], [], []
    for i in range(DEPTH):
        kind, j = i % N_MIXERS, i // N_MIXERS
        h = rmsnorm(x, prm['norm_g'][i])
        if kind == 0:
            y, st = mlstm_layer(h, m_C[j], m_n[j], m_m[j], m_conv[j], prm, j)
            new_a.append(st)
        elif kind == 1:
            y, st = dsa_layer(h, pos, dsa_ctx, prm, j)
            new_b.append(st)
        else:
            y, st = pool_layer(h, pool_state[j], pos, prm, j)
            new_c.append(st)
        x = x + y
        x = x + jax.nn.sigmoid(x @ prm['ple_gate_w'][i]) * (p[i] @ prm['ple_w'][i])
    y = rmsnorm(x, prm['final_g'])
    a_states = tuple(jnp.stack([s[r] for s in new_a]) for r in range(4))
    b_states = tuple(jnp.stack([s[r] for s in new_b]) for r in range(3))
    c_state = jnp.stack(new_c)
    return y, a_states, b_states, c_state


def setup_inputs(seed: int = 0) -> dict:
    key = jax.random.key(seed)
    ks = iter(jax.random.split(key, 64))

    def nrm(shape, scale):
        return jax.random.normal(next(ks), shape, jnp.float32) * scale

    n_pages = PAST_LEN // PAGE_SIZE
    n_used = DEC_BATCH * n_pages
    n_pool = n_used + (n_used + 3) // 4
    page_table = jax.random.permutation(next(ks), n_pool)[:n_used].reshape(DEC_BATCH, n_pages).astype(jnp.int32)
    return {
        'x_prompt': nrm((BATCH, SEQ, D_MODEL), 1.0),
        'x_sample': nrm((DEC_BATCH, DEC_SEQ, D_MODEL), 1.0),
        'state_mlstm_C': nrm((N_A, DEC_BATCH, NH_A, DQK_A, DV_A), 1.0),
        'state_mlstm_n': nrm((N_A, DEC_BATCH, NH_A, DQK_A), 1.0),
        'state_mlstm_m': nrm((N_A, DEC_BATCH, NH_A), 0.5),
        'state_mlstm_conv': nrm((N_A, DEC_BATCH, CONV_W - 1, E_A), 1.0),
        'state_pool': nrm((N_C, DEC_BATCH, POOL_BUF, E_C), 1.0),
        'cache_k': nrm((N_B, n_pool, PAGE_SIZE, NH_B, DH_B), 1.0),
        'cache_v': nrm((N_B, n_pool, PAGE_SIZE, NH_B, DH_B), 1.0),
        'cache_kidx': nrm((N_B, n_pool, PAGE_SIZE, D_IDX), 1.0),
        'page_table': page_table,
        'p_prompt': nrm((DEPTH, BATCH, SEQ, D_PLE), 1.0),
        'p_sample': nrm((DEPTH, DEC_BATCH, DEC_SEQ, D_PLE), 1.0),
        'norm_g': 1.0 + nrm((DEPTH, D_MODEL), 0.02),
        'final_g': 1.0 + nrm((D_MODEL,), 0.02),
        'ple_w': nrm((DEPTH, D_PLE, D_MODEL), 0.5 * D_PLE ** -0.5),
        'ple_gate_w': nrm((DEPTH, D_MODEL, D_MODEL), D_MODEL ** -0.5),
        'a_w_up': nrm((N_A, D_MODEL, 2 * E_A), D_MODEL ** -0.5),
        'a_conv_w': nrm((N_A, CONV_W, E_A), CONV_W ** -0.5),
        'a_conv_b': nrm((N_A, E_A), 0.02),
        'a_w_q': nrm((N_A, E_A, NH_A * DQK_A), E_A ** -0.5),
        'a_w_k': nrm((N_A, E_A, NH_A * DQK_A), E_A ** -0.5),
        'a_w_v': nrm((N_A, E_A, E_A), E_A ** -0.5),
        'a_w_if': nrm((N_A, E_A, 2 * NH_A), 0.3 * E_A ** -0.5),
        'a_b_if': jnp.concatenate([nrm((N_A, NH_A), 0.1), 3.0 + nrm((N_A, NH_A), 0.5)], axis=-1),
        'a_w_o': nrm((N_A, E_A, E_A), E_A ** -0.5),
        'a_b_o': nrm((N_A, E_A), 0.02),
        'a_norm_g': 1.0 + nrm((N_A, E_A), 0.02),
        'a_skip': 1.0 + nrm((N_A, E_A), 0.02),
        'a_w_down': nrm((N_A, E_A, D_MODEL), E_A ** -0.5),
        'b_w_in': nrm((N_B, D_MODEL, B_IN), D_MODEL ** -0.5),
        'b_w_out': nrm((N_B, AW_B, D_MODEL), AW_B ** -0.5),
        'c_w_up': nrm((N_C, D_MODEL, 2 * E_C), D_MODEL ** -0.5),
        'c_w_grp': nrm((N_C, len(POOL_WINDOWS), G_C, G_C), G_C ** -0.5),
        'c_scale': 1.0 + nrm((N_C, E_C), 0.02),
        'c_w_down': nrm((N_C, E_C, D_MODEL), E_C ** -0.5),
    }


def reference(x_prompt, x_sample, state_mlstm_C, state_mlstm_n, state_mlstm_m, state_mlstm_conv, state_pool,
              cache_k, cache_v, cache_kidx, page_table, p_prompt, p_sample,
              norm_g, final_g, ple_w, ple_gate_w,
              a_w_up, a_conv_w, a_conv_b, a_w_q, a_w_k, a_w_v, a_w_if, a_b_if, a_w_o, a_b_o, a_norm_g, a_skip, a_w_down,
              b_w_in, b_w_out, c_w_up, c_w_grp, c_scale, c_w_down):
    prm = dict(norm_g=norm_g, final_g=final_g, ple_w=ple_w, ple_gate_w=ple_gate_w,
               a_w_up=a_w_up, a_conv_w=a_conv_w, a_conv_b=a_conv_b, a_w_q=a_w_q, a_w_k=a_w_k, a_w_v=a_w_v,
               a_w_if=a_w_if, a_b_if=a_b_if, a_w_o=a_w_o, a_b_o=a_b_o, a_norm_g=a_norm_g, a_skip=a_skip,
               a_w_down=a_w_down, b_w_in=b_w_in, b_w_out=b_w_out,
               c_w_up=c_w_up, c_w_grp=c_w_grp, c_scale=c_scale, c_w_down=c_w_down)
    B, S, _ = x_prompt.shape
    dt = x_prompt.dtype
    init_a = (jnp.zeros((N_A, B, NH_A, DQK_A, DV_A), dt), jnp.zeros((N_A, B, NH_A, DQK_A), dt),
              jnp.zeros((N_A, B, NH_A), dt), jnp.zeros((N_A, B, CONV_W - 1, E_A), dt))
    init_c = jnp.zeros((N_C, B, POOL_BUF, E_C), dt)
    pos_p = jnp.arange(S, dtype=jnp.int32)
    y_prompt, (p_C, p_n, p_m, p_conv), (p_k, p_v, p_kidx), p_pool = run_group(
        x_prompt, p_prompt, pos_p, init_a, None, init_c, prm)
    past = page_table.shape[1] * PAGE_SIZE
    pos_s = past + jnp.arange(x_sample.shape[1], dtype=jnp.int32)
    y_sample, (s_C, s_n, s_m, s_conv), (s_k, s_v, s_kidx), s_pool = run_group(
        x_sample, p_sample, pos_s, (state_mlstm_C, state_mlstm_n, state_mlstm_m, state_mlstm_conv),
        (cache_k, cache_v, cache_kidx, page_table), state_pool, prm)
    return (y_prompt, y_sample, p_C, p_n, p_m, p_conv, p_k, p_v, p_kidx, p_pool,
            s_C, s_n, s_m, s_conv, s_k, s_v, s_kidx, s_pool)
```

```python
import functools
import math

import numpy as np
import jax
import jax.numpy as jnp
from jax import lax
from jax.experimental import pallas as pl
from jax.experimental.pallas import tpu as pltpu

EPS = 1e-6
ROPE_THETA = 10000.0
TOPK_MAX = 256
POOL_WINDOWS = (2, 4, 8, 16)
CONV_W = 4
MLSTM_CHUNK = 256
NEG = -1e30
LANES = 128
SUBLANES = 8
VMEM_LIMIT_BYTES = 56 * 1024 * 1024
ATT_TQ, ATT_TK = 256, 512
PAGES_PER_STEP = 8

F32 = jnp.float32
BF16 = jnp.bfloat16
INT_MIN = -2 ** 31


def _params(*sem):
    return pltpu.CompilerParams(dimension_semantics=sem, vmem_limit_bytes=VMEM_LIMIT_BYTES)


def _sigmoid(x):
    return 1.0 / (1.0 + jnp.exp(-x))


def _silu(x):
    return x * _sigmoid(x)


def _log_sigmoid(x):
    return jnp.minimum(x, 0.0) - jnp.log(1.0 + jnp.exp(-jnp.abs(x)))


def _dot(a, b):
    return jnp.dot(a, b, preferred_element_type=F32)


def _dot_nt(a, b):
    return lax.dot_general(a, b, (((1,), (1,)), ((), ())), preferred_element_type=F32)


def _dot_tn(a, b):
    return lax.dot_general(a, b, (((0,), (0,)), ((), ())), preferred_element_type=F32)


def _pick(n, cands):
    for c in cands:
        if n % c == 0:
            return c
    return n


def _rope_tile(a, cos, sin_signed):
    lane = lax.broadcasted_iota(jnp.int32, (1, LANES), 1)
    first_half = (lane % 64) < 32
    pieces = []
    for gi in range(a.shape[1] // LANES):
        ag = a[:, gi * LANES:(gi + 1) * LANES]
        ahead = pltpu.roll(ag, LANES - 32, 1)
        behind = pltpu.roll(ag, 32, 1)
        rot = jnp.where(first_half, ahead, behind)
        pieces.append(ag * cos + rot * sin_signed)
    return pieces[0] if len(pieces) == 1 else jnp.concatenate(pieces, axis=1)


def _norm_matmul_body(*refs, rope, n_out, scale):
    if rope:
        x_ref, g_ref, w_ref, cos_ref, sin_ref = refs[:5]
        rest = refs[5:]
    else:
        x_ref, g_ref, w_ref = refs[:3]
        rest = refs[3:]
    outs, xn_ref = rest[:n_out], rest[n_out]

    @pl.when(pl.program_id(1) == 0)
    def _():
        xf = x_ref[...]
        y = xf * lax.rsqrt(jnp.mean(xf * xf, axis=-1, keepdims=True) + EPS) * g_ref[...]
        xn_ref[...] = y.astype(BF16)

    acc = _dot(xn_ref[...], w_ref[...])
    if rope:
        acc = _rope_tile(acc, cos_ref[...], sin_ref[...])
    if scale != 1.0:
        acc = acc * scale
    for o in outs:
        o[...] = acc.astype(o.dtype)


def norm_matmul(x, g, w, out_dtypes, rope_tabs=None, tab_blocks=1, scale=1.0):
    M, D = x.shape
    N = w.shape[1]
    tm = _pick(M, (512, 256))
    tn = _pick(N, (512, 256, 128))
    in_specs = [pl.BlockSpec((tm, D), lambda i, j: (i, 0)),
                pl.BlockSpec((1, D), lambda i, j: (0, 0)),
                pl.BlockSpec((D, tn), lambda i, j: (0, j))]
    args = [x, g.reshape(1, D), w]
    if rope_tabs is not None:
        for t in rope_tabs:
            in_specs.append(pl.BlockSpec((tm, LANES), lambda i, j: (i % tab_blocks, 0)))
            args.append(t)
    return pl.pallas_call(
        functools.partial(_norm_matmul_body, rope=rope_tabs is not None, n_out=len(out_dtypes),
                          scale=scale),
        grid=(M // tm, N // tn),
        in_specs=in_specs,
        out_specs=[pl.BlockSpec((tm, tn), lambda i, j: (i, j)) for _ in out_dtypes],
        out_shape=[jax.ShapeDtypeStruct((M, N), dt) for dt in out_dtypes],
        scratch_shapes=[pltpu.VMEM((tm, D), BF16)],
        compiler_params=_params("parallel", "arbitrary"),
        name="norm_matmul",
    )(*args)


def _matmul_body(*refs, has_bias, act, n_li):
    if has_bias:
        x_ref, w_ref, b_ref, o_ref = refs
    else:
        x_ref, w_ref, o_ref = refs
    acc = _dot(x_ref[...], w_ref[...])
    if has_bias:
        acc = acc + b_ref[...]
    if act == "sigmoid":
        acc = _sigmoid(acc)
    elif act == "gates":
        col = lax.broadcasted_iota(jnp.int32, acc.shape, 1)
        acc = jnp.where(col < n_li, acc, _log_sigmoid(acc))
    o_ref[...] = acc.astype(o_ref.dtype)


def matmul(x, w, out_dtype, bias=None, act=None, n_li=0):
    M, K = x.shape
    N = w.shape[1]
    tm = _pick(M, (512, 256))
    tn = _pick(N, (512, 256, 128))
    in_specs = [pl.BlockSpec((tm, K), lambda i, j: (i, 0)),
                pl.BlockSpec((K, tn), lambda i, j: (0, j))]
    args = [x, w]
    if bias is not None:
        in_specs.append(pl.BlockSpec((1, tn), lambda i, j: (0, j)))
        args.append(bias.reshape(1, N).astype(F32))
    return pl.pallas_call(
        functools.partial(_matmul_body, has_bias=bias is not None, act=act, n_li=n_li),
        grid=(M // tm, N // tn),
        in_specs=in_specs,
        out_specs=pl.BlockSpec((tm, tn), lambda i, j: (i, j)),
        out_shape=jax.ShapeDtypeStruct((M, N), out_dtype),
        compiler_params=_params("parallel", "parallel"),
        name="matmul",
    )(*args)


CONV_HALO = 8
POOL_HALO = 16


def _conv_body(xb_ref, st_ref, w_ref, b_ref, xc_ref, ext_ref, *, tb):
    c = pl.program_id(1)
    nst = CONV_W - 1
    lo = CONV_HALO - nst

    @pl.when(c == 0)
    def _():
        ext_ref[:, lo:CONV_HALO, :] = st_ref[...]

    @pl.when(c > 0)
    def _():
        ext_ref[:, lo:CONV_HALO, :] = ext_ref[:, lo + tb:CONV_HALO + tb, :]

    ext_ref[:, CONV_HALO:CONV_HALO + tb, :] = xb_ref[...].astype(F32)
    acc = b_ref[...][None] + ext_ref[:, lo:lo + tb, :] * w_ref[0:1, :][None]
    for j in range(1, CONV_W):
        acc = acc + ext_ref[:, lo + j:lo + j + tb, :] * w_ref[j:j + 1, :][None]
    xc_ref[...] = _silu(acc).astype(xc_ref.dtype)


def conv_silu(xb, state, w, b):
    B, T, E = xb.shape
    bb, tb = (1, 256) if T % 256 == 0 else (B, T)
    return pl.pallas_call(
        functools.partial(_conv_body, tb=tb),
        grid=(B // bb, T // tb),
        in_specs=[pl.BlockSpec((bb, tb, E), lambda i, c: (i, c, 0)),
                  pl.BlockSpec((bb, CONV_W - 1, E), lambda i, c: (i, 0, 0)),
                  pl.BlockSpec((CONV_W, E), lambda i, c: (0, 0)),
                  pl.BlockSpec((1, E), lambda i, c: (0, 0))],
        out_specs=pl.BlockSpec((bb, tb, E), lambda i, c: (i, c, 0)),
        out_shape=jax.ShapeDtypeStruct((B, T, E), BF16),
        scratch_shapes=[pltpu.VMEM((bb, CONV_HALO + tb, E), F32)],
        compiler_params=_params("parallel", "arbitrary"),
        name="conv_silu",
    )(xb, state.astype(F32), w, b.reshape(1, E))


def _mlstm_body(q_ref, k_ref, v_ref, gc_ref, gr_ref, c0_ref, n0_ref, m0_ref, ng_ref,
                h_ref, c_out_ref, n_out_ref, m_out_ref, c_s, n_s, m_s, *, nh, dqk, dv, lc):
    c = pl.program_id(1)

    @pl.when(c == 0)
    def _():
        c_s[...] = c0_ref[0]
        n_s[...] = n0_ref[0]
        m_s[...] = m0_ref[0]

    row = lax.broadcasted_iota(jnp.int32, (lc, lc), 0)
    col = lax.broadcasted_iota(jnp.int32, (lc, lc), 1)
    causal = row >= col
    scale = dqk ** -0.5
    gc = gc_ref[0]
    gr = gr_ref[0]
    for h in range(nh):
        q = (q_ref[0, :, h * dqk:(h + 1) * dqk].astype(F32) * scale).astype(BF16)
        k = k_ref[0, :, h * dqk:(h + 1) * dqk]
        v = v_ref[0, :, h * dv:(h + 1) * dv]
        li_c, lf_c = gc[:, h:h + 1], gc[:, nh + h:nh + h + 1]
        li_r, lf_r = gr[h:h + 1, :], gr[nh + h:nh + h + 1, :]
        b_c = jnp.sum(jnp.where(causal, lf_r, 0.0), axis=1, keepdims=True)
        b_r = jnp.sum(jnp.where(row <= col, lf_c, 0.0), axis=0, keepdims=True)
        m_prev = m_s[h:h + 1, 0:1]
        dm = jnp.where(causal, b_c - b_r + li_r, NEG)
        m_inter = b_c + m_prev
        m_t = jnp.maximum(m_inter, jnp.max(dm, axis=1, keepdims=True))
        a = _dot_nt(q, k) * jnp.exp(dm - m_t)
        inter = jnp.exp(m_inter - m_t)
        c_prev = c_s[h]
        n_prev = n_s[h:h + 1, :]
        num = _dot(a.astype(BF16), v) + inter * _dot(q, c_prev.astype(BF16))
        qn = jnp.sum(q.astype(F32) * n_prev, axis=1, keepdims=True)
        den = jnp.sum(a, axis=1, keepdims=True) + inter * qn
        hh = num / jnp.maximum(jnp.abs(den), jnp.exp(-m_t))
        hn = hh * lax.rsqrt(jnp.mean(hh * hh, axis=-1, keepdims=True) + EPS) \
            * ng_ref[:, h * dv:(h + 1) * dv]
        h_ref[0, :, h * dv:(h + 1) * dv] = hn.astype(h_ref.dtype)
        b_last = b_c[lc - 1:lc, :]
        g_r = b_last - b_r + li_r
        g_c = b_last - b_c + li_c
        m_new = jnp.maximum(b_last + m_prev, jnp.max(g_r, axis=1, keepdims=True))
        w_r = jnp.exp(g_r - m_new)
        w_c = jnp.exp(g_c - m_new)
        decay = jnp.exp(b_last + m_prev - m_new)
        wv = (w_c * v.astype(F32)).astype(BF16)
        c_s[h] = decay * c_prev + _dot_tn(k, wv)
        wr8 = jnp.broadcast_to(w_r, (SUBLANES, lc)).astype(BF16)
        n_s[h:h + 1, :] = decay * n_prev + _dot(wr8, k)[0:1, :]
        m_s[h:h + 1, :] = jnp.broadcast_to(m_new, (1, LANES))

    @pl.when(c == pl.num_programs(1) - 1)
    def _():
        c_out_ref[0] = c_s[...]
        n_out_ref[0] = n_s[...]
        m_out_ref[0] = m_s[...]


def mlstm_recurrence(q, k, v, gcol, grow, c0, n0, m0, norm_g):
    B, T, _ = q.shape
    nh, dqk, dv = c0.shape[1], c0.shape[2], c0.shape[3]
    lc = min(MLSTM_CHUNK, T)
    m0b = jnp.broadcast_to(m0[:, :, None], (B, nh, LANES)).astype(F32)
    h, c_new, n_new, m_new = pl.pallas_call(
        functools.partial(_mlstm_body, nh=nh, dqk=dqk, dv=dv, lc=lc),
        grid=(B, T // lc),
        in_specs=[pl.BlockSpec((1, lc, nh * dqk), lambda b, c: (b, c, 0)),
                  pl.BlockSpec((1, lc, nh * dqk), lambda b, c: (b, c, 0)),
                  pl.BlockSpec((1, lc, nh * dv), lambda b, c: (b, c, 0)),
                  pl.BlockSpec((1, lc, LANES), lambda b, c: (b, c, 0)),
                  pl.BlockSpec((1, 2 * nh, lc), lambda b, c: (b, 0, c)),
                  pl.BlockSpec((1, nh, dqk, dv), lambda b, c: (b, 0, 0, 0)),
                  pl.BlockSpec((1, nh, dqk), lambda b, c: (b, 0, 0)),
                  pl.BlockSpec((1, nh, LANES), lambda b, c: (b, 0, 0)),
                  pl.BlockSpec((1, nh * dv), lambda b, c: (0, 0))],
        out_specs=[pl.BlockSpec((1, lc, nh * dv), lambda b, c: (b, c, 0)),
                   pl.BlockSpec((1, nh, dqk, dv), lambda b, c: (b, 0, 0, 0)),
                   pl.BlockSpec((1, nh, dqk), lambda b, c: (b, 0, 0)),
                   pl.BlockSpec((1, nh, LANES), lambda b, c: (b, 0, 0))],
        out_shape=[jax.ShapeDtypeStruct((B, T, nh * dv), BF16),
                   jax.ShapeDtypeStruct((B, nh, dqk, dv), F32),
                   jax.ShapeDtypeStruct((B, nh, dqk), F32),
                   jax.ShapeDtypeStruct((B, nh, LANES), F32)],
        scratch_shapes=[pltpu.VMEM((nh, dqk, dv), F32),
                        pltpu.VMEM((nh, dqk), F32),
                        pltpu.VMEM((nh, LANES), F32)],
        compiler_params=_params("parallel", "arbitrary"),
        name="mlstm_recurrence",
    )(q, k, v, gcol, grow, c0.astype(F32), n0.astype(F32), m0b, norm_g.reshape(1, nh * dv))
    return h, c_new, n_new, m_new[:, :, 0]


def _pool_body(xb_ref, st_ref, wg_ref, sc_ref, y_ref, ext_ref, *, tb, pos0, nbuf):
    c = pl.program_id(1)
    lo = POOL_HALO - nbuf
    bb = xb_ref.shape[0]
    gw = wg_ref.shape[1]

    @pl.when(c == 0)
    def _():
        ext_ref[:, lo:POOL_HALO, :] = st_ref[...]

    @pl.when(c > 0)
    def _():
        ext_ref[:, lo:POOL_HALO, :] = ext_ref[:, lo + tb:POOL_HALO + tb, :]

    ext_ref[:, POOL_HALO:POOL_HALO + tb, :] = xb_ref[...].astype(F32)
    pos = pos0 + c * tb + lax.broadcasted_iota(jnp.int32, (1, tb, 1), 1)
    for g, w in enumerate(POOL_WINDOWS):
        cols = slice(g * gw, (g + 1) * gw)
        cur = ext_ref[:, POOL_HALO:POOL_HALO + tb, cols]
        win = cur
        for i in range(1, w):
            win = win + ext_ref[:, POOL_HALO - i:POOL_HALO - i + tb, cols]
        cnt = jnp.minimum(w, pos + 1).astype(F32)
        d = (win / cnt - cur).reshape(bb * tb, gw).astype(BF16)
        yg = _dot(d, wg_ref[g]) * sc_ref[:, cols]
        y_ref[:, :, cols] = yg.reshape(bb, tb, gw).astype(y_ref.dtype)


def pool_mix(xb, state, w_grp, scale, pos0):
    B, T, E = xb.shape
    nbuf = state.shape[1]
    bb, tb = (1, 256) if T % 256 == 0 else (B, T)
    ng, gw = w_grp.shape[0], w_grp.shape[1]
    return pl.pallas_call(
        functools.partial(_pool_body, tb=tb, pos0=pos0, nbuf=nbuf),
        grid=(B // bb, T // tb),
        in_specs=[pl.BlockSpec((bb, tb, E), lambda i, c: (i, c, 0)),
                  pl.BlockSpec((bb, nbuf, E), lambda i, c: (i, 0, 0)),
                  pl.BlockSpec((ng, gw, gw), lambda i, c: (0, 0, 0)),
                  pl.BlockSpec((1, E), lambda i, c: (0, 0))],
        out_specs=pl.BlockSpec((bb, tb, E), lambda i, c: (i, c, 0)),
        out_shape=jax.ShapeDtypeStruct((B, T, E), BF16),
        scratch_shapes=[pltpu.VMEM((bb, POOL_HALO + tb, E), F32)],
        compiler_params=_params("parallel", "arbitrary"),
        name="pool_mix",
    )(xb, state.astype(F32), w_grp, scale.reshape(1, E))


def _tail_body(*refs, kind, final):
    if kind == "mlstm":
        hn_ref, o_ref, xc_ref, z_ref, skip_ref = refs[:5]
        rest = refs[5:]
        mix = (o_ref[...].astype(F32) * hn_ref[...].astype(F32)
               + skip_ref[...] * xc_ref[...].astype(F32)) * _silu(z_ref[...].astype(F32))
    else:
        y_ref, z_ref = refs[:2]
        rest = refs[2:]
        mix = y_ref[...].astype(F32) * _silu(z_ref[...].astype(F32))
    x_ref, p_ref, wd_ref, gw_ref, pw_ref = rest[:5]
    rest = rest[5:]
    x1 = x_ref[...] + _dot(mix.astype(BF16), wd_ref[...])
    gate = _sigmoid(_dot(x1.astype(BF16), gw_ref[...]))
    x2 = x1 + gate * _dot(p_ref[...].astype(BF16), pw_ref[...])
    if final:
        fg_ref, xo_ref, yo_ref = rest
        yo_ref[...] = x2 * lax.rsqrt(jnp.mean(x2 * x2, axis=-1, keepdims=True) + EPS) * fg_ref[...]
    else:
        (xo_ref,) = rest
    xo_ref[...] = x2


def layer_tail(kind, mix_inputs, x, p, w_down, gate_w, ple_w, skip=None, final_g=None):
    M, D = x.shape
    tm = _pick(M, (256,))
    row = lambda i: (i, 0)
    fixed = lambda i: (0, 0)
    in_specs, args = [], []
    for a in mix_inputs:
        in_specs.append(pl.BlockSpec((tm, a.shape[1]), row))
        args.append(a)
    if kind == "mlstm":
        in_specs.append(pl.BlockSpec((1, skip.shape[-1]), fixed))
        args.append(skip.reshape(1, -1))
    in_specs += [pl.BlockSpec((tm, D), row), pl.BlockSpec((tm, p.shape[1]), row),
                 pl.BlockSpec(w_down.shape, fixed), pl.BlockSpec(gate_w.shape, fixed),
                 pl.BlockSpec(ple_w.shape, fixed)]
    args += [x, p, w_down, gate_w, ple_w]
    out_specs = [pl.BlockSpec((tm, D), row)]
    out_shape = [jax.ShapeDtypeStruct((M, D), F32)]
    if final_g is not None:
        in_specs.append(pl.BlockSpec((1, D), fixed))
        args.append(final_g.reshape(1, D))
        out_specs.append(pl.BlockSpec((tm, D), row))
        out_shape.append(jax.ShapeDtypeStruct((M, D), F32))
    return pl.pallas_call(
        functools.partial(_tail_body, kind=kind, final=final_g is not None),
        grid=(M // tm,),
        in_specs=in_specs, out_specs=out_specs, out_shape=out_shape,
        compiler_params=_params("parallel"),
        name="layer_tail_" + kind,
    )(*args)


def _sortable_key(s):
    s = jnp.where(s == 0.0, 0.0, s)
    bits = lax.bitcast_convert_type(s, jnp.int32)
    return jnp.where(bits < 0, bits ^ jnp.int32(0x7FFFFFFF), bits)


def _count(keys_ref, nvalid, pred):
    rows, cw = keys_ref.shape[1], keys_ref.shape[2]

    def body(kc, acc):
        hit = jnp.where(pred(keys_ref[kc], kc), 1.0, 0.0)
        part = hit[:, 0:LANES]
        for g in range(1, cw // LANES):
            part = part + hit[:, g * LANES:(g + 1) * LANES]
        return acc + part

    acc = lax.fori_loop(0, nvalid, body, jnp.zeros((rows, LANES), F32))
    return jnp.sum(acc, axis=1, keepdims=True)


def _select_topk(keys_ref, cidx_ref, nvalid, k_top, idx_bits):
    rows, cw = keys_ref.shape[1], keys_ref.shape[2]
    kf = float(k_top)
    imin = jnp.int32(INT_MIN)

    def bit_body(i, tau_u):
        cand_u = tau_u | jnp.left_shift(jnp.int32(1), 31 - i)
        cand = cand_u ^ imin
        cnt = _count(keys_ref, nvalid, lambda kk, kc: kk >= cand)
        return jnp.where(cnt >= kf, cand_u, tau_u)

    tau_u = lax.fori_loop(0, 32, bit_body, jnp.zeros((rows, 1), jnp.int32))
    tau = tau_u ^ imin
    n_gt = _count(keys_ref, nvalid, lambda kk, kc: kk > tau)
    n_ge = _count(keys_ref, nvalid, lambda kk, kc: kk >= tau)
    need = kf - n_gt
    cidx_ref[...] = jnp.full(cidx_ref.shape, 2 ** 30, jnp.int32)
    excess = jnp.max(jnp.where((n_ge > kf) & (tau_u != 0), 1.0, 0.0))

    @pl.when(excess > 0.5)
    def _():
        lane = lax.broadcasted_iota(jnp.int32, (1, cw), 1)

        def idx_body(i, cut):
            cand = cut | jnp.left_shift(jnp.int32(1), idx_bits - 1 - i)
            cnt = _count(keys_ref, nvalid,
                         lambda kk, kc: (kk == tau) & ((kc * cw + lane) < cand))
            return jnp.where(cnt < need, cand, cut)

        cut = lax.fori_loop(0, idx_bits, idx_body, jnp.zeros((rows, 1), jnp.int32))
        cidx_ref[...] = jnp.broadcast_to(cut, cidx_ref.shape)

    return tau


def _selected(kk, colv, rowpos, tau, cut):
    return (colv <= rowpos) & ((kk > tau) | ((kk == tau) & (colv <= cut)))


def _n_causal_chunks(qb, tq, tk):
    return lax.div(qb * tq + (tq - 1), jnp.int32(tk)) + 1


def _idx_prompt_body(qi_ref, wi_ref, ki_ref, bias_ref, keys_ref, cidx_ref, *,
                     nh, dh, tq, tk, nk, k_top, idx_bits):
    qb = pl.program_id(1)
    nvalid = _n_causal_chunks(qb, tq, tk)
    rowpos = qb * tq + lax.broadcasted_iota(jnp.int32, (tq, 1), 0)
    lane = lax.broadcasted_iota(jnp.int32, (1, tk), 1)
    wi = wi_ref[0]

    def score_chunk(kc, carry):
        kt = ki_ref[0, kc]
        s = jnp.zeros((tq, tk), F32)
        for h in range(nh):
            rel = jnp.maximum(_dot_nt(qi_ref[0, :, h * dh:(h + 1) * dh], kt), 0.0)
            s = s + rel * wi[:, h:h + 1]
        colv = kc * tk + lane
        keys_ref[kc] = jnp.where(colv <= rowpos, _sortable_key(s), jnp.int32(INT_MIN))
        return carry

    lax.fori_loop(0, nvalid, score_chunk, 0)
    tau = _select_topk(keys_ref, cidx_ref, nvalid, k_top, idx_bits)
    cut = cidx_ref[:, 0:1]

    def write_chunk(kc, carry):
        sel = _selected(keys_ref[kc], kc * tk + lane, rowpos, tau, cut)
        bias_ref[0, 0, kc] = jnp.where(sel, 0.0, NEG).astype(bias_ref.dtype)
        return carry

    lax.fori_loop(0, nvalid, write_chunk, 0)

    def fill_chunk(kc, carry):
        bias_ref[0, 0, kc] = jnp.full((tq, tk), NEG, bias_ref.dtype)
        return carry

    lax.fori_loop(nvalid, nk, fill_chunk, 0)


def indexer_bias_prompt(qi, wi, ki, k_top):
    B, T, _ = qi.shape
    dh = ki.shape[-1]
    nh = qi.shape[-1] // dh
    tq, tk = min(ATT_TQ, T), min(ATT_TK, T)
    nq, nk = T // tq, T // tk
    idx_bits = int(T).bit_length() + 1
    return pl.pallas_call(
        functools.partial(_idx_prompt_body, nh=nh, dh=dh, tq=tq, tk=tk, nk=nk, k_top=k_top,
                          idx_bits=idx_bits),
        grid=(B, nq),
        in_specs=[pl.BlockSpec((1, tq, nh * dh), lambda b, i: (b, i, 0)),
                  pl.BlockSpec((1, tq, LANES), lambda b, i: (b, i, 0)),
                  pl.BlockSpec((1, nk, tk, dh), lambda b, i: (b, 0, 0, 0))],
        out_specs=pl.BlockSpec((1, 1, nk, tq, tk), lambda b, i: (b, i, 0, 0, 0)),
        out_shape=jax.ShapeDtypeStruct((B, nq, nk, tq, tk), BF16),
        scratch_shapes=[pltpu.VMEM((nk, tq, tk), jnp.int32),
                        pltpu.VMEM((tq, LANES), jnp.int32)],
        compiler_params=_params("parallel", "arbitrary"),
        name="indexer_bias_prompt",
    )(qi, wi, ki.reshape(B, nk, tk, dh))


def _attn_prompt_body(q_ref, k_ref, v_ref, bias_ref, o_ref, m_s, l_s, acc_s, *, nh, tq, tk):
    qb, kb = pl.program_id(1), pl.program_id(2)
    nvalid = _n_causal_chunks(qb, tq, tk)

    @pl.when(kb == 0)
    def _():
        m_s[...] = jnp.full(m_s.shape, NEG, F32)
        l_s[...] = jnp.zeros(l_s.shape, F32)
        acc_s[...] = jnp.zeros(acc_s.shape, F32)

    @pl.when(kb < nvalid)
    def _():
        bias = bias_ref[0, 0, 0].astype(F32)

        def head(h, carry):
            s = _dot_nt(q_ref[0, h], k_ref[0, h]) + bias
            m_prev = m_s[h]
            m_new = jnp.maximum(m_prev, jnp.max(s, axis=1, keepdims=True))
            alpha = jnp.exp(m_prev - m_new)
            p = jnp.exp(s - m_new)
            l_s[h] = alpha * l_s[h] + jnp.sum(p, axis=1, keepdims=True)
            acc_s[h] = alpha * acc_s[h] + _dot(p.astype(BF16), v_ref[0, h])
            m_s[h] = m_new
            return carry

        lax.fori_loop(0, nh, head, 0)

    @pl.when(kb == pl.num_programs(2) - 1)
    def _():
        for h in range(nh):
            o_ref[0, h] = (acc_s[h] / l_s[h]).astype(o_ref.dtype)


def attention_prompt(q, k, v, bias):
    B, nh, T, dh = q.shape
    _, nq, nk, tq, tk = bias.shape

    def kv_map(b, i, j):
        return (b, 0, jnp.minimum(j, _n_causal_chunks(i, tq, tk) - 1), 0)

    def bias_map(b, i, j):
        return (b, i, jnp.minimum(j, _n_causal_chunks(i, tq, tk) - 1), 0, 0)

    return pl.pallas_call(
        functools.partial(_attn_prompt_body, nh=nh, tq=tq, tk=tk),
        grid=(B, nq, nk),
        in_specs=[pl.BlockSpec((1, nh, tq, dh), lambda b, i, j: (b, 0, i, 0)),
                  pl.BlockSpec((1, nh, tk, dh), kv_map),
                  pl.BlockSpec((1, nh, tk, dh), kv_map),
                  pl.BlockSpec((1, 1, 1, tq, tk), bias_map)],
        out_specs=pl.BlockSpec((1, nh, tq, dh), lambda b, i, j: (b, 0, i, 0)),
        out_shape=jax.ShapeDtypeStruct((B, nh, T, dh), BF16),
        scratch_shapes=[pltpu.VMEM((nh, tq, 1), F32),
                        pltpu.VMEM((nh, tq, 1), F32),
                        pltpu.VMEM((nh, tq, dh), F32)],
        compiler_params=_params("parallel", "parallel", "arbitrary"),
        name="attention_prompt",
    )(q, k, v, bias)


def _idx_sample_body(pt_ref, qi_ref, wi_ref, knew_ref, *rest, nh, ts, npg, nsteps, k_top,
                     idx_bits):
    page_refs = rest[:npg]
    bias_ref, keys_ref, cidx_ref = rest[npg:]
    s = pl.program_id(1)
    cw = keys_ref.shape[2]
    wi = wi_ref[0]
    qi = qi_ref[0]

    def scores(kt):
        full = jnp.maximum(_dot_nt(qi, kt), 0.0)
        out = jnp.zeros((ts, cw), F32)
        for h in range(nh):
            out = out + full[h * ts:(h + 1) * ts, :] * wi[:, h:h + 1]
        return out

    kt = jnp.concatenate([r[0] for r in page_refs], axis=0).astype(BF16)
    keys_ref[s] = _sortable_key(scores(kt))

    @pl.when(s == nsteps - 1)
    def _():
        lane = lax.broadcasted_iota(jnp.int32, (1, cw), 1)
        past = nsteps * cw
        rowpos = past + lax.broadcasted_iota(jnp.int32, (ts, 1), 0)
        colv_new = past + lane
        keys_ref[nsteps] = jnp.where(colv_new <= rowpos, _sortable_key(scores(knew_ref[0])),
                                     jnp.int32(INT_MIN))
        tau = _select_topk(keys_ref, cidx_ref, nsteps + 1, k_top, idx_bits)
        cut = cidx_ref[:, 0:1]
        for kc in range(nsteps + 1):
            sel = _selected(keys_ref[kc], kc * cw + lane, rowpos, tau, cut)
            bias_ref[0, kc] = jnp.where(sel, 0.0, NEG).astype(bias_ref.dtype)


def _page_specs(n_pages_per_step, page, width, first_step_offset):
    specs = []
    for i in range(n_pages_per_step):
        def imap(b, s, pt, i=i):
            step = jnp.maximum(s - first_step_offset, 0)
            return (pt[b, step * n_pages_per_step + i], 0, 0)
        specs.append(pl.BlockSpec((1, page, width), imap))
    return specs


def indexer_bias_sample(qi_hq, wi, ki_new_pad, cache_kidx, page_table, k_top):
    B, n_pages = page_table.shape
    _, page, dh = cache_kidx.shape
    ts = wi.shape[1]
    nh = qi_hq.shape[1] // ts
    npg = min(PAGES_PER_STEP, n_pages)
    nsteps = n_pages // npg
    cw = npg * page
    idx_bits = int(n_pages * page + cw).bit_length() + 1
    grid_spec = pltpu.PrefetchScalarGridSpec(
        num_scalar_prefetch=1,
        grid=(B, nsteps),
        in_specs=[pl.BlockSpec((1, nh * ts, dh), lambda b, s, pt: (b, 0, 0)),
                  pl.BlockSpec((1, ts, LANES), lambda b, s, pt: (b, 0, 0)),
                  pl.BlockSpec((1, cw, dh), lambda b, s, pt: (b, 0, 0))]
        + _page_specs(npg, page, dh, 0),
        out_specs=pl.BlockSpec((1, nsteps + 1, ts, cw), lambda b, s, pt: (b, 0, 0, 0)),
        scratch_shapes=[pltpu.VMEM((nsteps + 1, ts, cw), jnp.int32),
                        pltpu.VMEM((ts, LANES), jnp.int32)],
    )
    return pl.pallas_call(
        functools.partial(_idx_sample_body, nh=nh, ts=ts, npg=npg, nsteps=nsteps, k_top=k_top,
                          idx_bits=idx_bits),
        grid_spec=grid_spec,
        out_shape=jax.ShapeDtypeStruct((B, nsteps + 1, ts, cw), F32),
        compiler_params=_params("parallel", "arbitrary"),
        name="indexer_bias_sample",
    )(page_table, qi_hq, wi, ki_new_pad, *([cache_kidx] * npg))


def _attn_sample_body(pt_ref, qbd_ref, knew_ref, vnew_ref, bias_ref, *rest, nh, ts, npg, dh):
    kpages, vpages = rest[:npg], rest[npg:2 * npg]
    o_ref, m_s, l_s, acc_s = rest[2 * npg:]
    s = pl.program_id(1)
    rows = nh * ts

    @pl.when(s == 0)
    def _():
        m_s[...] = jnp.full(m_s.shape, NEG, F32)
        l_s[...] = jnp.zeros(l_s.shape, F32)
        acc_s[...] = jnp.zeros(acc_s.shape, F32)

    def update(kt, vt):
        bias = bias_ref[0, 0]
        logits = _dot_nt(qbd_ref[0], kt) + jnp.concatenate([bias] * nh, axis=0)
        m_prev = m_s[...]
        m_new = jnp.maximum(m_prev, jnp.max(logits, axis=1, keepdims=True))
        alpha = jnp.exp(m_prev - m_new)
        p = jnp.exp(logits - m_new)
        l_s[...] = alpha * l_s[...] + jnp.sum(p, axis=1, keepdims=True)
        acc_s[...] = alpha * acc_s[...] + _dot(p.astype(BF16), vt)
        m_s[...] = m_new

    @pl.when(s == 0)
    def _():
        update(knew_ref[0], vnew_ref[0])

    @pl.when(s > 0)
    def _():
        kt = jnp.concatenate([r[0] for r in kpages], axis=0).astype(BF16)
        vt = jnp.concatenate([r[0] for r in vpages], axis=0).astype(BF16)
        update(kt, vt)

    @pl.when(s == pl.num_programs(1) - 1)
    def _():
        o = acc_s[...] / l_s[...]
        lane = lax.broadcasted_iota(jnp.int32, (1, nh * dh), 1)
        res = jnp.zeros((ts, nh * dh), F32)
        for h in range(nh):
            own = (lane >= h * dh) & (lane < (h + 1) * dh)
            res = res + jnp.where(own, o[h * ts:(h + 1) * ts, :], 0.0)
        o_ref[0] = res.astype(o_ref.dtype)


def attention_sample(qbd, k_new_pad, v_new_pad, bias, cache_k, cache_v, page_table, nh, ts):
    B, n_pages = page_table.shape
    _, page, width = cache_k.shape
    dh = width // nh
    nsteps1, cw = bias.shape[1], bias.shape[3]
    npg = cw // page
    rows = nh * ts

    def bias_map(b, s, pt):
        return (b, jnp.where(s == 0, nsteps1 - 1, s - 1), 0, 0)

    grid_spec = pltpu.PrefetchScalarGridSpec(
        num_scalar_prefetch=1,
        grid=(B, nsteps1),
        in_specs=[pl.BlockSpec((1, rows, width), lambda b, s, pt: (b, 0, 0)),
                  pl.BlockSpec((1, cw, width), lambda b, s, pt: (b, 0, 0)),
                  pl.BlockSpec((1, cw, width), lambda b, s, pt: (b, 0, 0)),
                  pl.BlockSpec((1, 1, ts, cw), bias_map)]
        + _page_specs(npg, page, width, 1) + _page_specs(npg, page, width, 1),
        out_specs=pl.BlockSpec((1, ts, width), lambda b, s, pt: (b, 0, 0)),
        scratch_shapes=[pltpu.VMEM((rows, 1), F32),
                        pltpu.VMEM((rows, 1), F32),
                        pltpu.VMEM((rows, width), F32)],
    )
    return pl.pallas_call(
        functools.partial(_attn_sample_body, nh=nh, ts=ts, npg=npg, dh=dh),
        grid_spec=grid_spec,
        out_shape=jax.ShapeDtypeStruct((B, ts, width), BF16),
        compiler_params=_params("parallel", "arbitrary"),
        name="attention_sample",
    )(page_table, qbd, k_new_pad, v_new_pad, bias, *([cache_k] * npg), *([cache_v] * npg))


def _pad_time(a, t_pad, value=0.0):
    return jnp.pad(a, ((0, 0), (0, t_pad - a.shape[1]), (0, 0)), constant_values=value)


def _mlstm_layer(x, g, B, T, state, w, j):
    c0, n0, m0, conv0 = state
    nh = c0.shape[1]
    E = w["a_conv_w"].shape[-1]
    w_up = w["a_w_up"][j]
    xb, = norm_matmul(x, g, w_up[:, :E], (BF16,))
    z, = norm_matmul(x, g, w_up[:, E:], (BF16,))
    xc = conv_silu(xb.reshape(B, T, E), conv0, w["a_conv_w_f32"][j], w["a_conv_b"][j]).reshape(B * T, E)
    q = matmul(xc, w["a_w_q"][j], BF16)
    k = matmul(xc, w["a_w_k"][j], BF16)
    v = matmul(xb, w["a_w_v"][j], BF16)
    o = matmul(xb, w["a_w_o"][j], BF16, bias=w["a_b_o"][j], act="sigmoid")
    gates = matmul(xc, w["a_w_if_pad"][j], F32, bias=w["a_b_if_pad"][j], act="gates", n_li=nh)
    gcol = gates.reshape(B, T, LANES)
    q3, k3, v3 = q.reshape(B, T, -1), k.reshape(B, T, -1), v.reshape(B, T, -1)
    tp = T if T % LANES == 0 else ((T + LANES - 1) // LANES) * LANES
    if tp != T:
        q3, k3, v3 = _pad_time(q3, tp), _pad_time(k3, tp), _pad_time(v3, tp)
        pad_row = jnp.where(jnp.arange(LANES) < nh, NEG, 0.0).astype(F32)
        gcol = jnp.concatenate([gcol, jnp.broadcast_to(pad_row, (B, tp - T, LANES))], axis=1)
    grow = jnp.swapaxes(gcol[:, :, :2 * nh], 1, 2)
    hn, c_new, n_new, m_new = mlstm_recurrence(q3, k3, v3, gcol, grow, c0, n0, m0, w["a_norm_g"][j])
    hn = hn[:, :T].reshape(B * T, -1)
    assert T >= CONV_W - 1
    conv_new = xb.reshape(B, T, E)[:, T - (CONV_W - 1):].astype(F32)
    return (hn, o, xc, z), (c_new, n_new, m_new, conv_new)


def _rope_tables(pos):
    half = 32
    inv = ROPE_THETA ** (-np.arange(half, dtype=np.float64) / half)
    ang = np.asarray(pos, np.float64)[:, None] * inv[None, :]
    cos = np.tile(np.cos(ang), (1, 4))
    sin = np.tile(np.concatenate([-np.sin(ang), np.sin(ang)], axis=1), (1, 2))
    return jnp.asarray(cos, F32), jnp.asarray(sin, F32)


def _dsa_project(x, g, B, T, pos, w, j, nh, dh, nhi, di):
    M = B * T
    cos, sin = _rope_tables(pos)
    tm = _pick(M, (512, 256))
    if T % tm == 0:
        tabs, tab_blocks = (cos, sin), T // tm
    else:
        tabs, tab_blocks = (jnp.tile(cos, (M // T, 1)), jnp.tile(sin, (M // T, 1))), M // tm
    rope = dict(rope_tabs=tabs, tab_blocks=tab_blocks)
    q, = norm_matmul(x, g, w["b_wq"][j], (BF16,), scale=dh ** -0.5, **rope)
    k32, k16 = norm_matmul(x, g, w["b_wk"][j], (F32, BF16), **rope)
    v32, v16 = norm_matmul(x, g, w["b_wv"][j], (F32, BF16))
    gate, = norm_matmul(x, g, w["b_wg"][j], (BF16,))
    qi, = norm_matmul(x, g, w["b_wqi"][j], (BF16,), **rope)
    ki32, = norm_matmul(x, g, w["b_wki_pad"][j], (F32,), **rope)
    wi, = norm_matmul(x, g, w["b_wwi_pad"][j], (F32,), scale=nhi ** -0.5 * di ** -0.5)
    return q, k32, k16, v32, v16, gate, qi, ki32[:, :di], wi


def _dsa_layer_prompt(x, g, B, T, w, j, dims):
    nh, dh, nhi, di = dims
    q, k32, k16, v32, v16, gate, qi, ki32, wi = _dsa_project(x, g, B, T, np.arange(T), w, j, *dims)
    k_top = min(TOPK_MAX, T // 4)
    bias = indexer_bias_prompt(qi.reshape(B, T, nhi * di), wi.reshape(B, T, LANES),
                               ki32.astype(BF16).reshape(B, T, di), k_top)
    heads = lambda a: jnp.transpose(a.reshape(B, T, nh, dh), (0, 2, 1, 3))
    attn = attention_prompt(heads(q), heads(k16), heads(v16), bias)
    attn = jnp.transpose(attn, (0, 2, 1, 3)).reshape(B * T, nh * dh)
    new = (k32.reshape(B, T, nh, dh), v32.reshape(B, T, nh, dh), ki32.reshape(B, T, di))
    return (attn, gate), new


def _dsa_layer_sample(x, g, B, T, w, j, dims, ctx):
    nh, dh, nhi, di = dims
    cache_k, cache_v, cache_kidx, page_table = ctx
    n_pages = page_table.shape[1]
    page = cache_kidx.shape[2]
    past = n_pages * page
    q, k32, k16, v32, v16, gate, qi, ki32, wi = _dsa_project(
        x, g, B, T, past + np.arange(T), w, j, *dims)
    k_top = min(TOPK_MAX, (past + T) // 4)
    cw = min(PAGES_PER_STEP, n_pages) * page
    qi_hq = jnp.transpose(qi.reshape(B, T, nhi, di), (0, 2, 1, 3)).reshape(B, nhi * T, di)
    ki_new = _pad_time(ki32.astype(BF16).reshape(B, T, di), cw)
    bias = indexer_bias_sample(qi_hq, wi.reshape(B, T, LANES), ki_new,
                               cache_kidx[j], page_table, k_top)
    q3 = q.reshape(B, T, nh * dh)
    own = (jnp.arange(nh * T)[:, None] // T) == (jnp.arange(nh * dh)[None, :] // dh)
    qbd = jnp.where(own[None], jnp.tile(q3, (1, nh, 1)), 0).astype(BF16)
    width = nh * dh
    attn = attention_sample(qbd, _pad_time(k16.reshape(B, T, width), cw),
                            _pad_time(v16.reshape(B, T, width), cw), bias,
                            cache_k[j].reshape(-1, page, width), cache_v[j].reshape(-1, page, width),
                            page_table, nh, T)
    new = (k32.reshape(B, T, nh, dh), v32.reshape(B, T, nh, dh), ki32.reshape(B, T, di))
    return (attn.reshape(B * T, width), gate), new


def _pool_layer(x, g, B, T, buf, pos0, w, j):
    E = w["c_scale"].shape[-1]
    w_up = w["c_w_up"][j]
    xb, = norm_matmul(x, g, w_up[:, :E], (BF16,))
    z, = norm_matmul(x, g, w_up[:, E:], (BF16,))
    xb3 = xb.reshape(B, T, E)
    y = pool_mix(xb3, buf, w["c_w_grp"][j], w["c_scale"][j], pos0).reshape(B * T, E)
    nbuf = buf.shape[1]
    if T >= nbuf:
        new_buf = xb3[:, T - nbuf:].astype(F32)
    else:
        new_buf = jnp.concatenate([buf[:, T:].astype(F32), xb3.astype(F32)], axis=1)
    return (y, z), new_buf


def _run_group(x3, p4, pos0, mlstm_state, dsa_ctx, pool_state, w, dims):
    B, T, D = x3.shape
    depth = p4.shape[0]
    x = x3.reshape(B * T, D)
    new_a, new_b, new_c = [], [], []
    y = None
    for i in range(depth):
        kind, j = i % 3, i // 3
        g = w["norm_g"][i]
        final_g = w["final_g"] if i == depth - 1 else None
        p = p4[i].reshape(B * T, -1)
        if kind == 0:
            state = tuple(s[j] for s in mlstm_state)
            mix, st = _mlstm_layer(x, g, B, T, state, w, j)
            new_a.append(st)
            out = layer_tail("mlstm", mix, x, p, w["a_w_down"][j], w["ple_gate_w"][i], w["ple_w"][i],
                             skip=w["a_skip"][j], final_g=final_g)
        elif kind == 1:
            if dsa_ctx is None:
                mix, st = _dsa_layer_prompt(x, g, B, T, w, j, dims)
            else:
                mix, st = _dsa_layer_sample(x, g, B, T, w, j, dims, dsa_ctx)
            new_b.append(st)
            out = layer_tail("gated", mix, x, p, w["b_w_out"][j], w["ple_gate_w"][i], w["ple_w"][i],
                             final_g=final_g)
        else:
            mix, st = _pool_layer(x, g, B, T, pool_state[j], pos0, w, j)
            new_c.append(st)
            out = layer_tail("gated", mix, x, p, w["c_w_down"][j], w["ple_gate_w"][i], w["ple_w"][i],
                             final_g=final_g)
        x = out[0]
        if final_g is not None:
            y = out[1]
    a_states = tuple(jnp.stack([s[r] for s in new_a]) for r in range(4))
    b_states = tuple(jnp.stack([s[r] for s in new_b]) for r in range(3))
    c_state = jnp.stack(new_c)
    return y.reshape(B, T, D), a_states, b_states, c_state


def kernel(x_prompt, x_sample, state_mlstm_C, state_mlstm_n, state_mlstm_m, state_mlstm_conv, state_pool,
           cache_k, cache_v, cache_kidx, page_table, p_prompt, p_sample,
           norm_g, final_g, ple_w, ple_gate_w,
           a_w_up, a_conv_w, a_conv_b, a_w_q, a_w_k, a_w_v, a_w_if, a_b_if, a_w_o, a_b_o, a_norm_g,
           a_skip, a_w_down, b_w_in, b_w_out, c_w_up, c_w_grp, c_scale, c_w_down):
    D = x_prompt.shape[-1]
    nh, dh = cache_k.shape[3], cache_k.shape[4]
    di = cache_kidx.shape[-1]
    aw = nh * dh
    nhi = (b_w_in.shape[-1] - 4 * aw - di) // (di + 1)
    dims = (nh, dh, nhi, di)
    nha = state_mlstm_C.shape[2]
    bf = lambda a: a.astype(BF16)

    def pad_cols(a, n):
        return jnp.pad(a, [(0, 0)] * (a.ndim - 1) + [(0, n - a.shape[-1])])

    o1, o2, o3, o4, o5, o6 = np.cumsum([aw, aw, aw, aw, nhi * di, di])
    w = dict(
        norm_g=norm_g, final_g=final_g, ple_w=bf(ple_w), ple_gate_w=bf(ple_gate_w),
        a_w_up=bf(a_w_up), a_conv_w_f32=a_conv_w, a_conv_w=a_conv_w, a_conv_b=a_conv_b,
        a_w_q=bf(a_w_q), a_w_k=bf(a_w_k), a_w_v=bf(a_w_v), a_w_o=bf(a_w_o), a_b_o=a_b_o,
        a_w_if_pad=bf(pad_cols(a_w_if, LANES)), a_b_if_pad=pad_cols(a_b_if, LANES),
        a_norm_g=a_norm_g, a_skip=a_skip, a_w_down=bf(a_w_down),
        b_wq=bf(b_w_in[..., :o1]), b_wk=bf(b_w_in[..., o1:o2]), b_wv=bf(b_w_in[..., o2:o3]),
        b_wg=bf(b_w_in[..., o3:o4]), b_wqi=bf(b_w_in[..., o4:o5]),
        b_wki_pad=bf(pad_cols(b_w_in[..., o5:o6], LANES)),
        b_wwi_pad=bf(pad_cols(b_w_in[..., o6:], LANES)),
        b_w_out=bf(b_w_out), c_w_up=bf(c_w_up), c_w_grp=bf(c_w_grp), c_scale=c_scale,
        c_w_down=bf(c_w_down),
    )

    Bp = x_prompt.shape[0]
    na, nc = state_mlstm_C.shape[0], state_pool.shape[0]
    zeros_a = (jnp.zeros((na, Bp) + state_mlstm_C.shape[2:], F32),
               jnp.zeros((na, Bp) + state_mlstm_n.shape[2:], F32),
               jnp.zeros((na, Bp) + state_mlstm_m.shape[2:], F32),
               jnp.zeros((na, Bp) + state_mlstm_conv.shape[2:], F32))
    zeros_c = jnp.zeros((nc, Bp) + state_pool.shape[2:], F32)
    y_p, a_p, b_p, c_p = _run_group(x_prompt, p_prompt, 0, zeros_a, None, zeros_c, w, dims)

    past = page_table.shape[1] * cache_k.shape[2]
    y_s, a_s, b_s, c_s = _run_group(
        x_sample, p_sample, past,
        (state_mlstm_C, state_mlstm_n, state_mlstm_m, state_mlstm_conv),
        (cache_k, cache_v, cache_kidx, page_table), state_pool, w, dims)
    return (y_p, y_s, *a_p, *b_p, c_p, *a_s, *b_s, c_s)
```

```python
import functools
import math

import numpy as np
import jax
import jax.numpy as jnp
from jax import lax
from jax.experimental import pallas as pl
from jax.experimental.pallas import tpu as pltpu

EPS = 1e-6
ROPE_THETA = 10000.0
TOPK_MAX = 256
POOL_WINDOWS = (2, 4, 8, 16)
CONV_W = 4
MLSTM_CHUNK = 256
NEG = -1e30
LANES = 128
SUBLANES = 8
VMEM_LIMIT_BYTES = 56 * 1024 * 1024
ATT_TQ, ATT_TK = 256, 512
PAGES_PER_STEP = 8

F32 = jnp.float32
BF16 = jnp.bfloat16
INT_MIN = -2 ** 31


def _params(*sem):
    return pltpu.CompilerParams(dimension_semantics=sem, vmem_limit_bytes=VMEM_LIMIT_BYTES)


def _sigmoid(x):
    return 1.0 / (1.0 + jnp.exp(-x))


def _silu(x):
    return x * _sigmoid(x)


def _log_sigmoid(x):
    return jnp.minimum(x, 0.0) - jnp.log(1.0 + jnp.exp(-jnp.abs(x)))


def _dot(a, b):
    return jnp.dot(a, b, preferred_element_type=F32)


def _dot_nt(a, b):
    return lax.dot_general(a, b, (((1,), (1,)), ((), ())), preferred_element_type=F32)


def _dot_tn(a, b):
    return lax.dot_general(a, b, (((0,), (0,)), ((), ())), preferred_element_type=F32)


def _pick(n, cands):
    for c in cands:
        if n % c == 0:
            return c
    return n


def _rope_tile(a, cos, sin_signed):
    lane = lax.broadcasted_iota(jnp.int32, (1, LANES), 1)
    first_half = (lane % 64) < 32
    pieces = []
    for gi in range(a.shape[1] // LANES):
        ag = a[:, gi * LANES:(gi + 1) * LANES]
        ahead = pltpu.roll(ag, LANES - 32, 1)
        behind = pltpu.roll(ag, 32, 1)
        rot = jnp.where(first_half, ahead, behind)
        pieces.append(ag * cos + rot * sin_signed)
    return pieces[0] if len(pieces) == 1 else jnp.concatenate(pieces, axis=1)


def _norm_matmul_body(*refs, rope, n_out, scale, head_dim):
    if rope:
        x_ref, g_ref, w_ref, cos_ref, sin_ref = refs[:5]
        rest = refs[5:]
    else:
        x_ref, g_ref, w_ref = refs[:3]
        rest = refs[3:]
    outs, xn_ref = rest[:n_out], rest[n_out]

    @pl.when(pl.program_id(1) == 0)
    def _():
        xf = x_ref[...]
        y = xf * lax.rsqrt(jnp.mean(xf * xf, axis=-1, keepdims=True) + EPS) * g_ref[...]
        xn_ref[...] = y.astype(BF16)

    acc = _dot(xn_ref[...], w_ref[...])
    if rope:
        acc = _rope_tile(acc, cos_ref[...], sin_ref[...])
    if scale != 1.0:
        acc = acc * scale
    for o in outs:
        if len(o.shape) == 4:
            npad = o.shape[3] - head_dim
            if npad:
                lane = lax.broadcasted_iota(jnp.int32, (acc.shape[0], npad), 1)
                pad = jnp.where(lane == 0, 1.0, 0.0)
            for hh in range(o.shape[1]):
                piece = acc[:, hh * head_dim:(hh + 1) * head_dim]
                if npad:
                    piece = jnp.concatenate([piece, pad], axis=1)
                o[0, hh] = piece.astype(o.dtype)
        else:
            o[...] = acc.astype(o.dtype)


def norm_matmul(x, g, w, out_dtypes, rope_tabs=None, tab_blocks=1, scale=1.0, head_major=None):
    M, D = x.shape
    N = w.shape[1]
    tm = _pick(M, (512, 256))
    tn = _pick(N, (512, 256, 128))
    out_specs = [pl.BlockSpec((tm, tn), lambda i, j: (i, j)) for _ in out_dtypes]
    out_shape = [jax.ShapeDtypeStruct((M, N), dt) for dt in out_dtypes]
    head_dim = 0
    if head_major is not None:
        hb, ht, head_dim, flags = head_major
        assert ht % tm == 0
        tpb = ht // tm
        for idx, width in enumerate(flags):
            if width:
                out_specs[idx] = pl.BlockSpec((1, tn // head_dim, tm, width),
                                              lambda i, j: (i // tpb, j, i % tpb, 0))
                out_shape[idx] = jax.ShapeDtypeStruct((hb, N // head_dim, ht, width), out_dtypes[idx])
    in_specs = [pl.BlockSpec((tm, D), lambda i, j: (i, 0)),
                pl.BlockSpec((1, D), lambda i, j: (0, 0)),
                pl.BlockSpec((D, tn), lambda i, j: (0, j))]
    args = [x, g.reshape(1, D), w]
    if rope_tabs is not None:
        for t in rope_tabs:
            in_specs.append(pl.BlockSpec((tm, LANES), lambda i, j: (i % tab_blocks, 0)))
            args.append(t)
    return pl.pallas_call(
        functools.partial(_norm_matmul_body, rope=rope_tabs is not None, n_out=len(out_dtypes),
                          scale=scale, head_dim=head_dim),
        grid=(M // tm, N // tn),
        in_specs=in_specs,
        out_specs=out_specs,
        out_shape=out_shape,
        scratch_shapes=[pltpu.VMEM((tm, D), BF16)],
        compiler_params=_params("parallel", "arbitrary"),
        name="norm_matmul",
    )(*args)


def _matmul_body(*refs, has_bias, act, n_li):
    if has_bias:
        x_ref, w_ref, b_ref, o_ref = refs
    else:
        x_ref, w_ref, o_ref = refs
    acc = _dot(x_ref[...], w_ref[...])
    if has_bias:
        acc = acc + b_ref[...]
    if act == "sigmoid":
        acc = _sigmoid(acc)
    elif act == "gates":
        col = lax.broadcasted_iota(jnp.int32, acc.shape, 1)
        acc = jnp.where(col < n_li, acc, _log_sigmoid(acc))
    o_ref[...] = acc.astype(o_ref.dtype)


def matmul(x, w, out_dtype, bias=None, act=None, n_li=0):
    M, K = x.shape
    N = w.shape[1]
    tm = _pick(M, (512, 256))
    tn = _pick(N, (512, 256, 128))
    in_specs = [pl.BlockSpec((tm, K), lambda i, j: (i, 0)),
                pl.BlockSpec((K, tn), lambda i, j: (0, j))]
    args = [x, w]
    if bias is not None:
        in_specs.append(pl.BlockSpec((1, tn), lambda i, j: (0, j)))
        args.append(bias.reshape(1, N).astype(F32))
    return pl.pallas_call(
        functools.partial(_matmul_body, has_bias=bias is not None, act=act, n_li=n_li),
        grid=(M // tm, N // tn),
        in_specs=in_specs,
        out_specs=pl.BlockSpec((tm, tn), lambda i, j: (i, j)),
        out_shape=jax.ShapeDtypeStruct((M, N), out_dtype),
        compiler_params=_params("parallel", "parallel"),
        name="matmul",
    )(*args)


CONV_HALO = 8
POOL_HALO = 16


def _conv_body(xb_ref, st_ref, w_ref, b_ref, xc_ref, ext_ref, *, tb):
    c = pl.program_id(1)
    nst = CONV_W - 1
    lo = CONV_HALO - nst

    @pl.when(c == 0)
    def _():
        ext_ref[:, lo:CONV_HALO, :] = st_ref[...]

    @pl.when(c > 0)
    def _():
        ext_ref[:, lo:CONV_HALO, :] = ext_ref[:, lo + tb:CONV_HALO + tb, :]

    ext_ref[:, CONV_HALO:CONV_HALO + tb, :] = xb_ref[...].astype(F32)
    acc = b_ref[...][None] + ext_ref[:, lo:lo + tb, :] * w_ref[0:1, :][None]
    for j in range(1, CONV_W):
        acc = acc + ext_ref[:, lo + j:lo + j + tb, :] * w_ref[j:j + 1, :][None]
    xc_ref[...] = _silu(acc).astype(xc_ref.dtype)


def conv_silu(xb, state, w, b):
    B, T, E = xb.shape
    bb, tb = (1, 256) if T % 256 == 0 else (B, T)
    return pl.pallas_call(
        functools.partial(_conv_body, tb=tb),
        grid=(B // bb, T // tb),
        in_specs=[pl.BlockSpec((bb, tb, E), lambda i, c: (i, c, 0)),
                  pl.BlockSpec((bb, CONV_W - 1, E), lambda i, c: (i, 0, 0)),
                  pl.BlockSpec((CONV_W, E), lambda i, c: (0, 0)),
                  pl.BlockSpec((1, E), lambda i, c: (0, 0))],
        out_specs=pl.BlockSpec((bb, tb, E), lambda i, c: (i, c, 0)),
        out_shape=jax.ShapeDtypeStruct((B, T, E), BF16),
        scratch_shapes=[pltpu.VMEM((bb, CONV_HALO + tb, E), F32)],
        compiler_params=_params("parallel", "arbitrary"),
        name="conv_silu",
    )(xb, state.astype(F32), w, b.reshape(1, E))


def _mlstm_body(q_ref, k_ref, v_ref, gc_ref, gr_ref, c0_ref, n0_ref, m0_ref, ng_ref,
                h_ref, c_out_ref, n_out_ref, m_out_ref, c_s, n_s, m_s, *, nh, dqk, dv, lc):
    c = pl.program_id(1)

    @pl.when(c == 0)
    def _():
        c_s[...] = c0_ref[0]
        n_s[...] = n0_ref[0]
        m_s[...] = m0_ref[0]

    row = lax.broadcasted_iota(jnp.int32, (lc, lc), 0)
    col = lax.broadcasted_iota(jnp.int32, (lc, lc), 1)
    causal = row >= col
    scale = dqk ** -0.5
    gc = gc_ref[0]
    gr = gr_ref[0]
    for h in range(nh):
        q = (q_ref[0, :, h * dqk:(h + 1) * dqk].astype(F32) * scale).astype(BF16)
        k = k_ref[0, :, h * dqk:(h + 1) * dqk]
        v = v_ref[0, :, h * dv:(h + 1) * dv]
        li_c, lf_c = gc[:, h:h + 1], gc[:, nh + h:nh + h + 1]
        li_r, lf_r = gr[h:h + 1, :], gr[nh + h:nh + h + 1, :]
        b_c = jnp.sum(jnp.where(causal, lf_r, 0.0), axis=1, keepdims=True)
        b_r = jnp.sum(jnp.where(row <= col, lf_c, 0.0), axis=0, keepdims=True)
        m_prev = m_s[h:h + 1, 0:1]
        dm = jnp.where(causal, b_c - b_r + li_r, NEG)
        m_inter = b_c + m_prev
        m_t = jnp.maximum(m_inter, jnp.max(dm, axis=1, keepdims=True))
        a = _dot_nt(q, k) * jnp.exp(dm - m_t)
        inter = jnp.exp(m_inter - m_t)
        c_prev = c_s[h]
        n_prev = n_s[h:h + 1, :]
        num = _dot(a.astype(BF16), v) + inter * _dot(q, c_prev.astype(BF16))
        qn = jnp.sum(q.astype(F32) * n_prev, axis=1, keepdims=True)
        den = jnp.sum(a, axis=1, keepdims=True) + inter * qn
        hh = num / jnp.maximum(jnp.abs(den), jnp.exp(-m_t))
        hn = hh * lax.rsqrt(jnp.mean(hh * hh, axis=-1, keepdims=True) + EPS) \
            * ng_ref[:, h * dv:(h + 1) * dv]
        h_ref[0, :, h * dv:(h + 1) * dv] = hn.astype(h_ref.dtype)
        b_last = b_c[lc - 1:lc, :]
        g_r = b_last - b_r + li_r
        g_c = b_last - b_c + li_c
        m_new = jnp.maximum(b_last + m_prev, jnp.max(g_r, axis=1, keepdims=True))
        w_r = jnp.exp(g_r - m_new)
        w_c = jnp.exp(g_c - m_new)
        decay = jnp.exp(b_last + m_prev - m_new)
        wv = (w_c * v.astype(F32)).astype(BF16)
        c_s[h] = decay * c_prev + _dot_tn(k, wv)
        wr8 = jnp.broadcast_to(w_r, (SUBLANES, lc)).astype(BF16)
        n_s[h:h + 1, :] = decay * n_prev + _dot(wr8, k)[0:1, :]
        m_s[h:h + 1, :] = jnp.broadcast_to(m_new, (1, LANES))

    @pl.when(c == pl.num_programs(1) - 1)
    def _():
        c_out_ref[0] = c_s[...]
        n_out_ref[0] = n_s[...]
        m_out_ref[0] = m_s[...]


def mlstm_recurrence(q, k, v, gcol, grow, c0, n0, m0, norm_g):
    B, T, _ = q.shape
    nh, dqk, dv = c0.shape[1], c0.shape[2], c0.shape[3]
    lc = min(MLSTM_CHUNK, T)
    m0b = jnp.broadcast_to(m0[:, :, None], (B, nh, LANES)).astype(F32)
    h, c_new, n_new, m_new = pl.pallas_call(
        functools.partial(_mlstm_body, nh=nh, dqk=dqk, dv=dv, lc=lc),
        grid=(B, T // lc),
        in_specs=[pl.BlockSpec((1, lc, nh * dqk), lambda b, c: (b, c, 0)),
                  pl.BlockSpec((1, lc, nh * dqk), lambda b, c: (b, c, 0)),
                  pl.BlockSpec((1, lc, nh * dv), lambda b, c: (b, c, 0)),
                  pl.BlockSpec((1, lc, LANES), lambda b, c: (b, c, 0)),
                  pl.BlockSpec((1, 2 * nh, lc), lambda b, c: (b, 0, c)),
                  pl.BlockSpec((1, nh, dqk, dv), lambda b, c: (b, 0, 0, 0)),
                  pl.BlockSpec((1, nh, dqk), lambda b, c: (b, 0, 0)),
                  pl.BlockSpec((1, nh, LANES), lambda b, c: (b, 0, 0)),
                  pl.BlockSpec((1, nh * dv), lambda b, c: (0, 0))],
        out_specs=[pl.BlockSpec((1, lc, nh * dv), lambda b, c: (b, c, 0)),
                   pl.BlockSpec((1, nh, dqk, dv), lambda b, c: (b, 0, 0, 0)),
                   pl.BlockSpec((1, nh, dqk), lambda b, c: (b, 0, 0)),
                   pl.BlockSpec((1, nh, LANES), lambda b, c: (b, 0, 0))],
        out_shape=[jax.ShapeDtypeStruct((B, T, nh * dv), BF16),
                   jax.ShapeDtypeStruct((B, nh, dqk, dv), F32),
                   jax.ShapeDtypeStruct((B, nh, dqk), F32),
                   jax.ShapeDtypeStruct((B, nh, LANES), F32)],
        scratch_shapes=[pltpu.VMEM((nh, dqk, dv), F32),
                        pltpu.VMEM((nh, dqk), F32),
                        pltpu.VMEM((nh, LANES), F32)],
        compiler_params=_params("parallel", "arbitrary"),
        name="mlstm_recurrence",
    )(q, k, v, gcol, grow, c0.astype(F32), n0.astype(F32), m0b, norm_g.reshape(1, nh * dv))
    return h, c_new, n_new, m_new[:, :, 0]


def _pool_body(xb_ref, st_ref, wg_ref, sc_ref, y_ref, ext_ref, *, tb, pos0, nbuf):
    c = pl.program_id(1)
    lo = POOL_HALO - nbuf
    bb = xb_ref.shape[0]
    gw = wg_ref.shape[1]

    @pl.when(c == 0)
    def _():
        ext_ref[:, lo:POOL_HALO, :] = st_ref[...]

    @pl.when(c > 0)
    def _():
        ext_ref[:, lo:POOL_HALO, :] = ext_ref[:, lo + tb:POOL_HALO + tb, :]

    ext_ref[:, POOL_HALO:POOL_HALO + tb, :] = xb_ref[...].astype(F32)
    pos = pos0 + c * tb + lax.broadcasted_iota(jnp.int32, (1, tb, 1), 1)
    for g, w in enumerate(POOL_WINDOWS):
        cols = slice(g * gw, (g + 1) * gw)
        cur = ext_ref[:, POOL_HALO:POOL_HALO + tb, cols]
        win = cur
        for i in range(1, w):
            win = win + ext_ref[:, POOL_HALO - i:POOL_HALO - i + tb, cols]
        cnt = jnp.minimum(w, pos + 1).astype(F32)
        d = (win / cnt - cur).reshape(bb * tb, gw).astype(BF16)
        yg = _dot(d, wg_ref[g]) * sc_ref[:, cols]
        y_ref[:, :, cols] = yg.reshape(bb, tb, gw).astype(y_ref.dtype)


def pool_mix(xb, state, w_grp, scale, pos0):
    B, T, E = xb.shape
    nbuf = state.shape[1]
    bb, tb = (1, 256) if T % 256 == 0 else (B, T)
    ng, gw = w_grp.shape[0], w_grp.shape[1]
    return pl.pallas_call(
        functools.partial(_pool_body, tb=tb, pos0=pos0, nbuf=nbuf),
        grid=(B // bb, T // tb),
        in_specs=[pl.BlockSpec((bb, tb, E), lambda i, c: (i, c, 0)),
                  pl.BlockSpec((bb, nbuf, E), lambda i, c: (i, 0, 0)),
                  pl.BlockSpec((ng, gw, gw), lambda i, c: (0, 0, 0)),
                  pl.BlockSpec((1, E), lambda i, c: (0, 0))],
        out_specs=pl.BlockSpec((bb, tb, E), lambda i, c: (i, c, 0)),
        out_shape=jax.ShapeDtypeStruct((B, T, E), BF16),
        scratch_shapes=[pltpu.VMEM((bb, POOL_HALO + tb, E), F32)],
        compiler_params=_params("parallel", "arbitrary"),
        name="pool_mix",
    )(xb, state.astype(F32), w_grp, scale.reshape(1, E))


def _tail_body(*refs, kind, final):
    if kind == "mlstm":
        hn_ref, o_ref, xc_ref, z_ref, skip_ref = refs[:5]
        rest = refs[5:]
        mix = (o_ref[...].astype(F32) * hn_ref[...].astype(F32)
               + skip_ref[...] * xc_ref[...].astype(F32)) * _silu(z_ref[...].astype(F32))
    else:
        y_ref, z_ref = refs[:2]
        rest = refs[2:]
        mix = y_ref[...].astype(F32) * _silu(z_ref[...].astype(F32))
    x_ref, p_ref, wd_ref, gw_ref, pw_ref = rest[:5]
    rest = rest[5:]
    x1 = x_ref[...] + _dot(mix.astype(BF16), wd_ref[...])
    gate = _sigmoid(_dot(x1.astype(BF16), gw_ref[...]))
    x2 = x1 + gate * _dot(p_ref[...].astype(BF16), pw_ref[...])
    if final:
        fg_ref, xo_ref, yo_ref = rest
        yo_ref[...] = x2 * lax.rsqrt(jnp.mean(x2 * x2, axis=-1, keepdims=True) + EPS) * fg_ref[...]
    else:
        (xo_ref,) = rest
    xo_ref[...] = x2


def layer_tail(kind, mix_inputs, x, p, w_down, gate_w, ple_w, skip=None, final_g=None):
    M, D = x.shape
    tm = _pick(M, (256,))
    row = lambda i: (i, 0)
    fixed = lambda i: (0, 0)
    in_specs, args = [], []
    for a in mix_inputs:
        in_specs.append(pl.BlockSpec((tm, a.shape[1]), row))
        args.append(a)
    if kind == "mlstm":
        in_specs.append(pl.BlockSpec((1, skip.shape[-1]), fixed))
        args.append(skip.reshape(1, -1))
    in_specs += [pl.BlockSpec((tm, D), row), pl.BlockSpec((tm, p.shape[1]), row),
                 pl.BlockSpec(w_down.shape, fixed), pl.BlockSpec(gate_w.shape, fixed),
                 pl.BlockSpec(ple_w.shape, fixed)]
    args += [x, p, w_down, gate_w, ple_w]
    out_specs = [pl.BlockSpec((tm, D), row)]
    out_shape = [jax.ShapeDtypeStruct((M, D), F32)]
    if final_g is not None:
        in_specs.append(pl.BlockSpec((1, D), fixed))
        args.append(final_g.reshape(1, D))
        out_specs.append(pl.BlockSpec((tm, D), row))
        out_shape.append(jax.ShapeDtypeStruct((M, D), F32))
    return pl.pallas_call(
        functools.partial(_tail_body, kind=kind, final=final_g is not None),
        grid=(M // tm,),
        in_specs=in_specs, out_specs=out_specs, out_shape=out_shape,
        compiler_params=_params("parallel"),
        name="layer_tail_" + kind,
    )(*args)


def _sortable_key(s):
    s = jnp.where(s == 0.0, 0.0, s)
    bits = lax.bitcast_convert_type(s, jnp.int32)
    return jnp.where(bits < 0, bits ^ jnp.int32(0x7FFFFFFF), bits)


def _count(keys_ref, nvalid, pred):
    rows, cw = keys_ref.shape[1], keys_ref.shape[2]

    def body(kc, acc):
        hit = jnp.where(pred(keys_ref[kc], kc), 1.0, 0.0)
        part = hit[:, 0:LANES]
        for g in range(1, cw // LANES):
            part = part + hit[:, g * LANES:(g + 1) * LANES]
        return acc + part

    acc = lax.fori_loop(0, nvalid, body, jnp.zeros((rows, LANES), F32))
    return jnp.sum(acc, axis=1, keepdims=True)


def _select_topk(keys_ref, cidx_ref, nvalid, k_top, idx_bits):
    rows, cw = keys_ref.shape[1], keys_ref.shape[2]
    kf = float(k_top)
    imin = jnp.int32(INT_MIN)

    def bit_body(i, tau_u):
        cand_u = tau_u | jnp.left_shift(jnp.int32(1), 31 - i)
        cand = cand_u ^ imin
        cnt = _count(keys_ref, nvalid, lambda kk, kc: kk >= cand)
        return jnp.where(cnt >= kf, cand_u, tau_u)

    tau_u = lax.fori_loop(0, 32, bit_body, jnp.zeros((rows, 1), jnp.int32))
    tau = tau_u ^ imin
    n_gt = _count(keys_ref, nvalid, lambda kk, kc: kk > tau)
    n_ge = _count(keys_ref, nvalid, lambda kk, kc: kk >= tau)
    need = kf - n_gt
    short = tau_u == 0
    cidx_ref[...] = jnp.broadcast_to(jnp.where(short, -1, 2 ** 30), cidx_ref.shape)
    excess = jnp.max(jnp.where((n_ge > kf) & jnp.logical_not(short), 1.0, 0.0))

    @pl.when(excess > 0.5)
    def _():
        lane = lax.broadcasted_iota(jnp.int32, (1, cw), 1)

        def idx_body(i, cut):
            cand = cut | jnp.left_shift(jnp.int32(1), idx_bits - 1 - i)
            cnt = _count(keys_ref, nvalid,
                         lambda kk, kc: (kk == tau) & ((kc * cw + lane) < cand))
            return jnp.where(cnt < need, cand, cut)

        cut = lax.fori_loop(0, idx_bits, idx_body, jnp.zeros((rows, 1), jnp.int32))
        cidx_ref[...] = jnp.broadcast_to(jnp.where(short, -1, cut), cidx_ref.shape)

    return tau


def _selected(kk, colv, tau, cut):
    return (kk > tau) | ((kk == tau) & (colv <= cut))


def _n_causal_chunks(qb, tq, tk):
    return lax.div(qb * tq + (tq - 1), jnp.int32(tk)) + 1


def _idx_prompt_body(qi_ref, wi_ref, ki_ref, bias_ref, keys_ref, cidx_ref, *,
                     nh, dh, tq, tk, nk, k_top, idx_bits):
    per = keys_ref.shape[2] // tk
    gw = per * tk
    qb = pl.program_id(1)
    nvalid = _n_causal_chunks(qb, tq, gw)
    rowpos = qb * tq + lax.broadcasted_iota(jnp.int32, (tq, 1), 0)
    lane = lax.broadcasted_iota(jnp.int32, (1, tk), 1)
    wi = wi_ref[0]

    def score_group(gc, carry):
        for part in range(per):
            kt = ki_ref[0, gc * per + part]
            s = jnp.zeros((tq, tk), F32)
            for h in range(nh):
                rel = jnp.maximum(_dot_nt(qi_ref[0, :, h * dh:(h + 1) * dh], kt), 0.0)
                s = s + rel * wi[:, h:h + 1]
            colv = gc * gw + part * tk + lane
            keys_ref[gc, :, part * tk:(part + 1) * tk] = jnp.where(
                colv <= rowpos, _sortable_key(s), jnp.int32(INT_MIN))
        return carry

    lax.fori_loop(0, nvalid, score_group, 0)
    tau = _select_topk(keys_ref, cidx_ref, nvalid, k_top, idx_bits)
    cut = cidx_ref[:, 0:1]

    def write_group(gc, carry):
        for part in range(per):
            kk = keys_ref[gc, :, part * tk:(part + 1) * tk]
            sel = _selected(kk, gc * gw + part * tk + lane, tau, cut)
            bias_ref[0, 0, gc * per + part] = jnp.where(sel, 0.0, NEG).astype(bias_ref.dtype)
        return carry

    lax.fori_loop(0, nvalid, write_group, 0)

    def fill_chunk(kc, carry):
        bias_ref[0, 0, kc] = jnp.full((tq, tk), NEG, bias_ref.dtype)
        return carry

    lax.fori_loop(nvalid * per, nk, fill_chunk, 0)


def indexer_bias_prompt(qi, wi, ki, k_top):
    B, T, _ = qi.shape
    dh = ki.shape[-1]
    nh = qi.shape[-1] // dh
    tq, tk = min(ATT_TQ, T), min(ATT_TK, T)
    nq, nk = T // tq, T // tk
    per = 2 if nk % 2 == 0 else 1
    idx_bits = int(T).bit_length() + 1
    return pl.pallas_call(
        functools.partial(_idx_prompt_body, nh=nh, dh=dh, tq=tq, tk=tk, nk=nk, k_top=k_top,
                          idx_bits=idx_bits),
        grid=(B, nq),
        in_specs=[pl.BlockSpec((1, tq, nh * dh), lambda b, i: (b, i, 0)),
                  pl.BlockSpec((1, tq, LANES), lambda b, i: (b, i, 0)),
                  pl.BlockSpec((1, nk, tk, dh), lambda b, i: (b, 0, 0, 0))],
        out_specs=pl.BlockSpec((1, 1, nk, tq, tk), lambda b, i: (b, i, 0, 0, 0)),
        out_shape=jax.ShapeDtypeStruct((B, nq, nk, tq, tk), BF16),
        scratch_shapes=[pltpu.VMEM((nk // per, tq, per * tk), jnp.int32),
                        pltpu.VMEM((tq, LANES), jnp.int32)],
        compiler_params=_params("parallel", "arbitrary"),
        name="indexer_bias_prompt",
    )(qi, wi, ki.reshape(B, nk, tk, dh))


def _attn_prompt_body(q_ref, k_ref, v_ref, bias_ref, o_ref, m_s, acc_s, *, nh, dh, tq, tk):
    qb, kb = pl.program_id(1), pl.program_id(2)
    nvalid = _n_causal_chunks(qb, tq, tk)

    @pl.when(kb == 0)
    def _():
        m_s[...] = jnp.full(m_s.shape, NEG, F32)
        acc_s[...] = jnp.zeros(acc_s.shape, F32)

    @pl.when(kb < nvalid)
    def _():
        bias = bias_ref[0, 0, 0].astype(F32)
        for h in range(nh):
            s = _dot_nt(q_ref[0, h], k_ref[0, h]) + bias
            m_prev = m_s[h]
            m_new = jnp.maximum(m_prev, jnp.max(s, axis=1, keepdims=True))
            alpha = jnp.exp2(m_prev - m_new)
            p = jnp.exp2(s - jnp.concatenate([m_new] * (tk // LANES), axis=1))
            acc_s[h] = alpha * acc_s[h] + _dot(p.astype(BF16), v_ref[0, h])
            m_s[h] = m_new

    @pl.when(kb == pl.num_programs(2) - 1)
    def _():
        for h in range(nh):
            acc = acc_s[h]
            o_ref[0, :, h * dh:(h + 1) * dh] = (acc[:, :dh] / acc[:, dh:dh + 1]).astype(o_ref.dtype)


def attention_prompt(q, k, v, bias):
    B, nh, T, dh = q.shape
    assert tuple(v.shape) == (B, nh, T, LANES) and dh < LANES
    _, nq, nk, tq, tk = bias.shape

    def kv_map(b, i, j):
        return (b, 0, jnp.minimum(j, _n_causal_chunks(i, tq, tk) - 1), 0)

    def bias_map(b, i, j):
        return (b, i, jnp.minimum(j, _n_causal_chunks(i, tq, tk) - 1), 0, 0)

    return pl.pallas_call(
        functools.partial(_attn_prompt_body, nh=nh, dh=dh, tq=tq, tk=tk),
        grid=(B, nq, nk),
        in_specs=[pl.BlockSpec((1, nh, tq, dh), lambda b, i, j: (b, 0, i, 0)),
                  pl.BlockSpec((1, nh, tk, dh), kv_map),
                  pl.BlockSpec((1, nh, tk, LANES), kv_map),
                  pl.BlockSpec((1, 1, 1, tq, tk), bias_map)],
        out_specs=pl.BlockSpec((1, tq, nh * dh), lambda b, i, j: (b, i, 0)),
        out_shape=jax.ShapeDtypeStruct((B, T, nh * dh), BF16),
        scratch_shapes=[pltpu.VMEM((nh, tq, LANES), F32),
                        pltpu.VMEM((nh, tq, LANES), F32)],
        compiler_params=_params("parallel", "parallel", "arbitrary"),
        name="attention_prompt",
    )(q, k, v, bias)


def _idx_sample_body(pt_ref, qi_ref, wi_ref, knew_ref, *rest, nh, ts, npg, nsteps, k_top,
                     idx_bits):
    page_refs = rest[:npg]
    bias_ref, keys_ref, cidx_ref = rest[npg:]
    s = pl.program_id(1)
    cw = keys_ref.shape[2]
    wi = wi_ref[0]
    qi = qi_ref[0]

    def scores(kt):
        full = jnp.maximum(_dot_nt(qi, kt), 0.0)
        out = jnp.zeros((ts, cw), F32)
        for h in range(nh):
            out = out + full[h * ts:(h + 1) * ts, :] * wi[:, h:h + 1]
        return out

    kt = jnp.concatenate([r[0] for r in page_refs], axis=0).astype(BF16)
    keys_ref[s] = _sortable_key(scores(kt))

    @pl.when(s == nsteps - 1)
    def _():
        lane = lax.broadcasted_iota(jnp.int32, (1, cw), 1)
        past = nsteps * cw
        rowpos = past + lax.broadcasted_iota(jnp.int32, (ts, 1), 0)
        colv_new = past + lane
        keys_ref[nsteps] = jnp.where(colv_new <= rowpos, _sortable_key(scores(knew_ref[0])),
                                     jnp.int32(INT_MIN))
        tau = _select_topk(keys_ref, cidx_ref, nsteps + 1, k_top, idx_bits)
        cut = cidx_ref[:, 0:1]
        for kc in range(nsteps + 1):
            sel = _selected(keys_ref[kc], kc * cw + lane, tau, cut)
            bias_ref[0, kc] = jnp.where(sel, 0.0, NEG).astype(bias_ref.dtype)


def _page_specs(n_pages_per_step, page, width, first_step_offset):
    specs = []
    for i in range(n_pages_per_step):
        def imap(b, s, pt, i=i):
            step = jnp.maximum(s - first_step_offset, 0)
            return (pt[b, step * n_pages_per_step + i], 0, 0)
        specs.append(pl.BlockSpec((1, page, width), imap))
    return specs


def indexer_bias_sample(qi_hq, wi, ki_new_pad, cache_kidx, page_table, k_top):
    B, n_pages = page_table.shape
    _, page, dh = cache_kidx.shape
    ts = wi.shape[1]
    nh = qi_hq.shape[1] // ts
    npg = min(PAGES_PER_STEP, n_pages)
    nsteps = n_pages // npg
    cw = npg * page
    idx_bits = int(n_pages * page + cw).bit_length() + 1
    grid_spec = pltpu.PrefetchScalarGridSpec(
        num_scalar_prefetch=1,
        grid=(B, nsteps),
        in_specs=[pl.BlockSpec((1, nh * ts, dh), lambda b, s, pt: (b, 0, 0)),
                  pl.BlockSpec((1, ts, LANES), lambda b, s, pt: (b, 0, 0)),
                  pl.BlockSpec((1, cw, dh), lambda b, s, pt: (b, 0, 0))]
        + _page_specs(npg, page, dh, 0),
        out_specs=pl.BlockSpec((1, nsteps + 1, ts, cw), lambda b, s, pt: (b, 0, 0, 0)),
        scratch_shapes=[pltpu.VMEM((nsteps + 1, ts, cw), jnp.int32),
                        pltpu.VMEM((ts, LANES), jnp.int32)],
    )
    return pl.pallas_call(
        functools.partial(_idx_sample_body, nh=nh, ts=ts, npg=npg, nsteps=nsteps, k_top=k_top,
                          idx_bits=idx_bits),
        grid_spec=grid_spec,
        out_shape=jax.ShapeDtypeStruct((B, nsteps + 1, ts, cw), F32),
        compiler_params=_params("parallel", "arbitrary"),
        name="indexer_bias_sample",
    )(page_table, qi_hq, wi, ki_new_pad, *([cache_kidx] * npg))


def _attn_sample_body(pt_ref, qbd_ref, knew_ref, vnew_ref, bias_ref, *rest, nh, ts, npg, dh):
    kpages, vpages = rest[:npg], rest[npg:2 * npg]
    o_ref, m_s, l_s, acc_s = rest[2 * npg:]
    s = pl.program_id(1)
    rows = nh * ts

    @pl.when(s == 0)
    def _():
        m_s[...] = jnp.full(m_s.shape, NEG, F32)
        l_s[...] = jnp.zeros(l_s.shape, F32)
        acc_s[...] = jnp.zeros(acc_s.shape, F32)

    def update(kt, vt):
        bias = bias_ref[0, 0]
        logits = _dot_nt(qbd_ref[0], kt) + jnp.concatenate([bias] * nh, axis=0)
        m_prev = m_s[...]
        m_new = jnp.maximum(m_prev, jnp.max(logits, axis=1, keepdims=True))
        alpha = jnp.exp2(m_prev - m_new)
        p = jnp.exp2(logits - m_new)
        l_s[...] = alpha * l_s[...] + jnp.sum(p, axis=1, keepdims=True)
        acc_s[...] = alpha * acc_s[...] + _dot(p.astype(BF16), vt)
        m_s[...] = m_new

    @pl.when(s == 0)
    def _():
        update(knew_ref[0], vnew_ref[0])

    def gather_pages(page_refs):
        page = page_refs[0].shape[1] // nh
        mats = [jnp.concatenate([r[0, pl.ds(h, page, stride=nh), :] for h in range(nh)], axis=1)
                for r in page_refs]
        return jnp.concatenate(mats, axis=0).astype(BF16)

    @pl.when(s > 0)
    def _():
        update(gather_pages(kpages), gather_pages(vpages))

    @pl.when(s == pl.num_programs(1) - 1)
    def _():
        o = acc_s[...] / l_s[...]
        lane = lax.broadcasted_iota(jnp.int32, (1, nh * dh), 1)
        res = jnp.zeros((ts, nh * dh), F32)
        for h in range(nh):
            own = (lane >= h * dh) & (lane < (h + 1) * dh)
            res = res + jnp.where(own, o[h * ts:(h + 1) * ts, :], 0.0)
        o_ref[0] = res.astype(o_ref.dtype)


def attention_sample(qbd, k_new_pad, v_new_pad, bias, cache_k, cache_v, page_table, nh, ts):
    B, n_pages = page_table.shape
    _, page_rows, dh = cache_k.shape
    page = page_rows // nh
    width = nh * dh
    nsteps1, cw = bias.shape[1], bias.shape[3]
    npg = cw // page
    rows = nh * ts

    def bias_map(b, s, pt):
        return (b, jnp.where(s == 0, nsteps1 - 1, s - 1), 0, 0)

    grid_spec = pltpu.PrefetchScalarGridSpec(
        num_scalar_prefetch=1,
        grid=(B, nsteps1),
        in_specs=[pl.BlockSpec((1, rows, width), lambda b, s, pt: (b, 0, 0)),
                  pl.BlockSpec((1, cw, width), lambda b, s, pt: (b, 0, 0)),
                  pl.BlockSpec((1, cw, width), lambda b, s, pt: (b, 0, 0)),
                  pl.BlockSpec((1, 1, ts, cw), bias_map)]
        + _page_specs(npg, page_rows, dh, 1) + _page_specs(npg, page_rows, dh, 1),
        out_specs=pl.BlockSpec((1, ts, width), lambda b, s, pt: (b, 0, 0)),
        scratch_shapes=[pltpu.VMEM((rows, 1), F32),
                        pltpu.VMEM((rows, 1), F32),
                        pltpu.VMEM((rows, width), F32)],
    )
    return pl.pallas_call(
        functools.partial(_attn_sample_body, nh=nh, ts=ts, npg=npg, dh=dh),
        grid_spec=grid_spec,
        out_shape=jax.ShapeDtypeStruct((B, ts, width), BF16),
        compiler_params=_params("parallel", "arbitrary"),
        name="attention_sample",
    )(page_table, qbd, k_new_pad, v_new_pad, bias, *([cache_k] * npg), *([cache_v] * npg))


def _pad_time(a, t_pad, value=0.0):
    return jnp.pad(a, ((0, 0), (0, t_pad - a.shape[1]), (0, 0)), constant_values=value)


def _mlstm_layer(x, g, B, T, state, w, j):
    c0, n0, m0, conv0 = state
    nh = c0.shape[1]
    E = w["a_conv_w"].shape[-1]
    w_up = w["a_w_up"][j]
    xb, = norm_matmul(x, g, w_up[:, :E], (BF16,))
    z, = norm_matmul(x, g, w_up[:, E:], (BF16,))
    xc = conv_silu(xb.reshape(B, T, E), conv0, w["a_conv_w_f32"][j], w["a_conv_b"][j]).reshape(B * T, E)
    q = matmul(xc, w["a_w_q"][j], BF16)
    k = matmul(xc, w["a_w_k"][j], BF16)
    v = matmul(xb, w["a_w_v"][j], BF16)
    o = matmul(xb, w["a_w_o"][j], BF16, bias=w["a_b_o"][j], act="sigmoid")
    gates = matmul(xc, w["a_w_if_pad"][j], F32, bias=w["a_b_if_pad"][j], act="gates", n_li=nh)
    gcol = gates.reshape(B, T, LANES)
    q3, k3, v3 = q.reshape(B, T, -1), k.reshape(B, T, -1), v.reshape(B, T, -1)
    tp = T if T % LANES == 0 else ((T + LANES - 1) // LANES) * LANES
    if tp != T:
        q3, k3, v3 = _pad_time(q3, tp), _pad_time(k3, tp), _pad_time(v3, tp)
        pad_row = jnp.where(jnp.arange(LANES) < nh, NEG, 0.0).astype(F32)
        gcol = jnp.concatenate([gcol, jnp.broadcast_to(pad_row, (B, tp - T, LANES))], axis=1)
    grow = jnp.swapaxes(gcol[:, :, :2 * nh], 1, 2)
    hn, c_new, n_new, m_new = mlstm_recurrence(q3, k3, v3, gcol, grow, c0, n0, m0, w["a_norm_g"][j])
    hn = hn[:, :T].reshape(B * T, -1)
    assert T >= CONV_W - 1
    conv_new = xb.reshape(B, T, E)[:, T - (CONV_W - 1):].astype(F32)
    return (hn, o, xc, z), (c_new, n_new, m_new, conv_new)


def _rope_tables(pos):
    half = 32
    inv = ROPE_THETA ** (-np.arange(half, dtype=np.float64) / half)
    ang = np.asarray(pos, np.float64)[:, None] * inv[None, :]
    cos = np.tile(np.cos(ang), (1, 4))
    sin = np.tile(np.concatenate([-np.sin(ang), np.sin(ang)], axis=1), (1, 2))
    return jnp.asarray(cos, F32), jnp.asarray(sin, F32)


def _dsa_project(x, g, B, T, pos, w, j, nh, dh, nhi, di, head_major=False):
    M = B * T
    cos, sin = _rope_tables(pos)
    tm = _pick(M, (512, 256))
    if T % tm == 0:
        tabs, tab_blocks = (cos, sin), T // tm
    else:
        tabs, tab_blocks = (jnp.tile(cos, (M // T, 1)), jnp.tile(sin, (M // T, 1))), M // tm
    rope = dict(rope_tabs=tabs, tab_blocks=tab_blocks)
    hm = (lambda *widths: (B, T, dh, widths)) if head_major else (lambda *widths: None)
    q, = norm_matmul(x, g, w["b_wq"][j], (BF16,), scale=dh ** -0.5 * math.log2(math.e),
                     head_major=hm(dh), **rope)
    k32, k16 = norm_matmul(x, g, w["b_wk"][j], (F32, BF16), head_major=hm(0, dh), **rope)
    v32, v16 = norm_matmul(x, g, w["b_wv"][j], (F32, BF16), head_major=hm(0, LANES))
    gate, = norm_matmul(x, g, w["b_wg"][j], (BF16,))
    qi, = norm_matmul(x, g, w["b_wqi"][j], (BF16,), **rope)
    ki32, = norm_matmul(x, g, w["b_wki_pad"][j], (F32,), **rope)
    wi, = norm_matmul(x, g, w["b_wwi_pad"][j], (F32,), scale=nhi ** -0.5 * di ** -0.5)
    return q, k32, k16, v32, v16, gate, qi, ki32[:, :di], wi


def _dsa_layer_prompt(x, g, B, T, w, j, dims):
    nh, dh, nhi, di = dims
    q, k32, k16, v32, v16, gate, qi, ki32, wi = _dsa_project(
        x, g, B, T, np.arange(T), w, j, *dims, head_major=True)
    k_top = min(TOPK_MAX, T // 4)
    bias = indexer_bias_prompt(qi.reshape(B, T, nhi * di), wi.reshape(B, T, LANES),
                               ki32.astype(BF16).reshape(B, T, di), k_top)
    attn = attention_prompt(q, k16, v16, bias).reshape(B * T, nh * dh)
    new =(k32.reshape(B, T, nh, dh), v32.reshape(B, T, nh, dh), ki32.reshape(B, T, di))
    return (attn, gate), new


def _dsa_layer_sample(x, g, B, T, w, j, dims, ctx):
    nh, dh, nhi, di = dims
    cache_k, cache_v, cache_kidx, page_table = ctx
    n_pages = page_table.shape[1]
    page = cache_kidx.shape[2]
    past = n_pages * page
    q, k32, k16, v32, v16, gate, qi, ki32, wi = _dsa_project(
        x, g, B, T, past + np.arange(T), w, j, *dims)
    k_top = min(TOPK_MAX, (past + T) // 4)
    cw = min(PAGES_PER_STEP, n_pages) * page
    qi_hq = jnp.transpose(qi.reshape(B, T, nhi, di), (0, 2, 1, 3)).reshape(B, nhi * T, di)
    ki_new = _pad_time(ki32.astype(BF16).reshape(B, T, di), cw)
    bias = indexer_bias_sample(qi_hq, wi.reshape(B, T, LANES), ki_new,
                               cache_kidx[j], page_table, k_top)
    q3 = q.reshape(B, T, nh * dh)
    own = (jnp.arange(nh * T)[:, None] // T) == (jnp.arange(nh * dh)[None, :] // dh)
    qbd = jnp.where(own[None], jnp.tile(q3, (1, nh, 1)), 0).astype(BF16)
    width = nh * dh
    attn = attention_sample(qbd, _pad_time(k16.reshape(B, T, width), cw),
                            _pad_time(v16.reshape(B, T, width), cw), bias,
                            cache_k[j].reshape(-1, page * nh, dh), cache_v[j].reshape(-1, page * nh, dh),
                            page_table, nh, T)
    new = (k32.reshape(B, T, nh, dh), v32.reshape(B, T, nh, dh), ki32.reshape(B, T, di))
    return (attn.reshape(B * T, width), gate), new


def _pool_layer(x, g, B, T, buf, pos0, w, j):
    E = w["c_scale"].shape[-1]
    w_up = w["c_w_up"][j]
    xb, = norm_matmul(x, g, w_up[:, :E], (BF16,))
    z, = norm_matmul(x, g, w_up[:, E:], (BF16,))
    xb3 = xb.reshape(B, T, E)
    y = pool_mix(xb3, buf, w["c_w_grp"][j], w["c_scale"][j], pos0).reshape(B * T, E)
    nbuf = buf.shape[1]
    if T >= nbuf:
        new_buf = xb3[:, T - nbuf:].astype(F32)
    else:
        new_buf = jnp.concatenate([buf[:, T:].astype(F32), xb3.astype(F32)], axis=1)
    return (y, z), new_buf


def _run_group(x3, p4, pos0, mlstm_state, dsa_ctx, pool_state, w, dims):
    B, T, D = x3.shape
    depth = p4.shape[0]
    x = x3.reshape(B * T, D)
    new_a, new_b, new_c = [], [], []
    y = None
    for i in range(depth):
        kind, j = i % 3, i // 3
        g = w["norm_g"][i]
        final_g = w["final_g"] if i == depth - 1 else None
        p = p4[i].reshape(B * T, -1)
        if kind == 0:
            state = tuple(s[j] for s in mlstm_state)
            mix, st = _mlstm_layer(x, g, B, T, state, w, j)
            new_a.append(st)
            out = layer_tail("mlstm", mix, x, p, w["a_w_down"][j], w["ple_gate_w"][i], w["ple_w"][i],
                             skip=w["a_skip"][j], final_g=final_g)
        elif kind == 1:
            if dsa_ctx is None:
                mix, st = _dsa_layer_prompt(x, g, B, T, w, j, dims)
            else:
                mix, st = _dsa_layer_sample(x, g, B, T, w, j, dims, dsa_ctx)
            new_b.append(st)
            out = layer_tail("gated", mix, x, p, w["b_w_out"][j], w["ple_gate_w"][i], w["ple_w"][i],
                             final_g=final_g)
        else:
            mix, st = _pool_layer(x, g, B, T, pool_state[j], pos0, w, j)
            new_c.append(st)
            out = layer_tail("gated", mix, x, p, w["c_w_down"][j], w["ple_gate_w"][i], w["ple_w"][i],
                             final_g=final_g)
        x = out[0]
        if final_g is not None:
            y = out[1]
    a_states = tuple(jnp.stack([s[r] for s in new_a]) for r in range(4))
    b_states = tuple(jnp.stack([s[r] for s in new_b]) for r in range(3))
    c_state = jnp.stack(new_c)
    return y.reshape(B, T, D), a_states, b_states, c_state


def kernel(x_prompt, x_sample, state_mlstm_C, state_mlstm_n, state_mlstm_m, state_mlstm_conv, state_pool,
           cache_k, cache_v, cache_kidx, page_table, p_prompt, p_sample,
           norm_g, final_g, ple_w, ple_gate_w,
           a_w_up, a_conv_w, a_conv_b, a_w_q, a_w_k, a_w_v, a_w_if, a_b_if, a_w_o, a_b_o, a_norm_g,
           a_skip, a_w_down, b_w_in, b_w_out, c_w_up, c_w_grp, c_scale, c_w_down):
    D = x_prompt.shape[-1]
    nh, dh = cache_k.shape[3], cache_k.shape[4]
    di = cache_kidx.shape[-1]
    aw = nh * dh
    nhi = (b_w_in.shape[-1] - 4 * aw - di) // (di + 1)
    dims = (nh, dh, nhi, di)
    nha = state_mlstm_C.shape[2]
    bf = lambda a: a.astype(BF16)

    def pad_cols(a, n):
        return jnp.pad(a, [(0, 0)] * (a.ndim - 1) + [(0, n - a.shape[-1])])

    o1, o2, o3, o4, o5, o6 = np.cumsum([aw, aw, aw, aw, nhi * di, di])
    w = dict(
        norm_g=norm_g, final_g=final_g, ple_w=bf(ple_w), ple_gate_w=bf(ple_gate_w),
        a_w_up=bf(a_w_up), a_conv_w_f32=a_conv_w, a_conv_w=a_conv_w, a_conv_b=a_conv_b,
        a_w_q=bf(a_w_q), a_w_k=bf(a_w_k), a_w_v=bf(a_w_v), a_w_o=bf(a_w_o), a_b_o=a_b_o,
        a_w_if_pad=bf(pad_cols(a_w_if, LANES)), a_b_if_pad=pad_cols(a_b_if, LANES),
        a_norm_g=a_norm_g, a_skip=a_skip, a_w_down=bf(a_w_down),
        b_wq=bf(b_w_in[..., :o1]), b_wk=bf(b_w_in[..., o1:o2]), b_wv=bf(b_w_in[..., o2:o3]),
        b_wg=bf(b_w_in[..., o3:o4]), b_wqi=bf(b_w_in[..., o4:o5]),
        b_wki_pad=bf(pad_cols(b_w_in[..., o5:o6], LANES)),
        b_wwi_pad=bf(pad_cols(b_w_in[..., o6:], LANES)),
        b_w_out=bf(b_w_out), c_w_up=bf(c_w_up), c_w_grp=bf(c_w_grp), c_scale=c_scale,
        c_w_down=bf(c_w_down),
    )

    Bp = x_prompt.shape[0]
    na, nc = state_mlstm_C.shape[0], state_pool.shape[0]
    zeros_a = (jnp.zeros((na, Bp) + state_mlstm_C.shape[2:], F32),
               jnp.zeros((na, Bp) + state_mlstm_n.shape[2:], F32),
               jnp.zeros((na, Bp) + state_mlstm_m.shape[2:], F32),
               jnp.zeros((na, Bp) + state_mlstm_conv.shape[2:], F32))
    zeros_c = jnp.zeros((nc, Bp) + state_pool.shape[2:], F32)
    y_p, a_p, b_p, c_p = _run_group(x_prompt, p_prompt, 0, zeros_a, None, zeros_c, w, dims)

    past = page_table.shape[1] * cache_k.shape[2]
    y_s, a_s, b_s, c_s = _run_group(
        x_sample, p_sample, past,
        (state_mlstm_C, state_mlstm_n, state_mlstm_m, state_mlstm_conv),
        (cache_k, cache_v, cache_kidx, page_table), state_pool, w, dims)
    return (y_p, y_s, *a_p, *b_p, c_p, *a_s, *b_s, c_s)
```

```python
import functools
import math

import numpy as np
import jax
import jax.numpy as jnp
from jax import lax
from jax.experimental import pallas as pl
from jax.experimental.pallas import tpu as pltpu

EPS = 1e-6
ROPE_THETA = 10000.0
TOPK_MAX = 256
POOL_WINDOWS = (2, 4, 8, 16)
CONV_W = 4
MLSTM_CHUNK = 256
NEG = -1e30
LANES = 128
SUBLANES = 8
VMEM_LIMIT_BYTES = 56 * 1024 * 1024
ATT_TQ, ATT_TK = 256, 512
PAGES_PER_STEP = 8
ROW_TILES = (1024, 512, 256)
COL_TILES = (1024, 512, 256, 128)

F32 = jnp.float32
BF16 = jnp.bfloat16
INT_MIN = -2 ** 31


def _params(*sem):
    return pltpu.CompilerParams(dimension_semantics=sem, vmem_limit_bytes=VMEM_LIMIT_BYTES)


def _sigmoid(x):
    return 1.0 / (1.0 + jnp.exp(-x))


def _silu(x):
    return x * _sigmoid(x)


def _log_sigmoid(x):
    return jnp.minimum(x, 0.0) - jnp.log(1.0 + jnp.exp(-jnp.abs(x)))


def _dot(a, b):
    return jnp.dot(a, b, preferred_element_type=F32)


def _dot_nt(a, b):
    return lax.dot_general(a, b, (((1,), (1,)), ((), ())), preferred_element_type=F32)


def _dot_tn(a, b):
    return lax.dot_general(a, b, (((0,), (0,)), ((), ())), preferred_element_type=F32)


def _pick(n, cands):
    for c in cands:
        if n % c == 0:
            return c
    return n


def _rope_tile(a, cos, sin_signed):
    lane = lax.broadcasted_iota(jnp.int32, (1, LANES), 1)
    first_half = (lane % 64) < 32
    pieces = []
    for gi in range(a.shape[1] // LANES):
        ag = a[:, gi * LANES:(gi + 1) * LANES]
        ahead = pltpu.roll(ag, LANES - 32, 1)
        behind = pltpu.roll(ag, 32, 1)
        rot = jnp.where(first_half, ahead, behind)
        pieces.append(ag * cos + rot * sin_signed)
    return pieces[0] if len(pieces) == 1 else jnp.concatenate(pieces, axis=1)


def _norm_matmul_body(*refs, rope, n_out, scale, head_dim):
    if rope:
        x_ref, g_ref, w_ref, cos_ref, sin_ref = refs[:5]
        rest = refs[5:]
    else:
        x_ref, g_ref, w_ref = refs[:3]
        rest = refs[3:]
    outs, xn_ref = rest[:n_out], rest[n_out]

    @pl.when(pl.program_id(1) == 0)
    def _():
        xf = x_ref[...]
        y = xf * lax.rsqrt(jnp.mean(xf * xf, axis=-1, keepdims=True) + EPS) * g_ref[...]
        xn_ref[...] = y.astype(BF16)

    acc = _dot(xn_ref[...], w_ref[...])
    if rope:
        acc = _rope_tile(acc, cos_ref[...], sin_ref[...])
    if scale != 1.0:
        acc = acc * scale
    for o in outs:
        if len(o.shape) == 4:
            npad = o.shape[3] - head_dim
            if npad:
                lane = lax.broadcasted_iota(jnp.int32, (acc.shape[0], npad), 1)
                pad = jnp.where(lane == 0, 1.0, 0.0)
            for hh in range(o.shape[1]):
                piece = acc[:, hh * head_dim:(hh + 1) * head_dim]
                if npad:
                    piece = jnp.concatenate([piece, pad], axis=1)
                o[0, hh] = piece.astype(o.dtype)
        else:
            o[...] = acc.astype(o.dtype)


def norm_matmul(x, g, w, out_dtypes, rope_tabs=None, tab_blocks=1, scale=1.0, head_major=None,
                row_tile=None):
    M, D = x.shape
    N = w.shape[1]
    tm = row_tile if row_tile is not None else _pick(M, ROW_TILES)
    tn = _pick(N, COL_TILES)
    out_specs = [pl.BlockSpec((tm, tn), lambda i, j: (i, j)) for _ in out_dtypes]
    out_shape = [jax.ShapeDtypeStruct((M, N), dt) for dt in out_dtypes]
    head_dim = 0
    if head_major is not None:
        hb, ht, head_dim, flags = head_major
        assert ht % tm == 0
        tpb = ht // tm
        for idx, width in enumerate(flags):
            if width:
                out_specs[idx] = pl.BlockSpec((1, tn // head_dim, tm, width),
                                              lambda i, j: (i // tpb, j, i % tpb, 0))
                out_shape[idx] = jax.ShapeDtypeStruct((hb, N // head_dim, ht, width), out_dtypes[idx])
    in_specs = [pl.BlockSpec((tm, D), lambda i, j: (i, 0)),
                pl.BlockSpec((1, D), lambda i, j: (0, 0)),
                pl.BlockSpec((D, tn), lambda i, j: (0, j))]
    args = [x, g.reshape(1, D), w]
    if rope_tabs is not None:
        for t in rope_tabs:
            in_specs.append(pl.BlockSpec((tm, LANES), lambda i, j: (i % tab_blocks, 0)))
            args.append(t)
    return pl.pallas_call(
        functools.partial(_norm_matmul_body, rope=rope_tabs is not None, n_out=len(out_dtypes),
                          scale=scale, head_dim=head_dim),
        grid=(M // tm, N // tn),
        in_specs=in_specs,
        out_specs=out_specs,
        out_shape=out_shape,
        scratch_shapes=[pltpu.VMEM((tm, D), BF16)],
        compiler_params=_params("parallel", "arbitrary"),
        name="norm_matmul",
    )(*args)


def _matmul_body(*refs, has_bias, act, n_li):
    if has_bias:
        x_ref, w_ref, b_ref, o_ref = refs
    else:
        x_ref, w_ref, o_ref = refs
    acc = _dot(x_ref[...], w_ref[...])
    if has_bias:
        acc = acc + b_ref[...]
    if act == "sigmoid":
        acc = _sigmoid(acc)
    elif act == "gates":
        col = lax.broadcasted_iota(jnp.int32, acc.shape, 1)
        acc = jnp.where(col < n_li, acc, _log_sigmoid(acc))
    o_ref[...] = acc.astype(o_ref.dtype)


def matmul(x, w, out_dtype, bias=None, act=None, n_li=0):
    M, K = x.shape
    N = w.shape[1]
    tm = _pick(M, ROW_TILES)
    tn = _pick(N, COL_TILES)
    in_specs = [pl.BlockSpec((tm, K), lambda i, j: (i, 0)),
                pl.BlockSpec((K, tn), lambda i, j: (0, j))]
    args = [x, w]
    if bias is not None:
        in_specs.append(pl.BlockSpec((1, tn), lambda i, j: (0, j)))
        args.append(bias.reshape(1, N).astype(F32))
    return pl.pallas_call(
        functools.partial(_matmul_body, has_bias=bias is not None, act=act, n_li=n_li),
        grid=(M // tm, N // tn),
        in_specs=in_specs,
        out_specs=pl.BlockSpec((tm, tn), lambda i, j: (i, j)),
        out_shape=jax.ShapeDtypeStruct((M, N), out_dtype),
        compiler_params=_params("parallel", "parallel"),
        name="matmul",
    )(*args)


CONV_HALO = 8
POOL_HALO = 16


def _conv_body(xb_ref, st_ref, w_ref, b_ref, xc_ref, ext_ref, *, tb):
    c = pl.program_id(1)
    nst = CONV_W - 1
    lo = CONV_HALO - nst

    @pl.when(c == 0)
    def _():
        ext_ref[:, lo:CONV_HALO, :] = st_ref[...]

    @pl.when(c > 0)
    def _():
        ext_ref[:, lo:CONV_HALO, :] = ext_ref[:, lo + tb:CONV_HALO + tb, :]

    ext_ref[:, CONV_HALO:CONV_HALO + tb, :] = xb_ref[...].astype(F32)
    acc = b_ref[...][None] + ext_ref[:, lo:lo + tb, :] * w_ref[0:1, :][None]
    for j in range(1, CONV_W):
        acc = acc + ext_ref[:, lo + j:lo + j + tb, :] * w_ref[j:j + 1, :][None]
    xc_ref[...] = _silu(acc).astype(xc_ref.dtype)


def conv_silu(xb, state, w, b):
    B, T, E = xb.shape
    bb, tb = (1, 256) if T % 256 == 0 else (B, T)
    return pl.pallas_call(
        functools.partial(_conv_body, tb=tb),
        grid=(B // bb, T // tb),
        in_specs=[pl.BlockSpec((bb, tb, E), lambda i, c: (i, c, 0)),
                  pl.BlockSpec((bb, CONV_W - 1, E), lambda i, c: (i, 0, 0)),
                  pl.BlockSpec((CONV_W, E), lambda i, c: (0, 0)),
                  pl.BlockSpec((1, E), lambda i, c: (0, 0))],
        out_specs=pl.BlockSpec((bb, tb, E), lambda i, c: (i, c, 0)),
        out_shape=jax.ShapeDtypeStruct((B, T, E), BF16),
        scratch_shapes=[pltpu.VMEM((bb, CONV_HALO + tb, E), F32)],
        compiler_params=_params("parallel", "arbitrary"),
        name="conv_silu",
    )(xb, state.astype(F32), w, b.reshape(1, E))


def _mlstm_body(q_ref, k_ref, v_ref, gc_ref, gr_ref, c0_ref, n0_ref, m0_ref, ng_ref,
                h_ref, c_out_ref, n_out_ref, m_out_ref, c_s, n_s, m_s, *, nh, dqk, dv, lc):
    c = pl.program_id(1)

    @pl.when(c == 0)
    def _():
        c_s[...] = c0_ref[0]
        n_s[...] = n0_ref[0]
        m_s[...] = m0_ref[0]

    row = lax.broadcasted_iota(jnp.int32, (lc, lc), 0)
    col = lax.broadcasted_iota(jnp.int32, (lc, lc), 1)
    causal = row >= col
    scale = dqk ** -0.5
    gc = gc_ref[0]
    gr = gr_ref[0]
    for h in range(nh):
        q = (q_ref[0, :, h * dqk:(h + 1) * dqk].astype(F32) * scale).astype(BF16)
        k = k_ref[0, :, h * dqk:(h + 1) * dqk]
        v = v_ref[0, :, h * dv:(h + 1) * dv]
        li_c, lf_c = gc[:, h:h + 1], gc[:, nh + h:nh + h + 1]
        li_r, lf_r = gr[h:h + 1, :], gr[nh + h:nh + h + 1, :]
        b_c = jnp.sum(jnp.where(causal, lf_r, 0.0), axis=1, keepdims=True)
        b_r = jnp.sum(jnp.where(row <= col, lf_c, 0.0), axis=0, keepdims=True)
        m_prev = m_s[h:h + 1, 0:1]
        dm = jnp.where(causal, b_c - b_r + li_r, NEG)
        m_inter = b_c + m_prev
        m_t = jnp.maximum(m_inter, jnp.max(dm, axis=1, keepdims=True))
        a = _dot_nt(q, k) * jnp.exp(dm - m_t)
        inter = jnp.exp(m_inter - m_t)
        c_prev = c_s[h]
        n_prev = n_s[h:h + 1, :]
        num = _dot(a.astype(BF16), v) + inter * _dot(q, c_prev.astype(BF16))
        qn = jnp.sum(q.astype(F32) * n_prev, axis=1, keepdims=True)
        den = jnp.sum(a, axis=1, keepdims=True) + inter * qn
        hh = num / jnp.maximum(jnp.abs(den), jnp.exp(-m_t))
        hn = hh * lax.rsqrt(jnp.mean(hh * hh, axis=-1, keepdims=True) + EPS) \
            * ng_ref[:, h * dv:(h + 1) * dv]
        h_ref[0, :, h * dv:(h + 1) * dv] = hn.astype(h_ref.dtype)
        b_last = b_c[lc - 1:lc, :]
        g_r = b_last - b_r + li_r
        g_c = b_last - b_c + li_c
        m_new = jnp.maximum(b_last + m_prev, jnp.max(g_r, axis=1, keepdims=True))
        w_r = jnp.exp(g_r - m_new)
        w_c = jnp.exp(g_c - m_new)
        decay = jnp.exp(b_last + m_prev - m_new)
        wv = (w_c * v.astype(F32)).astype(BF16)
        c_s[h] = decay * c_prev + _dot_tn(k, wv)
        wr8 = jnp.broadcast_to(w_r, (SUBLANES, lc)).astype(BF16)
        n_s[h:h + 1, :] = decay * n_prev + _dot(wr8, k)[0:1, :]
        m_s[h:h + 1, :] = jnp.broadcast_to(m_new, (1, LANES))

    @pl.when(c == pl.num_programs(1) - 1)
    def _():
        c_out_ref[0] = c_s[...]
        n_out_ref[0] = n_s[...]
        m_out_ref[0] = m_s[...]


def mlstm_recurrence(q, k, v, gcol, grow, c0, n0, m0, norm_g):
    B, T, _ = q.shape
    nh, dqk, dv = c0.shape[1], c0.shape[2], c0.shape[3]
    lc = min(MLSTM_CHUNK, T)
    m0b = jnp.broadcast_to(m0[:, :, None], (B, nh, LANES)).astype(F32)
    h, c_new, n_new, m_new = pl.pallas_call(
        functools.partial(_mlstm_body, nh=nh, dqk=dqk, dv=dv, lc=lc),
        grid=(B, T // lc),
        in_specs=[pl.BlockSpec((1, lc, nh * dqk), lambda b, c: (b, c, 0)),
                  pl.BlockSpec((1, lc, nh * dqk), lambda b, c: (b, c, 0)),
                  pl.BlockSpec((1, lc, nh * dv), lambda b, c: (b, c, 0)),
                  pl.BlockSpec((1, lc, LANES), lambda b, c: (b, c, 0)),
                  pl.BlockSpec((1, 2 * nh, lc), lambda b, c: (b, 0, c)),
                  pl.BlockSpec((1, nh, dqk, dv), lambda b, c: (b, 0, 0, 0)),
                  pl.BlockSpec((1, nh, dqk), lambda b, c: (b, 0, 0)),
                  pl.BlockSpec((1, nh, LANES), lambda b, c: (b, 0, 0)),
                  pl.BlockSpec((1, nh * dv), lambda b, c: (0, 0))],
        out_specs=[pl.BlockSpec((1, lc, nh * dv), lambda b, c: (b, c, 0)),
                   pl.BlockSpec((1, nh, dqk, dv), lambda b, c: (b, 0, 0, 0)),
                   pl.BlockSpec((1, nh, dqk), lambda b, c: (b, 0, 0)),
                   pl.BlockSpec((1, nh, LANES), lambda b, c: (b, 0, 0))],
        out_shape=[jax.ShapeDtypeStruct((B, T, nh * dv), BF16),
                   jax.ShapeDtypeStruct((B, nh, dqk, dv), F32),
                   jax.ShapeDtypeStruct((B, nh, dqk), F32),
                   jax.ShapeDtypeStruct((B, nh, LANES), F32)],
        scratch_shapes=[pltpu.VMEM((nh, dqk, dv), F32),
                        pltpu.VMEM((nh, dqk), F32),
                        pltpu.VMEM((nh, LANES), F32)],
        compiler_params=_params("parallel", "arbitrary"),
        name="mlstm_recurrence",
    )(q, k, v, gcol, grow, c0.astype(F32), n0.astype(F32), m0b, norm_g.reshape(1, nh * dv))
    return h, c_new, n_new, m_new[:, :, 0]


def _pool_body(xb_ref, st_ref, wg_ref, sc_ref, y_ref, ext_ref, *, tb, pos0, nbuf):
    c = pl.program_id(1)
    lo = POOL_HALO - nbuf
    bb = xb_ref.shape[0]
    gw = wg_ref.shape[1]

    @pl.when(c == 0)
    def _():
        ext_ref[:, lo:POOL_HALO, :] = st_ref[...]

    @pl.when(c > 0)
    def _():
        ext_ref[:, lo:POOL_HALO, :] = ext_ref[:, lo + tb:POOL_HALO + tb, :]

    ext_ref[:, POOL_HALO:POOL_HALO + tb, :] = xb_ref[...].astype(F32)
    pos = pos0 + c * tb + lax.broadcasted_iota(jnp.int32, (1, tb, 1), 1)
    for g, w in enumerate(POOL_WINDOWS):
        cols = slice(g * gw, (g + 1) * gw)
        cur = ext_ref[:, POOL_HALO:POOL_HALO + tb, cols]
        win = cur
        for i in range(1, w):
            win = win + ext_ref[:, POOL_HALO - i:POOL_HALO - i + tb, cols]
        cnt = jnp.minimum(w, pos + 1).astype(F32)
        d = (win / cnt - cur).reshape(bb * tb, gw).astype(BF16)
        yg = _dot(d, wg_ref[g]) * sc_ref[:, cols]
        y_ref[:, :, cols] = yg.reshape(bb, tb, gw).astype(y_ref.dtype)


def pool_mix(xb, state, w_grp, scale, pos0):
    B, T, E = xb.shape
    nbuf = state.shape[1]
    bb, tb = (1, 256) if T % 256 == 0 else (B, T)
    ng, gw = w_grp.shape[0], w_grp.shape[1]
    return pl.pallas_call(
        functools.partial(_pool_body, tb=tb, pos0=pos0, nbuf=nbuf),
        grid=(B // bb, T // tb),
        in_specs=[pl.BlockSpec((bb, tb, E), lambda i, c: (i, c, 0)),
                  pl.BlockSpec((bb, nbuf, E), lambda i, c: (i, 0, 0)),
                  pl.BlockSpec((ng, gw, gw), lambda i, c: (0, 0, 0)),
                  pl.BlockSpec((1, E), lambda i, c: (0, 0))],
        out_specs=pl.BlockSpec((bb, tb, E), lambda i, c: (i, c, 0)),
        out_shape=jax.ShapeDtypeStruct((B, T, E), BF16),
        scratch_shapes=[pltpu.VMEM((bb, POOL_HALO + tb, E), F32)],
        compiler_params=_params("parallel", "arbitrary"),
        name="pool_mix",
    )(xb, state.astype(F32), w_grp, scale.reshape(1, E))


def _tail_body(*refs, kind, final):
    if kind == "mlstm":
        hn_ref, o_ref, xc_ref, z_ref, skip_ref = refs[:5]
        rest = refs[5:]
        mix = (o_ref[...].astype(F32) * hn_ref[...].astype(F32)
               + skip_ref[...] * xc_ref[...].astype(F32)) * _silu(z_ref[...].astype(F32))
    else:
        y_ref, z_ref = refs[:2]
        rest = refs[2:]
        mix = y_ref[...].astype(F32) * _silu(z_ref[...].astype(F32))
    x_ref, p_ref, wd_ref, gw_ref, pw_ref = rest[:5]
    rest = rest[5:]
    x1 = x_ref[...] + _dot(mix.astype(BF16), wd_ref[...])
    gate = _sigmoid(_dot(x1.astype(BF16), gw_ref[...]))
    x2 = x1 + gate * _dot(p_ref[...].astype(BF16), pw_ref[...])
    if final:
        fg_ref, xo_ref, yo_ref = rest
        yo_ref[...] = x2 * lax.rsqrt(jnp.mean(x2 * x2, axis=-1, keepdims=True) + EPS) * fg_ref[...]
    else:
        (xo_ref,) = rest
    xo_ref[...] = x2


def layer_tail(kind, mix_inputs, x, p, w_down, gate_w, ple_w, skip=None, final_g=None):
    M, D = x.shape
    tm = _pick(M, (512, 256))
    row = lambda i: (i, 0)
    fixed = lambda i: (0, 0)
    in_specs, args = [], []
    for a in mix_inputs:
        in_specs.append(pl.BlockSpec((tm, a.shape[1]), row))
        args.append(a)
    if kind == "mlstm":
        in_specs.append(pl.BlockSpec((1, skip.shape[-1]), fixed))
        args.append(skip.reshape(1, -1))
    in_specs += [pl.BlockSpec((tm, D), row), pl.BlockSpec((tm, p.shape[1]), row),
                 pl.BlockSpec(w_down.shape, fixed), pl.BlockSpec(gate_w.shape, fixed),
                 pl.BlockSpec(ple_w.shape, fixed)]
    args += [x, p, w_down, gate_w, ple_w]
    out_specs = [pl.BlockSpec((tm, D), row)]
    out_shape = [jax.ShapeDtypeStruct((M, D), F32)]
    if final_g is not None:
        in_specs.append(pl.BlockSpec((1, D), fixed))
        args.append(final_g.reshape(1, D))
        out_specs.append(pl.BlockSpec((tm, D), row))
        out_shape.append(jax.ShapeDtypeStruct((M, D), F32))
    return pl.pallas_call(
        functools.partial(_tail_body, kind=kind, final=final_g is not None),
        grid=(M // tm,),
        in_specs=in_specs, out_specs=out_specs, out_shape=out_shape,
        compiler_params=_params("parallel"),
        name="layer_tail_" + kind,
    )(*args)


def _sortable_key(s):
    s = jnp.where(s == 0.0, 0.0, s)
    bits = lax.bitcast_convert_type(s, jnp.int32)
    return jnp.where(bits < 0, bits ^ jnp.int32(0x7FFFFFFF), bits)


def _count(keys_ref, nvalid, pred):
    rows, cw = keys_ref.shape[1], keys_ref.shape[2]

    def body(kc, acc):
        hit = jnp.where(pred(keys_ref[kc], kc), 1.0, 0.0)
        part = hit[:, 0:LANES]
        for g in range(1, cw // LANES):
            part = part + hit[:, g * LANES:(g + 1) * LANES]
        return acc + part

    acc = lax.fori_loop(0, nvalid, body, jnp.zeros((rows, LANES), F32))
    return jnp.sum(acc, axis=1, keepdims=True)


def _select_topk(keys_ref, cidx_ref, nvalid, k_top, idx_bits):
    rows, cw = keys_ref.shape[1], keys_ref.shape[2]
    kf = float(k_top)
    imin = jnp.int32(INT_MIN)

    def bit_body(i, tau_u):
        cand_u = tau_u | jnp.left_shift(jnp.int32(1), 31 - i)
        cand = cand_u ^ imin
        cnt = _count(keys_ref, nvalid, lambda kk, kc: kk >= cand)
        return jnp.where(cnt >= kf, cand_u, tau_u)

    tau_u = lax.fori_loop(0, 32, bit_body, jnp.zeros((rows, 1), jnp.int32))
    tau = tau_u ^ imin
    n_gt = _count(keys_ref, nvalid, lambda kk, kc: kk > tau)
    n_ge = _count(keys_ref, nvalid, lambda kk, kc: kk >= tau)
    need = kf - n_gt
    short = tau_u == 0
    cidx_ref[...] = jnp.broadcast_to(jnp.where(short, -1, 2 ** 30), cidx_ref.shape)
    excess = jnp.max(jnp.where((n_ge > kf) & jnp.logical_not(short), 1.0, 0.0))

    @pl.when(excess > 0.5)
    def _():
        lane = lax.broadcasted_iota(jnp.int32, (1, cw), 1)

        def idx_body(i, cut):
            cand = cut | jnp.left_shift(jnp.int32(1), idx_bits - 1 - i)
            cnt = _count(keys_ref, nvalid,
                         lambda kk, kc: (kk == tau) & ((kc * cw + lane) < cand))
            return jnp.where(cnt < need, cand, cut)

        cut = lax.fori_loop(0, idx_bits, idx_body, jnp.zeros((rows, 1), jnp.int32))
        cidx_ref[...] = jnp.broadcast_to(jnp.where(short, -1, cut), cidx_ref.shape)

    return tau


def _selected(kk, colv, tau, cut):
    return (kk > tau) | ((kk == tau) & (colv <= cut))


def _n_causal_chunks(qb, tq, tk):
    return lax.div(qb * tq + (tq - 1), jnp.int32(tk)) + 1


def _idx_prompt_body(qi_ref, wi_ref, ki_ref, bias_ref, keys_ref, cidx_ref, *,
                     nh, dh, tq, tk, nk, k_top, idx_bits):
    per = keys_ref.shape[2] // tk
    gw = per * tk
    qb = pl.program_id(1)
    nvalid = _n_causal_chunks(qb, tq, gw)
    rowpos = qb * tq + lax.broadcasted_iota(jnp.int32, (tq, 1), 0)
    lane = lax.broadcasted_iota(jnp.int32, (1, tk), 1)
    wi = wi_ref[0]

    def score_group(gc, carry):
        for part in range(per):
            kt = ki_ref[0, gc * per + part]
            s = jnp.zeros((tq, tk), F32)
            for h in range(nh):
                rel = jnp.maximum(_dot_nt(qi_ref[0, :, h * dh:(h + 1) * dh], kt), 0.0)
                s = s + rel * wi[:, h:h + 1]
            colv = gc * gw + part * tk + lane
            keys_ref[gc, :, part * tk:(part + 1) * tk] = jnp.where(
                colv <= rowpos, _sortable_key(s), jnp.int32(INT_MIN))
        return carry

    lax.fori_loop(0, nvalid, score_group, 0)
    tau = _select_topk(keys_ref, cidx_ref, nvalid, k_top, idx_bits)
    cut = cidx_ref[:, 0:1]

    def write_group(gc, carry):
        for part in range(per):
            kk = keys_ref[gc, :, part * tk:(part + 1) * tk]
            sel = _selected(kk, gc * gw + part * tk + lane, tau, cut)
            bias_ref[0, 0, gc * per + part] = jnp.where(sel, 0.0, NEG).astype(bias_ref.dtype)
        return carry

    lax.fori_loop(0, nvalid, write_group, 0)

    def fill_chunk(kc, carry):
        bias_ref[0, 0, kc] = jnp.full((tq, tk), NEG, bias_ref.dtype)
        return carry

    lax.fori_loop(nvalid * per, nk, fill_chunk, 0)


def indexer_bias_prompt(qi, wi, ki, k_top):
    B, T, _ = qi.shape
    dh = ki.shape[-1]
    nh = qi.shape[-1] // dh
    tq, tk = min(ATT_TQ, T), min(ATT_TK, T)
    nq, nk = T // tq, T // tk
    per = 2 if nk % 2 == 0 else 1
    idx_bits = int(T).bit_length() + 1
    return pl.pallas_call(
        functools.partial(_idx_prompt_body, nh=nh, dh=dh, tq=tq, tk=tk, nk=nk, k_top=k_top,
                          idx_bits=idx_bits),
        grid=(B, nq),
        in_specs=[pl.BlockSpec((1, tq, nh * dh), lambda b, i: (b, i, 0)),
                  pl.BlockSpec((1, tq, LANES), lambda b, i: (b, i, 0)),
                  pl.BlockSpec((1, nk, tk, dh), lambda b, i: (b, 0, 0, 0))],
        out_specs=pl.BlockSpec((1, 1, nk, tq, tk), lambda b, i: (b, i, 0, 0, 0)),
        out_shape=jax.ShapeDtypeStruct((B, nq, nk, tq, tk), BF16),
        scratch_shapes=[pltpu.VMEM((nk // per, tq, per * tk), jnp.int32),
                        pltpu.VMEM((tq, LANES), jnp.int32)],
        compiler_params=_params("parallel", "arbitrary"),
        name="indexer_bias_prompt",
    )(qi, wi, ki.reshape(B, nk, tk, dh))


def _attn_prompt_body(q_ref, k_ref, v_ref, bias_ref, o_ref, m_s, acc_s, *, nh, dh, tq, tk):
    qb, kb = pl.program_id(1), pl.program_id(2)
    nvalid = _n_causal_chunks(qb, tq, tk)

    @pl.when(kb == 0)
    def _():
        m_s[...] = jnp.full(m_s.shape, NEG, F32)
        acc_s[...] = jnp.zeros(acc_s.shape, F32)

    @pl.when(kb < nvalid)
    def _():
        bias = bias_ref[0, 0, 0].astype(F32)
        for h in range(nh):
            s = _dot_nt(q_ref[0, h], k_ref[0, h]) + bias
            m_prev = m_s[h]
            m_new = jnp.maximum(m_prev, jnp.max(s, axis=1, keepdims=True))
            alpha = jnp.exp2(m_prev - m_new)
            p = jnp.exp2(s - jnp.concatenate([m_new] * (tk // LANES), axis=1))
            acc_s[h] = alpha * acc_s[h] + _dot(p.astype(BF16), v_ref[0, h])
            m_s[h] = m_new

    @pl.when(kb == pl.num_programs(2) - 1)
    def _():
        for h in range(nh):
            acc = acc_s[h]
            o_ref[0, :, h * dh:(h + 1) * dh] = (acc[:, :dh] / acc[:, dh:dh + 1]).astype(o_ref.dtype)


def attention_prompt(q, k, v, bias):
    B, nh, T, dh = q.shape
    assert tuple(v.shape) == (B, nh, T, LANES) and dh < LANES
    _, nq, nk, tq, tk = bias.shape

    def kv_map(b, i, j):
        return (b, 0, jnp.minimum(j, _n_causal_chunks(i, tq, tk) - 1), 0)

    def bias_map(b, i, j):
        return (b, i, jnp.minimum(j, _n_causal_chunks(i, tq, tk) - 1), 0, 0)

    return pl.pallas_call(
        functools.partial(_attn_prompt_body, nh=nh, dh=dh, tq=tq, tk=tk),
        grid=(B, nq, nk),
        in_specs=[pl.BlockSpec((1, nh, tq, dh), lambda b, i, j: (b, 0, i, 0)),
                  pl.BlockSpec((1, nh, tk, dh), kv_map),
                  pl.BlockSpec((1, nh, tk, LANES), kv_map),
                  pl.BlockSpec((1, 1, 1, tq, tk), bias_map)],
        out_specs=pl.BlockSpec((1, tq, nh * dh), lambda b, i, j: (b, i, 0)),
        out_shape=jax.ShapeDtypeStruct((B, T, nh * dh), BF16),
        scratch_shapes=[pltpu.VMEM((nh, tq, LANES), F32),
                        pltpu.VMEM((nh, tq, LANES), F32)],
        compiler_params=_params("parallel", "parallel", "arbitrary"),
        name="attention_prompt",
    )(q, k, v, bias)


def _idx_sample_body(pt_ref, qi_ref, wi_ref, knew_ref, *rest, nh, ts, npg, nsteps, k_top,
                     idx_bits):
    page_refs = rest[:npg]
    bias_ref, keys_ref, cidx_ref = rest[npg:]
    s = pl.program_id(1)
    cw = keys_ref.shape[2]
    wi = wi_ref[0]
    qi = qi_ref[0]

    def scores(kt):
        full = jnp.maximum(_dot_nt(qi, kt), 0.0)
        out = jnp.zeros((ts, cw), F32)
        for h in range(nh):
            out = out + full[h * ts:(h + 1) * ts, :] * wi[:, h:h + 1]
        return out

    kt = jnp.concatenate([r[0] for r in page_refs], axis=0).astype(BF16)
    keys_ref[s] = _sortable_key(scores(kt))

    @pl.when(s == nsteps - 1)
    def _():
        lane = lax.broadcasted_iota(jnp.int32, (1, cw), 1)
        past = nsteps * cw
        rowpos = past + lax.broadcasted_iota(jnp.int32, (ts, 1), 0)
        colv_new = past + lane
        keys_ref[nsteps] = jnp.where(colv_new <= rowpos, _sortable_key(scores(knew_ref[0])),
                                     jnp.int32(INT_MIN))
        tau = _select_topk(keys_ref, cidx_ref, nsteps + 1, k_top, idx_bits)
        cut = cidx_ref[:, 0:1]
        for kc in range(nsteps + 1):
            sel = _selected(keys_ref[kc], kc * cw + lane, tau, cut)
            bias_ref[0, kc] = jnp.where(sel, 0.0, NEG).astype(bias_ref.dtype)


def _page_specs(n_pages_per_step, page, width, layer):
    specs = []
    for i in range(n_pages_per_step):
        def imap(b, s, pt, i=i):
            return (layer, pt[b, s * n_pages_per_step + i], 0, 0)
        specs.append(pl.BlockSpec((None, 1, page, width), imap))
    return specs


def indexer_bias_sample(qi_hq, wi, ki_new_pad, cache_kidx, layer, page_table, k_top):
    B, n_pages = page_table.shape
    _, _, page, dh = cache_kidx.shape
    ts = wi.shape[1]
    nh = qi_hq.shape[1] // ts
    npg = min(PAGES_PER_STEP, n_pages)
    nsteps = n_pages // npg
    cw = npg * page
    idx_bits = int(n_pages * page + cw).bit_length() + 1
    grid_spec = pltpu.PrefetchScalarGridSpec(
        num_scalar_prefetch=1,
        grid=(B, nsteps),
        in_specs=[pl.BlockSpec((1, nh * ts, dh), lambda b, s, pt: (b, 0, 0)),
                  pl.BlockSpec((1, ts, LANES), lambda b, s, pt: (b, 0, 0)),
                  pl.BlockSpec((1, cw, dh), lambda b, s, pt: (b, 0, 0))]
        + _page_specs(npg, page, dh, layer),
        out_specs=pl.BlockSpec((1, nsteps + 1, ts, cw), lambda b, s, pt: (b, 0, 0, 0)),
        scratch_shapes=[pltpu.VMEM((nsteps + 1, ts, cw), jnp.int32),
                        pltpu.VMEM((ts, LANES), jnp.int32)],
    )
    return pl.pallas_call(
        functools.partial(_idx_sample_body, nh=nh, ts=ts, npg=npg, nsteps=nsteps, k_top=k_top,
                          idx_bits=idx_bits),
        grid_spec=grid_spec,
        out_shape=jax.ShapeDtypeStruct((B, nsteps + 1, ts, cw), F32),
        compiler_params=_params("parallel", "arbitrary"),
        name="indexer_bias_sample",
    )(page_table, qi_hq, wi, ki_new_pad, *([cache_kidx] * npg))


def _attn_sample_body(pt_ref, qbd_ref, knew_ref, vnew_ref, bias_ref, *rest, nh, ts, npg, dh):
    kpages, vpages = rest[:npg], rest[npg:2 * npg]
    o_ref, m_s, l_s, acc_s = rest[2 * npg:]
    s = pl.program_id(1)
    rows = nh * ts

    @pl.when(s == 0)
    def _():
        m_s[...] = jnp.full(m_s.shape, NEG, F32)
        l_s[...] = jnp.zeros(l_s.shape, F32)
        acc_s[...] = jnp.zeros(acc_s.shape, F32)

    def update(kt, vt):
        bias = bias_ref[0, 0]
        logits = _dot_nt(qbd_ref[0], kt) + jnp.concatenate([bias] * nh, axis=0)
        m_prev = m_s[...]
        m_new = jnp.maximum(m_prev, jnp.max(logits, axis=1, keepdims=True))
        alpha = jnp.exp2(m_prev - m_new)
        p = jnp.exp2(logits - m_new)
        l_s[...] = alpha * l_s[...] + jnp.sum(p, axis=1, keepdims=True)
        acc_s[...] = alpha * acc_s[...] + _dot(p.astype(BF16), vt)
        m_s[...] = m_new

    @pl.when(s == 0)
    def _():
        update(knew_ref[0], vnew_ref[0])

    def gather_pages(page_refs):
        mats = []
        for r in page_refs:
            page = r.shape[0]
            flat = r.reshape(page * nh, dh)
            mats.append(jnp.concatenate(
                [flat[pl.ds(h, page, stride=nh), :] for h in range(nh)], axis=1))
        return jnp.concatenate(mats, axis=0).astype(BF16)

    @pl.when(s > 0)
    def _():
        update(gather_pages(kpages), gather_pages(vpages))

    @pl.when(s == pl.num_programs(1) - 1)
    def _():
        o = acc_s[...] / l_s[...]
        lane = lax.broadcasted_iota(jnp.int32, (1, nh * dh), 1)
        res = jnp.zeros((ts, nh * dh), F32)
        for h in range(nh):
            own = (lane >= h * dh) & (lane < (h + 1) * dh)
            res = res + jnp.where(own, o[h * ts:(h + 1) * ts, :], 0.0)
        o_ref[0] = res.astype(o_ref.dtype)


def attention_sample(qbd, k_new_pad, v_new_pad, bias, cache_k, cache_v, layer, page_table, ts):
    B, n_pages = page_table.shape
    _, _, page, nh, dh = cache_k.shape
    width = nh * dh
    nsteps1, cw = bias.shape[1], bias.shape[3]
    npg = cw // page
    rows = nh * ts

    def cache_specs():
        specs = []
        for i in range(npg):
            def imap(b, s, pt, i=i):
                return (layer, pt[b, jnp.maximum(s - 1, 0) * npg + i], 0, 0, 0)
            specs.append(pl.BlockSpec((None, None, page, nh, dh), imap))
        return specs

    def bias_map(b, s, pt):
        return (b, jnp.where(s == 0, nsteps1 - 1, s - 1), 0, 0)

    grid_spec = pltpu.PrefetchScalarGridSpec(
        num_scalar_prefetch=1,
        grid=(B, nsteps1),
        in_specs=[pl.BlockSpec((1, rows, width), lambda b, s, pt: (b, 0, 0)),
                  pl.BlockSpec((1, cw, width), lambda b, s, pt: (b, 0, 0)),
                  pl.BlockSpec((1, cw, width), lambda b, s, pt: (b, 0, 0)),
                  pl.BlockSpec((1, 1, ts, cw), bias_map)]
        + cache_specs() + cache_specs(),
        out_specs=pl.BlockSpec((1, ts, width), lambda b, s, pt: (b, 0, 0)),
        scratch_shapes=[pltpu.VMEM((rows, 1), F32),
                        pltpu.VMEM((rows, 1), F32),
                        pltpu.VMEM((rows, width), F32)],
    )
    return pl.pallas_call(
        functools.partial(_attn_sample_body, nh=nh, ts=ts, npg=npg, dh=dh),
        grid_spec=grid_spec,
        out_shape=jax.ShapeDtypeStruct((B, ts, width), BF16),
        compiler_params=_params("parallel", "arbitrary"),
        name="attention_sample",
    )(page_table, qbd, k_new_pad, v_new_pad, bias, *([cache_k] * npg), *([cache_v] * npg))


def _pad_time(a, t_pad, value=0.0):
    return jnp.pad(a, ((0, 0), (0, t_pad - a.shape[1]), (0, 0)), constant_values=value)


def _mlstm_layer(x, g, B, T, state, w, j):
    c0, n0, m0, conv0 = state
    nh = c0.shape[1]
    E = w["a_conv_w"].shape[-1]
    w_up = w["a_w_up"][j]
    xb, = norm_matmul(x, g, w_up[:, :E], (BF16,))
    z, = norm_matmul(x, g, w_up[:, E:], (BF16,))
    xc = conv_silu(xb.reshape(B, T, E), conv0, w["a_conv_w_f32"][j], w["a_conv_b"][j]).reshape(B * T, E)
    q = matmul(xc, w["a_w_q"][j], BF16)
    k = matmul(xc, w["a_w_k"][j], BF16)
    v = matmul(xb, w["a_w_v"][j], BF16)
    o = matmul(xb, w["a_w_o"][j], BF16, bias=w["a_b_o"][j], act="sigmoid")
    gates = matmul(xc, w["a_w_if_pad"][j], F32, bias=w["a_b_if_pad"][j], act="gates", n_li=nh)
    gcol = gates.reshape(B, T, LANES)
    q3, k3, v3 = q.reshape(B, T, -1), k.reshape(B, T, -1), v.reshape(B, T, -1)
    tp = T if T % LANES == 0 else ((T + LANES - 1) // LANES) * LANES
    if tp != T:
        q3, k3, v3 = _pad_time(q3, tp), _pad_time(k3, tp), _pad_time(v3, tp)
        pad_row = jnp.where(jnp.arange(LANES) < nh, NEG, 0.0).astype(F32)
        gcol = jnp.concatenate([gcol, jnp.broadcast_to(pad_row, (B, tp - T, LANES))], axis=1)
    grow = jnp.swapaxes(gcol[:, :, :2 * nh], 1, 2)
    hn, c_new, n_new, m_new = mlstm_recurrence(q3, k3, v3, gcol, grow, c0, n0, m0, w["a_norm_g"][j])
    hn = hn[:, :T].reshape(B * T, -1)
    assert T >= CONV_W - 1
    conv_new = xb.reshape(B, T, E)[:, T - (CONV_W - 1):].astype(F32)
    return (hn, o, xc, z), (c_new, n_new, m_new, conv_new)


def _rope_tables(pos):
    half = 32
    inv = ROPE_THETA ** (-np.arange(half, dtype=np.float64) / half)
    ang = np.asarray(pos, np.float64)[:, None] * inv[None, :]
    cos = np.tile(np.cos(ang), (1, 4))
    sin = np.tile(np.concatenate([-np.sin(ang), np.sin(ang)], axis=1), (1, 2))
    return jnp.asarray(cos, F32), jnp.asarray(sin, F32)


def _dsa_project(x, g, B, T, pos, w, j, nh, dh, nhi, di, head_major=False):
    M = B * T
    cos, sin = _rope_tables(pos)
    tm = _pick(T, ROW_TILES) if T % ROW_TILES[-1] == 0 else _pick(M, ROW_TILES)
    if T % tm == 0:
        tabs, tab_blocks = (cos, sin), T // tm
    else:
        tabs, tab_blocks = (jnp.tile(cos, (M // T, 1)), jnp.tile(sin, (M // T, 1))), M // tm
    rope = dict(rope_tabs=tabs, tab_blocks=tab_blocks, row_tile=tm)
    hm = (lambda *widths: (B, T, dh, widths)) if head_major else (lambda *widths: None)
    q, = norm_matmul(x, g, w["b_wq"][j], (BF16,), scale=dh ** -0.5 * math.log2(math.e),
                     head_major=hm(dh), **rope)
    k32, k16 = norm_matmul(x, g, w["b_wk"][j], (F32, BF16), head_major=hm(0, dh), **rope)
    v32, v16 = norm_matmul(x, g, w["b_wv"][j], (F32, BF16), head_major=hm(0, LANES), row_tile=tm)
    gate, = norm_matmul(x, g, w["b_wg"][j], (BF16,))
    qi, = norm_matmul(x, g, w["b_wqi"][j], (BF16,), **rope)
    ki32, = norm_matmul(x, g, w["b_wki_pad"][j], (F32,), **rope)
    wi, = norm_matmul(x, g, w["b_wwi_pad"][j], (F32,), scale=nhi ** -0.5 * di ** -0.5)
    return q, k32, k16, v32, v16, gate, qi, ki32[:, :di], wi


def _dsa_layer_prompt(x, g, B, T, w, j, dims):
    nh, dh, nhi, di = dims
    q, k32, k16, v32, v16, gate, qi, ki32, wi = _dsa_project(
        x, g, B, T, np.arange(T), w, j, *dims, head_major=True)
    k_top = min(TOPK_MAX, T // 4)
    bias = indexer_bias_prompt(qi.reshape(B, T, nhi * di), wi.reshape(B, T, LANES),
                               ki32.astype(BF16).reshape(B, T, di), k_top)
    attn = attention_prompt(q, k16, v16, bias).reshape(B * T, nh * dh)
    new =(k32.reshape(B, T, nh, dh), v32.reshape(B, T, nh, dh), ki32.reshape(B, T, di))
    return (attn, gate), new


def _dsa_layer_sample(x, g, B, T, w, j, dims, ctx):
    nh, dh, nhi, di = dims
    cache_k, cache_v, cache_kidx, page_table = ctx
    n_pages = page_table.shape[1]
    page = cache_kidx.shape[2]
    past = n_pages * page
    q, k32, k16, v32, v16, gate, qi, ki32, wi = _dsa_project(
        x, g, B, T, past + np.arange(T), w, j, *dims)
    k_top = min(TOPK_MAX, (past + T) // 4)
    cw = min(PAGES_PER_STEP, n_pages) * page
    qi_hq = jnp.transpose(qi.reshape(B, T, nhi, di), (0, 2, 1, 3)).reshape(B, nhi * T, di)
    ki_new = _pad_time(ki32.astype(BF16).reshape(B, T, di), cw)
    bias = indexer_bias_sample(qi_hq, wi.reshape(B, T, LANES), ki_new,
                               cache_kidx, j, page_table, k_top)
    q3 = q.reshape(B, T, nh * dh)
    own = (jnp.arange(nh * T)[:, None] // T) == (jnp.arange(nh * dh)[None, :] // dh)
    qbd = jnp.where(own[None], jnp.tile(q3, (1, nh, 1)), 0).astype(BF16)
    width = nh * dh
    attn = attention_sample(qbd, _pad_time(k16.reshape(B, T, width), cw),
                            _pad_time(v16.reshape(B, T, width), cw), bias,
                            cache_k, cache_v, j, page_table, T)
    new = (k32.reshape(B, T, nh, dh), v32.reshape(B, T, nh, dh), ki32.reshape(B, T, di))
    return (attn.reshape(B * T, width), gate), new


def _pool_layer(x, g, B, T, buf, pos0, w, j):
    E = w["c_scale"].shape[-1]
    w_up = w["c_w_up"][j]
    xb, = norm_matmul(x, g, w_up[:, :E], (BF16,))
    z, = norm_matmul(x, g, w_up[:, E:], (BF16,))
    xb3 = xb.reshape(B, T, E)
    y = pool_mix(xb3, buf, w["c_w_grp"][j], w["c_scale"][j], pos0).reshape(B * T, E)
    nbuf = buf.shape[1]
    if T >= nbuf:
        new_buf = xb3[:, T - nbuf:].astype(F32)
    else:
        new_buf = jnp.concatenate([buf[:, T:].astype(F32), xb3.astype(F32)], axis=1)
    return (y, z), new_buf


def _run_group(x3, p4, pos0, mlstm_state, dsa_ctx, pool_state, w, dims):
    B, T, D = x3.shape
    depth = p4.shape[0]
    x = x3.reshape(B * T, D)
    new_a, new_b, new_c = [], [], []
    y = None
    for i in range(depth):
        kind, j = i % 3, i // 3
        g = w["norm_g"][i]
        final_g = w["final_g"] if i == depth - 1 else None
        p = p4[i].reshape(B * T, -1)
        if kind == 0:
            state = tuple(s[j] for s in mlstm_state)
            mix, st = _mlstm_layer(x, g, B, T, state, w, j)
            new_a.append(st)
            out = layer_tail("mlstm", mix, x, p, w["a_w_down"][j], w["ple_gate_w"][i], w["ple_w"][i],
                             skip=w["a_skip"][j], final_g=final_g)
        elif kind == 1:
            if dsa_ctx is None:
                mix, st = _dsa_layer_prompt(x, g, B, T, w, j, dims)
            else:
                mix, st = _dsa_layer_sample(x, g, B, T, w, j, dims, dsa_ctx)
            new_b.append(st)
            out = layer_tail("gated", mix, x, p, w["b_w_out"][j], w["ple_gate_w"][i], w["ple_w"][i],
                             final_g=final_g)
        else:
            mix, st = _pool_layer(x, g, B, T, pool_state[j], pos0, w, j)
            new_c.append(st)
            out = layer_tail("gated", mix, x, p, w["c_w_down"][j], w["ple_gate_w"][i], w["ple_w"][i],
                             final_g=final_g)
        x = out[0]
        if final_g is not None:
            y = out[1]
    a_states = tuple(jnp.stack([s[r] for s in new_a]) for r in range(4))
    b_states = tuple(jnp.stack([s[r] for s in new_b]) for r in range(3))
    c_state = jnp.stack(new_c)
    return y.reshape(B, T, D), a_states, b_states, c_state


def kernel(x_prompt, x_sample, state_mlstm_C, state_mlstm_n, state_mlstm_m, state_mlstm_conv, state_pool,
           cache_k, cache_v, cache_kidx, page_table, p_prompt, p_sample,
           norm_g, final_g, ple_w, ple_gate_w,
           a_w_up, a_conv_w, a_conv_b, a_w_q, a_w_k, a_w_v, a_w_if, a_b_if, a_w_o, a_b_o, a_norm_g,
           a_skip, a_w_down, b_w_in, b_w_out, c_w_up, c_w_grp, c_scale, c_w_down):
    D = x_prompt.shape[-1]
    nh, dh = cache_k.shape[3], cache_k.shape[4]
    di = cache_kidx.shape[-1]
    aw = nh * dh
    nhi = (b_w_in.shape[-1] - 4 * aw - di) // (di + 1)
    dims = (nh, dh, nhi, di)
    nha = state_mlstm_C.shape[2]
    bf = lambda a: a.astype(BF16)

    def pad_cols(a, n):
        return jnp.pad(a, [(0, 0)] * (a.ndim - 1) + [(0, n - a.shape[-1])])

    o1, o2, o3, o4, o5, o6 = np.cumsum([aw, aw, aw, aw, nhi * di, di])
    w = dict(
        norm_g=norm_g, final_g=final_g, ple_w=bf(ple_w), ple_gate_w=bf(ple_gate_w),
        a_w_up=bf(a_w_up), a_conv_w_f32=a_conv_w, a_conv_w=a_conv_w, a_conv_b=a_conv_b,
        a_w_q=bf(a_w_q), a_w_k=bf(a_w_k), a_w_v=bf(a_w_v), a_w_o=bf(a_w_o), a_b_o=a_b_o,
        a_w_if_pad=bf(pad_cols(a_w_if, LANES)), a_b_if_pad=pad_cols(a_b_if, LANES),
        a_norm_g=a_norm_g, a_skip=a_skip, a_w_down=bf(a_w_down),
        b_wq=bf(b_w_in[..., :o1]), b_wk=bf(b_w_in[..., o1:o2]), b_wv=bf(b_w_in[..., o2:o3]),
        b_wg=bf(b_w_in[..., o3:o4]), b_wqi=bf(b_w_in[..., o4:o5]),
        b_wki_pad=bf(pad_cols(b_w_in[..., o5:o6], LANES)),
        b_wwi_pad=bf(pad_cols(b_w_in[..., o6:], LANES)),
        b_w_out=bf(b_w_out), c_w_up=bf(c_w_up), c_w_grp=bf(c_w_grp), c_scale=c_scale,
        c_w_down=bf(c_w_down),
    )

    Bp = x_prompt.shape[0]
    na, nc = state_mlstm_C.shape[0], state_pool.shape[0]
    zeros_a = (jnp.zeros((na, Bp) + state_mlstm_C.shape[2:], F32),
               jnp.zeros((na, Bp) + state_mlstm_n.shape[2:], F32),
               jnp.zeros((na, Bp) + state_mlstm_m.shape[2:], F32),
               jnp.zeros((na, Bp) + state_mlstm_conv.shape[2:], F32))
    zeros_c = jnp.zeros((nc, Bp) + state_pool.shape[2:], F32)
    y_p, a_p, b_p, c_p = _run_group(x_prompt, p_prompt, 0, zeros_a, None, zeros_c, w, dims)

    past = page_table.shape[1] * cache_k.shape[2]
    y_s, a_s, b_s, c_s = _run_group(
        x_sample, p_sample, past,
        (state_mlstm_C, state_mlstm_n, state_mlstm_m, state_mlstm_conv),
        (cache_k, cache_v, cache_kidx, page_table), state_pool, w, dims)
    return (y_p, y_s, *a_p, *b_p, c_p, *a_s, *b_s, c_s)
```

```python
import functools
import math

import numpy as np
import jax
import jax.numpy as jnp
from jax import lax
from jax.experimental import pallas as pl
from jax.experimental.pallas import tpu as pltpu

EPS = 1e-6
ROPE_THETA = 10000.0
TOPK_MAX = 256
POOL_WINDOWS = (2, 4, 8, 16)
CONV_W = 4
MLSTM_CHUNK = 256
NEG = -1e30
LANES = 128
SUBLANES = 8
VMEM_LIMIT_BYTES = 56 * 1024 * 1024
ATT_TQ, ATT_TK = 256, 512
PAGES_PER_STEP = 8
ROW_TILES = (1024, 512, 256)
COL_TILES = (1024, 512, 256, 128)

F32 = jnp.float32
BF16 = jnp.bfloat16
INT_MIN = -2 ** 31


def _params(*sem):
    return pltpu.CompilerParams(dimension_semantics=sem, vmem_limit_bytes=VMEM_LIMIT_BYTES)


def _sigmoid(x):
    return 1.0 / (1.0 + jnp.exp(-x))


def _silu(x):
    return x * _sigmoid(x)


def _log_sigmoid(x):
    return jnp.minimum(x, 0.0) - jnp.log(1.0 + jnp.exp(-jnp.abs(x)))


def _dot(a, b):
    return jnp.dot(a, b, preferred_element_type=F32)


def _dot_nt(a, b):
    return lax.dot_general(a, b, (((1,), (1,)), ((), ())), preferred_element_type=F32)


def _dot_tn(a, b):
    return lax.dot_general(a, b, (((0,), (0,)), ((), ())), preferred_element_type=F32)


def _pick(n, cands):
    for c in cands:
        if n % c == 0:
            return c
    return n


def _rope_tile(a, cos, sin_signed):
    lane = lax.broadcasted_iota(jnp.int32, (1, LANES), 1)
    first_half = (lane % 64) < 32
    pieces = []
    for gi in range(a.shape[1] // LANES):
        ag = a[:, gi * LANES:(gi + 1) * LANES]
        ahead = pltpu.roll(ag, LANES - 32, 1)
        behind = pltpu.roll(ag, 32, 1)
        rot = jnp.where(first_half, ahead, behind)
        pieces.append(ag * cos + rot * sin_signed)
    return pieces[0] if len(pieces) == 1 else jnp.concatenate(pieces, axis=1)


def _norm_matmul_body(*refs, rope, n_out, scale, head_dim):
    if rope:
        x_ref, g_ref, w_ref, cos_ref, sin_ref = refs[:5]
        rest = refs[5:]
    else:
        x_ref, g_ref, w_ref = refs[:3]
        rest = refs[3:]
    outs, xn_ref = rest[:n_out], rest[n_out]

    @pl.when(pl.program_id(1) == 0)
    def _():
        xf = x_ref[...]
        y = xf * lax.rsqrt(jnp.mean(xf * xf, axis=-1, keepdims=True) + EPS) * g_ref[...]
        xn_ref[...] = y.astype(BF16)

    acc = _dot(xn_ref[...], w_ref[...])
    if rope:
        acc = _rope_tile(acc, cos_ref[...], sin_ref[...])
    if scale != 1.0:
        acc = acc * scale
    for o in outs:
        if len(o.shape) == 4:
            npad = o.shape[3] - head_dim
            if npad:
                lane = lax.broadcasted_iota(jnp.int32, (acc.shape[0], npad), 1)
                pad = jnp.where(lane == 0, 1.0, 0.0)
            for hh in range(o.shape[1]):
                piece = acc[:, hh * head_dim:(hh + 1) * head_dim]
                if npad:
                    piece = jnp.concatenate([piece, pad], axis=1)
                o[0, hh] = piece.astype(o.dtype)
        else:
            o[...] = acc.astype(o.dtype)


def norm_matmul(x, g, w, out_dtypes, rope_tabs=None, tab_blocks=1, scale=1.0, head_major=None,
                row_tile=None):
    M, D = x.shape
    N = w.shape[1]
    tm = row_tile if row_tile is not None else _pick(M, ROW_TILES)
    tn = _pick(N, COL_TILES)
    out_specs = [pl.BlockSpec((tm, tn), lambda i, j: (i, j)) for _ in out_dtypes]
    out_shape = [jax.ShapeDtypeStruct((M, N), dt) for dt in out_dtypes]
    head_dim = 0
    if head_major is not None:
        hb, ht, head_dim, flags = head_major
        assert ht % tm == 0
        tpb = ht // tm
        for idx, width in enumerate(flags):
            if width:
                out_specs[idx] = pl.BlockSpec((1, tn // head_dim, tm, width),
                                              lambda i, j: (i // tpb, j, i % tpb, 0))
                out_shape[idx] = jax.ShapeDtypeStruct((hb, N // head_dim, ht, width), out_dtypes[idx])
    in_specs = [pl.BlockSpec((tm, D), lambda i, j: (i, 0)),
                pl.BlockSpec((1, D), lambda i, j: (0, 0)),
                pl.BlockSpec((D, tn), lambda i, j: (0, j))]
    args = [x, g.reshape(1, D), w]
    if rope_tabs is not None:
        for t in rope_tabs:
            in_specs.append(pl.BlockSpec((tm, LANES), lambda i, j: (i % tab_blocks, 0)))
            args.append(t)
    return pl.pallas_call(
        functools.partial(_norm_matmul_body, rope=rope_tabs is not None, n_out=len(out_dtypes),
                          scale=scale, head_dim=head_dim),
        grid=(M // tm, N // tn),
        in_specs=in_specs,
        out_specs=out_specs,
        out_shape=out_shape,
        scratch_shapes=[pltpu.VMEM((tm, D), BF16)],
        compiler_params=_params("parallel", "arbitrary"),
        name="norm_matmul",
    )(*args)


def _matmul_body(*refs, has_bias, act, n_li):
    if has_bias:
        x_ref, w_ref, b_ref, o_ref = refs
    else:
        x_ref, w_ref, o_ref = refs
    acc = _dot(x_ref[...], w_ref[...])
    if has_bias:
        acc = acc + b_ref[...]
    if act == "sigmoid":
        acc = _sigmoid(acc)
    elif act == "gates":
        col = lax.broadcasted_iota(jnp.int32, acc.shape, 1)
        acc = jnp.where(col < n_li, acc, _log_sigmoid(acc))
    o_ref[...] = acc.astype(o_ref.dtype)


def matmul(x, w, out_dtype, bias=None, act=None, n_li=0):
    M, K = x.shape
    N = w.shape[1]
    tm = _pick(M, ROW_TILES)
    tn = _pick(N, COL_TILES)
    in_specs = [pl.BlockSpec((tm, K), lambda i, j: (i, 0)),
                pl.BlockSpec((K, tn), lambda i, j: (0, j))]
    args = [x, w]
    if bias is not None:
        in_specs.append(pl.BlockSpec((1, tn), lambda i, j: (0, j)))
        args.append(bias.reshape(1, N).astype(F32))
    return pl.pallas_call(
        functools.partial(_matmul_body, has_bias=bias is not None, act=act, n_li=n_li),
        grid=(M // tm, N // tn),
        in_specs=in_specs,
        out_specs=pl.BlockSpec((tm, tn), lambda i, j: (i, j)),
        out_shape=jax.ShapeDtypeStruct((M, N), out_dtype),
        compiler_params=_params("parallel", "parallel"),
        name="matmul",
    )(*args)


CONV_HALO = 8
POOL_HALO = 16


def _conv_body(xb_ref, st_ref, w_ref, b_ref, xc_ref, ext_ref, *, tb):
    c = pl.program_id(1)
    nst = CONV_W - 1
    lo = CONV_HALO - nst

    @pl.when(c == 0)
    def _():
        ext_ref[:, lo:CONV_HALO, :] = st_ref[...]

    @pl.when(c > 0)
    def _():
        ext_ref[:, lo:CONV_HALO, :] = ext_ref[:, lo + tb:CONV_HALO + tb, :]

    ext_ref[:, CONV_HALO:CONV_HALO + tb, :] = xb_ref[...].astype(F32)
    acc = b_ref[...][None] + ext_ref[:, lo:lo + tb, :] * w_ref[0:1, :][None]
    for j in range(1, CONV_W):
        acc = acc + ext_ref[:, lo + j:lo + j + tb, :] * w_ref[j:j + 1, :][None]
    xc_ref[...] = _silu(acc).astype(xc_ref.dtype)


def conv_silu(xb, state, w, b):
    B, T, E = xb.shape
    bb, tb = (1, 256) if T % 256 == 0 else (B, T)
    return pl.pallas_call(
        functools.partial(_conv_body, tb=tb),
        grid=(B // bb, T // tb),
        in_specs=[pl.BlockSpec((bb, tb, E), lambda i, c: (i, c, 0)),
                  pl.BlockSpec((bb, CONV_W - 1, E), lambda i, c: (i, 0, 0)),
                  pl.BlockSpec((CONV_W, E), lambda i, c: (0, 0)),
                  pl.BlockSpec((1, E), lambda i, c: (0, 0))],
        out_specs=pl.BlockSpec((bb, tb, E), lambda i, c: (i, c, 0)),
        out_shape=jax.ShapeDtypeStruct((B, T, E), BF16),
        scratch_shapes=[pltpu.VMEM((bb, CONV_HALO + tb, E), F32)],
        compiler_params=_params("parallel", "arbitrary"),
        name="conv_silu",
    )(xb, state.astype(F32), w, b.reshape(1, E))


def _mlstm_body(q_ref, k_ref, v_ref, gc_ref, gr_ref, c0_ref, n0_ref, m0_ref, ng_ref,
                h_ref, c_out_ref, n_out_ref, m_out_ref, c_s, n_s, m_s, *, nh, dqk, dv, lc):
    c = pl.program_id(1)

    @pl.when(c == 0)
    def _():
        c_s[...] = c0_ref[0]
        n_s[...] = n0_ref[0]
        m_s[...] = m0_ref[0]

    row = lax.broadcasted_iota(jnp.int32, (lc, lc), 0)
    col = lax.broadcasted_iota(jnp.int32, (lc, lc), 1)
    causal = row >= col
    scale = dqk ** -0.5
    gc = gc_ref[0]
    gr = gr_ref[0]
    for h in range(nh):
        q = (q_ref[0, :, h * dqk:(h + 1) * dqk].astype(F32) * scale).astype(BF16)
        k = k_ref[0, :, h * dqk:(h + 1) * dqk]
        v = v_ref[0, :, h * dv:(h + 1) * dv]
        li_c, lf_c = gc[:, h:h + 1], gc[:, nh + h:nh + h + 1]
        li_r, lf_r = gr[h:h + 1, :], gr[nh + h:nh + h + 1, :]
        b_c = jnp.sum(jnp.where(causal, lf_r, 0.0), axis=1, keepdims=True)
        b_r = jnp.sum(jnp.where(row <= col, lf_c, 0.0), axis=0, keepdims=True)
        m_prev = m_s[h:h + 1, 0:1]
        dm = jnp.where(causal, b_c - b_r + li_r, NEG)
        m_inter = b_c + m_prev
        m_t = jnp.maximum(m_inter, jnp.max(dm, axis=1, keepdims=True))
        a = _dot_nt(q, k) * jnp.exp(dm - m_t)
        inter = jnp.exp(m_inter - m_t)
        c_prev = c_s[h]
        n_prev = n_s[h:h + 1, :]
        num = _dot(a.astype(BF16), v) + inter * _dot(q, c_prev.astype(BF16))
        qn = jnp.sum(q.astype(F32) * n_prev, axis=1, keepdims=True)
        den = jnp.sum(a, axis=1, keepdims=True) + inter * qn
        hh = num / jnp.maximum(jnp.abs(den), jnp.exp(-m_t))
        hn = hh * lax.rsqrt(jnp.mean(hh * hh, axis=-1, keepdims=True) + EPS) \
            * ng_ref[:, h * dv:(h + 1) * dv]
        h_ref[0, :, h * dv:(h + 1) * dv] = hn.astype(h_ref.dtype)
        b_last = b_c[lc - 1:lc, :]
        g_r = b_last - b_r + li_r
        g_c = b_last - b_c + li_c
        m_new = jnp.maximum(b_last + m_prev, jnp.max(g_r, axis=1, keepdims=True))
        w_r = jnp.exp(g_r - m_new)
        w_c = jnp.exp(g_c - m_new)
        decay = jnp.exp(b_last + m_prev - m_new)
        wv = (w_c * v.astype(F32)).astype(BF16)
        c_s[h] = decay * c_prev + _dot_tn(k, wv)
        wr8 = jnp.broadcast_to(w_r, (SUBLANES, lc)).astype(BF16)
        n_s[h:h + 1, :] = decay * n_prev + _dot(wr8, k)[0:1, :]
        m_s[h:h + 1, :] = jnp.broadcast_to(m_new, (1, LANES))

    @pl.when(c == pl.num_programs(1) - 1)
    def _():
        c_out_ref[0] = c_s[...]
        n_out_ref[0] = n_s[...]
        m_out_ref[0] = m_s[...]


def mlstm_recurrence(q, k, v, gcol, grow, c0, n0, m0, norm_g):
    B, T, _ = q.shape
    nh, dqk, dv = c0.shape[1], c0.shape[2], c0.shape[3]
    lc = min(MLSTM_CHUNK, T)
    m0b = jnp.broadcast_to(m0[:, :, None], (B, nh, LANES)).astype(F32)
    h, c_new, n_new, m_new = pl.pallas_call(
        functools.partial(_mlstm_body, nh=nh, dqk=dqk, dv=dv, lc=lc),
        grid=(B, T // lc),
        in_specs=[pl.BlockSpec((1, lc, nh * dqk), lambda b, c: (b, c, 0)),
                  pl.BlockSpec((1, lc, nh * dqk), lambda b, c: (b, c, 0)),
                  pl.BlockSpec((1, lc, nh * dv), lambda b, c: (b, c, 0)),
                  pl.BlockSpec((1, lc, LANES), lambda b, c: (b, c, 0)),
                  pl.BlockSpec((1, 2 * nh, lc), lambda b, c: (b, 0, c)),
                  pl.BlockSpec((1, nh, dqk, dv), lambda b, c: (b, 0, 0, 0)),
                  pl.BlockSpec((1, nh, dqk), lambda b, c: (b, 0, 0)),
                  pl.BlockSpec((1, nh, LANES), lambda b, c: (b, 0, 0)),
                  pl.BlockSpec((1, nh * dv), lambda b, c: (0, 0))],
        out_specs=[pl.BlockSpec((1, lc, nh * dv), lambda b, c: (b, c, 0)),
                   pl.BlockSpec((1, nh, dqk, dv), lambda b, c: (b, 0, 0, 0)),
                   pl.BlockSpec((1, nh, dqk), lambda b, c: (b, 0, 0)),
                   pl.BlockSpec((1, nh, LANES), lambda b, c: (b, 0, 0))],
        out_shape=[jax.ShapeDtypeStruct((B, T, nh * dv), BF16),
                   jax.ShapeDtypeStruct((B, nh, dqk, dv), F32),
                   jax.ShapeDtypeStruct((B, nh, dqk), F32),
                   jax.ShapeDtypeStruct((B, nh, LANES), F32)],
        scratch_shapes=[pltpu.VMEM((nh, dqk, dv), F32),
                        pltpu.VMEM((nh, dqk), F32),
                        pltpu.VMEM((nh, LANES), F32)],
        compiler_params=_params("parallel", "arbitrary"),
        name="mlstm_recurrence",
    )(q, k, v, gcol, grow, c0.astype(F32), n0.astype(F32), m0b, norm_g.reshape(1, nh * dv))
    return h, c_new, n_new, m_new[:, :, 0]


def _pool_body(xb_ref, st_ref, wg_ref, sc_ref, y_ref, ext_ref, *, tb, pos0, nbuf):
    c = pl.program_id(1)
    lo = POOL_HALO - nbuf
    bb = xb_ref.shape[0]
    gw = wg_ref.shape[1]

    @pl.when(c == 0)
    def _():
        ext_ref[:, lo:POOL_HALO, :] = st_ref[...]

    @pl.when(c > 0)
    def _():
        ext_ref[:, lo:POOL_HALO, :] = ext_ref[:, lo + tb:POOL_HALO + tb, :]

    ext_ref[:, POOL_HALO:POOL_HALO + tb, :] = xb_ref[...].astype(F32)
    pos = pos0 + c * tb + lax.broadcasted_iota(jnp.int32, (1, tb, 1), 1)
    for g, w in enumerate(POOL_WINDOWS):
        cols = slice(g * gw, (g + 1) * gw)
        cur = ext_ref[:, POOL_HALO:POOL_HALO + tb, cols]
        win = cur
        for i in range(1, w):
            win = win + ext_ref[:, POOL_HALO - i:POOL_HALO - i + tb, cols]
        cnt = jnp.minimum(w, pos + 1).astype(F32)
        d = (win / cnt - cur).reshape(bb * tb, gw).astype(BF16)
        yg = _dot(d, wg_ref[g]) * sc_ref[:, cols]
        y_ref[:, :, cols] = yg.reshape(bb, tb, gw).astype(y_ref.dtype)


def pool_mix(xb, state, w_grp, scale, pos0):
    B, T, E = xb.shape
    nbuf = state.shape[1]
    bb, tb = (1, 256) if T % 256 == 0 else (B, T)
    ng, gw = w_grp.shape[0], w_grp.shape[1]
    return pl.pallas_call(
        functools.partial(_pool_body, tb=tb, pos0=pos0, nbuf=nbuf),
        grid=(B // bb, T // tb),
        in_specs=[pl.BlockSpec((bb, tb, E), lambda i, c: (i, c, 0)),
                  pl.BlockSpec((bb, nbuf, E), lambda i, c: (i, 0, 0)),
                  pl.BlockSpec((ng, gw, gw), lambda i, c: (0, 0, 0)),
                  pl.BlockSpec((1, E), lambda i, c: (0, 0))],
        out_specs=pl.BlockSpec((bb, tb, E), lambda i, c: (i, c, 0)),
        out_shape=jax.ShapeDtypeStruct((B, T, E), BF16),
        scratch_shapes=[pltpu.VMEM((bb, POOL_HALO + tb, E), F32)],
        compiler_params=_params("parallel", "arbitrary"),
        name="pool_mix",
    )(xb, state.astype(F32), w_grp, scale.reshape(1, E))


def _tail_body(*refs, kind, final):
    if kind == "mlstm":
        hn_ref, o_ref, xc_ref, z_ref, skip_ref = refs[:5]
        rest = refs[5:]
        mix = (o_ref[...].astype(F32) * hn_ref[...].astype(F32)
               + skip_ref[...] * xc_ref[...].astype(F32)) * _silu(z_ref[...].astype(F32))
    else:
        y_ref, z_ref = refs[:2]
        rest = refs[2:]
        mix = y_ref[...].astype(F32) * _silu(z_ref[...].astype(F32))
    x_ref, p_ref, wd_ref, gw_ref, pw_ref = rest[:5]
    rest = rest[5:]
    x1 = x_ref[...] + _dot(mix.astype(BF16), wd_ref[...])
    gate = _sigmoid(_dot(x1.astype(BF16), gw_ref[...]))
    x2 = x1 + gate * _dot(p_ref[...].astype(BF16), pw_ref[...])
    if final:
        fg_ref, xo_ref, yo_ref = rest
        yo_ref[...] = x2 * lax.rsqrt(jnp.mean(x2 * x2, axis=-1, keepdims=True) + EPS) * fg_ref[...]
    else:
        (xo_ref,) = rest
    xo_ref[...] = x2


def layer_tail(kind, mix_inputs, x, p, w_down, gate_w, ple_w, skip=None, final_g=None):
    M, D = x.shape
    tm = _pick(M, (512, 256))
    row = lambda i: (i, 0)
    fixed = lambda i: (0, 0)
    in_specs, args = [], []
    for a in mix_inputs:
        in_specs.append(pl.BlockSpec((tm, a.shape[1]), row))
        args.append(a)
    if kind == "mlstm":
        in_specs.append(pl.BlockSpec((1, skip.shape[-1]), fixed))
        args.append(skip.reshape(1, -1))
    in_specs += [pl.BlockSpec((tm, D), row), pl.BlockSpec((tm, p.shape[1]), row),
                 pl.BlockSpec(w_down.shape, fixed), pl.BlockSpec(gate_w.shape, fixed),
                 pl.BlockSpec(ple_w.shape, fixed)]
    args += [x, p, w_down, gate_w, ple_w]
    out_specs = [pl.BlockSpec((tm, D), row)]
    out_shape = [jax.ShapeDtypeStruct((M, D), F32)]
    if final_g is not None:
        in_specs.append(pl.BlockSpec((1, D), fixed))
        args.append(final_g.reshape(1, D))
        out_specs.append(pl.BlockSpec((tm, D), row))
        out_shape.append(jax.ShapeDtypeStruct((M, D), F32))
    return pl.pallas_call(
        functools.partial(_tail_body, kind=kind, final=final_g is not None),
        grid=(M // tm,),
        in_specs=in_specs, out_specs=out_specs, out_shape=out_shape,
        compiler_params=_params("parallel"),
        name="layer_tail_" + kind,
    )(*args)


def _sortable_key(s):
    s = jnp.where(s == 0.0, 0.0, s)
    bits = lax.bitcast_convert_type(s, jnp.int32)
    return jnp.where(bits < 0, bits ^ jnp.int32(0x7FFFFFFF), bits)


def _count(keys_ref, nvalid, pred):
    rows, cw = keys_ref.shape[1], keys_ref.shape[2]

    def body(kc, acc):
        hit = jnp.where(pred(keys_ref[kc], kc), 1.0, 0.0)
        part = hit[:, 0:LANES]
        for g in range(1, cw // LANES):
            part = part + hit[:, g * LANES:(g + 1) * LANES]
        return acc + part

    acc = lax.fori_loop(0, nvalid, body, jnp.zeros((rows, LANES), F32))
    return jnp.sum(acc, axis=1, keepdims=True)


def _select_topk(keys_ref, cidx_ref, nvalid, k_top, idx_bits):
    rows, cw = keys_ref.shape[1], keys_ref.shape[2]
    kf = float(k_top)
    imin = jnp.int32(INT_MIN)

    def bit_body(i, tau_u):
        cand_u = tau_u | jnp.left_shift(jnp.int32(1), 31 - i)
        cand = cand_u ^ imin
        cnt = _count(keys_ref, nvalid, lambda kk, kc: kk >= cand)
        return jnp.where(cnt >= kf, cand_u, tau_u)

    tau_u = lax.fori_loop(0, 32, bit_body, jnp.zeros((rows, 1), jnp.int32))
    tau = tau_u ^ imin
    n_gt = _count(keys_ref, nvalid, lambda kk, kc: kk > tau)
    n_ge = _count(keys_ref, nvalid, lambda kk, kc: kk >= tau)
    need = kf - n_gt
    short = tau_u == 0
    cidx_ref[...] = jnp.broadcast_to(jnp.where(short, -1, 2 ** 30), cidx_ref.shape)
    excess = jnp.max(jnp.where((n_ge > kf) & jnp.logical_not(short), 1.0, 0.0))

    @pl.when(excess > 0.5)
    def _():
        lane = lax.broadcasted_iota(jnp.int32, (1, cw), 1)

        def idx_body(i, cut):
            cand = cut | jnp.left_shift(jnp.int32(1), idx_bits - 1 - i)
            cnt = _count(keys_ref, nvalid,
                         lambda kk, kc: (kk == tau) & ((kc * cw + lane) < cand))
            return jnp.where(cnt < need, cand, cut)

        cut = lax.fori_loop(0, idx_bits, idx_body, jnp.zeros((rows, 1), jnp.int32))
        cidx_ref[...] = jnp.broadcast_to(jnp.where(short, -1, cut), cidx_ref.shape)

    return tau


def _selected(kk, colv, tau, cut):
    return (kk > tau) | ((kk == tau) & (colv <= cut))


def _n_causal_chunks(qb, tq, tk):
    return lax.div(qb * tq + (tq - 1), jnp.int32(tk)) + 1


def _idx_prompt_body(qi_ref, wi_ref, ki_ref, bias_ref, keys_ref, cidx_ref, *,
                     nh, dh, tq, tk, nk, k_top, idx_bits):
    per = keys_ref.shape[2] // tk
    gw = per * tk
    qb = pl.program_id(1)
    nvalid = _n_causal_chunks(qb, tq, gw)
    rowpos = qb * tq + lax.broadcasted_iota(jnp.int32, (tq, 1), 0)
    lane = lax.broadcasted_iota(jnp.int32, (1, tk), 1)
    wi = wi_ref[0]

    def score_group(gc, carry):
        for part in range(per):
            kt = ki_ref[0, gc * per + part]
            s = jnp.zeros((tq, tk), F32)
            for h in range(nh):
                rel = jnp.maximum(_dot_nt(qi_ref[0, :, h * dh:(h + 1) * dh], kt), 0.0)
                s = s + rel * wi[:, h:h + 1]
            colv = gc * gw + part * tk + lane
            keys_ref[gc, :, part * tk:(part + 1) * tk] = jnp.where(
                colv <= rowpos, _sortable_key(s), jnp.int32(INT_MIN))
        return carry

    lax.fori_loop(0, nvalid, score_group, 0)
    tau = _select_topk(keys_ref, cidx_ref, nvalid, k_top, idx_bits)
    cut = cidx_ref[:, 0:1]

    def write_group(gc, carry):
        for part in range(per):
            kk = keys_ref[gc, :, part * tk:(part + 1) * tk]
            sel = _selected(kk, gc * gw + part * tk + lane, tau, cut)
            bias_ref[0, 0, gc * per + part] = jnp.where(sel, 0.0, NEG).astype(bias_ref.dtype)
        return carry

    lax.fori_loop(0, nvalid, write_group, 0)

    def fill_chunk(kc, carry):
        bias_ref[0, 0, kc] = jnp.full((tq, tk), NEG, bias_ref.dtype)
        return carry

    lax.fori_loop(nvalid * per, nk, fill_chunk, 0)


def indexer_bias_prompt(qi, wi, ki, k_top):
    B, T, _ = qi.shape
    dh = ki.shape[-1]
    nh = qi.shape[-1] // dh
    tq, tk = min(ATT_TQ, T), min(ATT_TK, T)
    nq, nk = T // tq, T // tk
    per = 2 if nk % 2 == 0 else 1
    idx_bits = int(T).bit_length() + 1
    return pl.pallas_call(
        functools.partial(_idx_prompt_body, nh=nh, dh=dh, tq=tq, tk=tk, nk=nk, k_top=k_top,
                          idx_bits=idx_bits),
        grid=(B, nq),
        in_specs=[pl.BlockSpec((1, tq, nh * dh), lambda b, i: (b, i, 0)),
                  pl.BlockSpec((1, tq, LANES), lambda b, i: (b, i, 0)),
                  pl.BlockSpec((1, nk, tk, dh), lambda b, i: (b, 0, 0, 0))],
        out_specs=pl.BlockSpec((1, 1, nk, tq, tk), lambda b, i: (b, i, 0, 0, 0)),
        out_shape=jax.ShapeDtypeStruct((B, nq, nk, tq, tk), BF16),
        scratch_shapes=[pltpu.VMEM((nk // per, tq, per * tk), jnp.int32),
                        pltpu.VMEM((tq, LANES), jnp.int32)],
        compiler_params=_params("parallel", "arbitrary"),
        name="indexer_bias_prompt",
    )(qi, wi, ki.reshape(B, nk, tk, dh))


def _attn_prompt_body(q_ref, k_ref, v_ref, bias_ref, o_ref, m_s, acc_s, *, nh, dh, tq, tk):
    qb, kb = pl.program_id(1), pl.program_id(2)
    nvalid = _n_causal_chunks(qb, tq, tk)

    @pl.when(kb == 0)
    def _():
        m_s[...] = jnp.full(m_s.shape, NEG, F32)
        acc_s[...] = jnp.zeros(acc_s.shape, F32)

    @pl.when(kb < nvalid)
    def _():
        bias = bias_ref[0, 0, 0].astype(F32)
        for h in range(nh):
            s = _dot_nt(q_ref[0, h], k_ref[0, h]) + bias
            m_prev = m_s[h]
            m_new = jnp.maximum(m_prev, jnp.max(s, axis=1, keepdims=True))
            alpha = jnp.exp2(m_prev - m_new)
            p = jnp.exp2(s - jnp.concatenate([m_new] * (tk // LANES), axis=1))
            acc_s[h] = alpha * acc_s[h] + _dot(p.astype(BF16), v_ref[0, h])
            m_s[h] = m_new

    @pl.when(kb == pl.num_programs(2) - 1)
    def _():
        for h in range(nh):
            acc = acc_s[h]
            o_ref[0, :, h * dh:(h + 1) * dh] = (acc[:, :dh] / acc[:, dh:dh + 1]).astype(o_ref.dtype)


def attention_prompt(q, k, v, bias):
    B, nh, T, dh = q.shape
    assert tuple(v.shape) == (B, nh, T, LANES) and dh < LANES
    _, nq, nk, tq, tk = bias.shape

    def kv_map(b, i, j):
        return (b, 0, jnp.minimum(j, _n_causal_chunks(i, tq, tk) - 1), 0)

    def bias_map(b, i, j):
        return (b, i, jnp.minimum(j, _n_causal_chunks(i, tq, tk) - 1), 0, 0)

    return pl.pallas_call(
        functools.partial(_attn_prompt_body, nh=nh, dh=dh, tq=tq, tk=tk),
        grid=(B, nq, nk),
        in_specs=[pl.BlockSpec((1, nh, tq, dh), lambda b, i, j: (b, 0, i, 0)),
                  pl.BlockSpec((1, nh, tk, dh), kv_map),
                  pl.BlockSpec((1, nh, tk, LANES), kv_map),
                  pl.BlockSpec((1, 1, 1, tq, tk), bias_map)],
        out_specs=pl.BlockSpec((1, tq, nh * dh), lambda b, i, j: (b, i, 0)),
        out_shape=jax.ShapeDtypeStruct((B, T, nh * dh), BF16),
        scratch_shapes=[pltpu.VMEM((nh, tq, LANES), F32),
                        pltpu.VMEM((nh, tq, LANES), F32)],
        compiler_params=_params("parallel", "parallel", "arbitrary"),
        name="attention_prompt",
    )(q, k, v, bias)


def _idx_sample_body(pt_ref, qi_ref, wi_ref, knew_ref, *rest, nh, ts, npg, nsteps, k_top,
                     idx_bits):
    page_refs = rest[:npg]
    bias_ref, keys_ref, cidx_ref = rest[npg:]
    s = pl.program_id(1)
    cw = keys_ref.shape[2]
    wi = wi_ref[0]
    qi = qi_ref[0]

    def scores(kt):
        full = jnp.maximum(_dot(qi, kt), 0.0)
        out = jnp.zeros((ts, cw), F32)
        for h in range(nh):
            out = out + full[h * ts:(h + 1) * ts, :] * wi[:, h:h + 1]
        return out

    kt = jnp.concatenate([r[...] for r in page_refs], axis=1).astype(BF16)
    keys_ref[s] = _sortable_key(scores(kt))

    @pl.when(s == nsteps - 1)
    def _():
        lane = lax.broadcasted_iota(jnp.int32, (1, cw), 1)
        past = nsteps * cw
        rowpos = past + lax.broadcasted_iota(jnp.int32, (ts, 1), 0)
        colv_new = past + lane
        keys_ref[nsteps] = jnp.where(colv_new <= rowpos, _sortable_key(scores(knew_ref[0])),
                                     jnp.int32(INT_MIN))
        tau = _select_topk(keys_ref, cidx_ref, nsteps + 1, k_top, idx_bits)
        cut = cidx_ref[:, 0:1]
        for kc in range(nsteps + 1):
            sel = _selected(keys_ref[kc], kc * cw + lane, tau, cut)
            bias_ref[0, kc] = jnp.where(sel, 0.0, NEG).astype(bias_ref.dtype)


def _page_specs(n_pages_per_step, block, layer, first_step):
    specs = []
    for i in range(n_pages_per_step):
        def imap(b, s, pt, i=i):
            step = jnp.maximum(s - first_step, 0)
            return (layer, pt[b, step * n_pages_per_step + i]) + (0,) * len(block)
        specs.append(pl.BlockSpec((None, None) + tuple(block), imap))
    return specs


def indexer_bias_sample(qi_hq, wi, ki_new_t, cache_kidx_t, layer, page_table, k_top):
    B, n_pages = page_table.shape
    _, _, dh, page = cache_kidx_t.shape
    ts = wi.shape[1]
    nh = qi_hq.shape[1] // ts
    npg = min(PAGES_PER_STEP, n_pages)
    nsteps = n_pages // npg
    cw = npg * page
    idx_bits = int(n_pages * page + cw).bit_length() + 1
    grid_spec = pltpu.PrefetchScalarGridSpec(
        num_scalar_prefetch=1,
        grid=(B, nsteps),
        in_specs=[pl.BlockSpec((1, nh * ts, dh), lambda b, s, pt: (b, 0, 0)),
                  pl.BlockSpec((1, ts, LANES), lambda b, s, pt: (b, 0, 0)),
                  pl.BlockSpec((1, dh, cw), lambda b, s, pt: (b, 0, 0))]
        + _page_specs(npg, (dh, page), layer, 0),
        out_specs=pl.BlockSpec((1, nsteps + 1, ts, cw), lambda b, s, pt: (b, 0, 0, 0)),
        scratch_shapes=[pltpu.VMEM((nsteps + 1, ts, cw), jnp.int32),
                        pltpu.VMEM((ts, LANES), jnp.int32)],
    )
    return pl.pallas_call(
        functools.partial(_idx_sample_body, nh=nh, ts=ts, npg=npg, nsteps=nsteps, k_top=k_top,
                          idx_bits=idx_bits),
        grid_spec=grid_spec,
        out_shape=jax.ShapeDtypeStruct((B, nsteps + 1, ts, cw), F32),
        compiler_params=_params("parallel", "arbitrary"),
        name="indexer_bias_sample",
    )(page_table, qi_hq, wi, ki_new_t, *([cache_kidx_t] * npg))


def _attn_sample_body(pt_ref, qbd_ref, knew_ref, vnew_ref, bias_ref, *rest, nh, ts, npg, dh):
    kpages, vpages = rest[:npg], rest[npg:2 * npg]
    o_ref, m_s, l_s, acc_s = rest[2 * npg:]
    s = pl.program_id(1)
    rows = nh * ts

    @pl.when(s == 0)
    def _():
        m_s[...] = jnp.full(m_s.shape, NEG, F32)
        l_s[...] = jnp.zeros(l_s.shape, F32)
        acc_s[...] = jnp.zeros(acc_s.shape, F32)

    def update(kt, vt):
        n = kt.shape[1]
        bias = bias_ref[0, 0][:, :n]
        logits = _dot(qbd_ref[0], kt) + jnp.concatenate([bias] * nh, axis=0)
        m_prev = m_s[...]
        m_new = jnp.maximum(m_prev, jnp.max(logits, axis=1, keepdims=True))
        alpha = jnp.exp2(m_prev - m_new)
        p = jnp.exp2(logits - m_new)
        l_s[...] = alpha * l_s[...] + jnp.sum(p, axis=1, keepdims=True)
        acc_s[...] = alpha * acc_s[...] + _dot_nt(p.astype(BF16), vt)
        m_s[...] = m_new

    @pl.when(s == 0)
    def _():
        update(knew_ref[0], vnew_ref[0])

    def gather_pages(page_refs):
        return jnp.concatenate([r[...].reshape(nh * dh, r.shape[2]) for r in page_refs],
                               axis=1).astype(BF16)

    @pl.when(s > 0)
    def _():
        update(gather_pages(kpages), gather_pages(vpages))

    @pl.when(s == pl.num_programs(1) - 1)
    def _():
        o = acc_s[...] / l_s[...]
        lane = lax.broadcasted_iota(jnp.int32, (1, nh * dh), 1)
        res = jnp.zeros((ts, nh * dh), F32)
        for h in range(nh):
            own = (lane >= h * dh) & (lane < (h + 1) * dh)
            res = res + jnp.where(own, o[h * ts:(h + 1) * ts, :], 0.0)
        o_ref[0] = res.astype(o_ref.dtype)


def attention_sample(qbd, k_new_t, v_new_t, bias, cache_k_t, cache_v_t, layer, page_table, ts):
    B, n_pages = page_table.shape
    _, _, nh, dh, page = cache_k_t.shape
    width = nh * dh
    n_new = k_new_t.shape[2]
    nsteps1, cw = bias.shape[1], bias.shape[3]
    npg = cw // page
    rows = nh * ts

    def bias_map(b, s, pt):
        return (b, jnp.where(s == 0, nsteps1 - 1, s - 1), 0, 0)

    grid_spec = pltpu.PrefetchScalarGridSpec(
        num_scalar_prefetch=1,
        grid=(B, nsteps1),
        in_specs=[pl.BlockSpec((1, rows, width), lambda b, s, pt: (b, 0, 0)),
                  pl.BlockSpec((1, width, n_new), lambda b, s, pt: (b, 0, 0)),
                  pl.BlockSpec((1, width, n_new), lambda b, s, pt: (b, 0, 0)),
                  pl.BlockSpec((1, 1, ts, cw), bias_map)]
        + _page_specs(npg, (nh, dh, page), layer, 1) + _page_specs(npg, (nh, dh, page), layer, 1),
        out_specs=pl.BlockSpec((1, ts, width), lambda b, s, pt: (b, 0, 0)),
        scratch_shapes=[pltpu.VMEM((rows, 1), F32),
                        pltpu.VMEM((rows, 1), F32),
                        pltpu.VMEM((rows, width), F32)],
    )
    return pl.pallas_call(
        functools.partial(_attn_sample_body, nh=nh, ts=ts, npg=npg, dh=dh),
        grid_spec=grid_spec,
        out_shape=jax.ShapeDtypeStruct((B, ts, width), BF16),
        compiler_params=_params("parallel", "arbitrary"),
        name="attention_sample",
    )(page_table, qbd, k_new_t, v_new_t, bias, *([cache_k_t] * npg), *([cache_v_t] * npg))


def _pad_time(a, t_pad, value=0.0):
    return jnp.pad(a, ((0, 0), (0, t_pad - a.shape[1]), (0, 0)), constant_values=value)


def _mlstm_layer(x, g, B, T, state, w, j):
    c0, n0, m0, conv0 = state
    nh = c0.shape[1]
    E = w["a_conv_w"].shape[-1]
    w_up = w["a_w_up"][j]
    xb, = norm_matmul(x, g, w_up[:, :E], (BF16,))
    z, = norm_matmul(x, g, w_up[:, E:], (BF16,))
    xc = conv_silu(xb.reshape(B, T, E), conv0, w["a_conv_w_f32"][j], w["a_conv_b"][j]).reshape(B * T, E)
    q = matmul(xc, w["a_w_q"][j], BF16)
    k = matmul(xc, w["a_w_k"][j], BF16)
    v = matmul(xb, w["a_w_v"][j], BF16)
    o = matmul(xb, w["a_w_o"][j], BF16, bias=w["a_b_o"][j], act="sigmoid")
    gates = matmul(xc, w["a_w_if_pad"][j], F32, bias=w["a_b_if_pad"][j], act="gates", n_li=nh)
    gcol = gates.reshape(B, T, LANES)
    q3, k3, v3 = q.reshape(B, T, -1), k.reshape(B, T, -1), v.reshape(B, T, -1)
    tp = T if T % LANES == 0 else ((T + LANES - 1) // LANES) * LANES
    if tp != T:
        q3, k3, v3 = _pad_time(q3, tp), _pad_time(k3, tp), _pad_time(v3, tp)
        pad_row = jnp.where(jnp.arange(LANES) < nh, NEG, 0.0).astype(F32)
        gcol = jnp.concatenate([gcol, jnp.broadcast_to(pad_row, (B, tp - T, LANES))], axis=1)
    grow = jnp.swapaxes(gcol[:, :, :2 * nh], 1, 2)
    hn, c_new, n_new, m_new = mlstm_recurrence(q3, k3, v3, gcol, grow, c0, n0, m0, w["a_norm_g"][j])
    hn = hn[:, :T].reshape(B * T, -1)
    assert T >= CONV_W - 1
    conv_new = xb.reshape(B, T, E)[:, T - (CONV_W - 1):].astype(F32)
    return (hn, o, xc, z), (c_new, n_new, m_new, conv_new)


def _rope_tables(pos):
    half = 32
    inv = ROPE_THETA ** (-np.arange(half, dtype=np.float64) / half)
    ang = np.asarray(pos, np.float64)[:, None] * inv[None, :]
    cos = np.tile(np.cos(ang), (1, 4))
    sin = np.tile(np.concatenate([-np.sin(ang), np.sin(ang)], axis=1), (1, 2))
    return jnp.asarray(cos, F32), jnp.asarray(sin, F32)


def _dsa_project(x, g, B, T, pos, w, j, nh, dh, nhi, di, head_major=False):
    M = B * T
    cos, sin = _rope_tables(pos)
    tm = _pick(T, ROW_TILES) if T % ROW_TILES[-1] == 0 else _pick(M, ROW_TILES)
    if T % tm == 0:
        tabs, tab_blocks = (cos, sin), T // tm
    else:
        tabs, tab_blocks = (jnp.tile(cos, (M // T, 1)), jnp.tile(sin, (M // T, 1))), M // tm
    rope = dict(rope_tabs=tabs, tab_blocks=tab_blocks, row_tile=tm)
    hm = (lambda *widths: (B, T, dh, widths)) if head_major else (lambda *widths: None)
    q, = norm_matmul(x, g, w["b_wq"][j], (BF16,), scale=dh ** -0.5 * math.log2(math.e),
                     head_major=hm(dh), **rope)
    k32, k16 = norm_matmul(x, g, w["b_wk"][j], (F32, BF16), head_major=hm(0, dh), **rope)
    v32, v16 = norm_matmul(x, g, w["b_wv"][j], (F32, BF16), head_major=hm(0, LANES), row_tile=tm)
    gate, = norm_matmul(x, g, w["b_wg"][j], (BF16,))
    qi, = norm_matmul(x, g, w["b_wqi"][j], (BF16,), **rope)
    ki32, = norm_matmul(x, g, w["b_wki_pad"][j], (F32,), **rope)
    wi, = norm_matmul(x, g, w["b_wwi_pad"][j], (F32,), scale=nhi ** -0.5 * di ** -0.5)
    return q, k32, k16, v32, v16, gate, qi, ki32[:, :di], wi


def _dsa_layer_prompt(x, g, B, T, w, j, dims):
    nh, dh, nhi, di = dims
    q, k32, k16, v32, v16, gate, qi, ki32, wi = _dsa_project(
        x, g, B, T, np.arange(T), w, j, *dims, head_major=True)
    k_top = min(TOPK_MAX, T // 4)
    bias = indexer_bias_prompt(qi.reshape(B, T, nhi * di), wi.reshape(B, T, LANES),
                               ki32.astype(BF16).reshape(B, T, di), k_top)
    attn = attention_prompt(q, k16, v16, bias).reshape(B * T, nh * dh)
    new =(k32.reshape(B, T, nh, dh), v32.reshape(B, T, nh, dh), ki32.reshape(B, T, di))
    return (attn, gate), new


def _dsa_layer_sample(x, g, B, T, w, j, dims, ctx):
    nh, dh, nhi, di = dims
    cache_k, cache_v, cache_kidx, page_table = ctx
    n_pages = page_table.shape[1]
    page = cache_kidx.shape[2]
    past = n_pages * page
    q, k32, k16, v32, v16, gate, qi, ki32, wi = _dsa_project(
        x, g, B, T, past + np.arange(T), w, j, *dims)
    k_top = min(TOPK_MAX, (past + T) // 4)
    cw = min(PAGES_PER_STEP, n_pages) * page
    qi_hq = jnp.transpose(qi.reshape(B, T, nhi, di), (0, 2, 1, 3)).reshape(B, nhi * T, di)
    token_last = lambda a, n: jnp.swapaxes(_pad_time(a, n), 1, 2)
    ki_new_t = token_last(ki32.astype(BF16).reshape(B, T, di), cw)
    bias = indexer_bias_sample(qi_hq, wi.reshape(B, T, LANES), ki_new_t,
                               jnp.swapaxes(cache_kidx, 2, 3), j, page_table, k_top)
    q3 = q.reshape(B, T, nh * dh)
    own = (jnp.arange(nh * T)[:, None] // T) == (jnp.arange(nh * dh)[None, :] // dh)
    qbd = jnp.where(own[None], jnp.tile(q3, (1, nh, 1)), 0).astype(BF16)
    width = nh * dh
    n_new = ((T + LANES - 1) // LANES) * LANES
    attn = attention_sample(qbd, token_last(k16.reshape(B, T, width), n_new),
                            token_last(v16.reshape(B, T, width), n_new), bias,
                            jnp.transpose(cache_k, (0, 1, 3, 4, 2)),
                            jnp.transpose(cache_v, (0, 1, 3, 4, 2)), j, page_table, T)
    new = (k32.reshape(B, T, nh, dh), v32.reshape(B, T, nh, dh), ki32.reshape(B, T, di))
    return (attn.reshape(B * T, width), gate), new


def _pool_layer(x, g, B, T, buf, pos0, w, j):
    E = w["c_scale"].shape[-1]
    w_up = w["c_w_up"][j]
    xb, = norm_matmul(x, g, w_up[:, :E], (BF16,))
    z, = norm_matmul(x, g, w_up[:, E:], (BF16,))
    xb3 = xb.reshape(B, T, E)
    y = pool_mix(xb3, buf, w["c_w_grp"][j], w["c_scale"][j], pos0).reshape(B * T, E)
    nbuf = buf.shape[1]
    if T >= nbuf:
        new_buf = xb3[:, T - nbuf:].astype(F32)
    else:
        new_buf = jnp.concatenate([buf[:, T:].astype(F32), xb3.astype(F32)], axis=1)
    return (y, z), new_buf


def _run_group(x3, p4, pos0, mlstm_state, dsa_ctx, pool_state, w, dims):
    B, T, D = x3.shape
    depth = p4.shape[0]
    x = x3.reshape(B * T, D)
    new_a, new_b, new_c = [], [], []
    y = None
    for i in range(depth):
        kind, j = i % 3, i // 3
        g = w["norm_g"][i]
        final_g = w["final_g"] if i == depth - 1 else None
        p = p4[i].reshape(B * T, -1)
        if kind == 0:
            state = tuple(s[j] for s in mlstm_state)
            mix, st = _mlstm_layer(x, g, B, T, state, w, j)
            new_a.append(st)
            out = layer_tail("mlstm", mix, x, p, w["a_w_down"][j], w["ple_gate_w"][i], w["ple_w"][i],
                             skip=w["a_skip"][j], final_g=final_g)
        elif kind == 1:
            if dsa_ctx is None:
                mix, st = _dsa_layer_prompt(x, g, B, T, w, j, dims)
            else:
                mix, st = _dsa_layer_sample(x, g, B, T, w, j, dims, dsa_ctx)
            new_b.append(st)
            out = layer_tail("gated", mix, x, p, w["b_w_out"][j], w["ple_gate_w"][i], w["ple_w"][i],
                             final_g=final_g)
        else:
            mix, st = _pool_layer(x, g, B, T, pool_state[j], pos0, w, j)
            new_c.append(st)
            out = layer_tail("gated", mix, x, p, w["c_w_down"][j], w["ple_gate_w"][i], w["ple_w"][i],
                             final_g=final_g)
        x = out[0]
        if final_g is not None:
            y = out[1]
    a_states = tuple(jnp.stack([s[r] for s in new_a]) for r in range(4))
    b_states = tuple(jnp.stack([s[r] for s in new_b]) for r in range(3))
    c_state = jnp.stack(new_c)
    return y.reshape(B, T, D), a_states, b_states, c_state


def kernel(x_prompt, x_sample, state_mlstm_C, state_mlstm_n, state_mlstm_m, state_mlstm_conv, state_pool,
           cache_k, cache_v, cache_kidx, page_table, p_prompt, p_sample,
           norm_g, final_g, ple_w, ple_gate_w,
           a_w_up, a_conv_w, a_conv_b, a_w_q, a_w_k, a_w_v, a_w_if, a_b_if, a_w_o, a_b_o, a_norm_g,
           a_skip, a_w_down, b_w_in, b_w_out, c_w_up, c_w_grp, c_scale, c_w_down):
    D = x_prompt.shape[-1]
    nh, dh = cache_k.shape[3], cache_k.shape[4]
    di = cache_kidx.shape[-1]
    aw = nh * dh
    nhi = (b_w_in.shape[-1] - 4 * aw - di) // (di + 1)
    dims = (nh, dh, nhi, di)
    nha = state_mlstm_C.shape[2]
    bf = lambda a: a.astype(BF16)

    def pad_cols(a, n):
        return jnp.pad(a, [(0, 0)] * (a.ndim - 1) + [(0, n - a.shape[-1])])

    o1, o2, o3, o4, o5, o6 = np.cumsum([aw, aw, aw, aw, nhi * di, di])
    w = dict(
        norm_g=norm_g, final_g=final_g, ple_w=bf(ple_w), ple_gate_w=bf(ple_gate_w),
        a_w_up=bf(a_w_up), a_conv_w_f32=a_conv_w, a_conv_w=a_conv_w, a_conv_b=a_conv_b,
        a_w_q=bf(a_w_q), a_w_k=bf(a_w_k), a_w_v=bf(a_w_v), a_w_o=bf(a_w_o), a_b_o=a_b_o,
        a_w_if_pad=bf(pad_cols(a_w_if, LANES)), a_b_if_pad=pad_cols(a_b_if, LANES),
        a_norm_g=a_norm_g, a_skip=a_skip, a_w_down=bf(a_w_down),
        b_wq=bf(b_w_in[..., :o1]), b_wk=bf(b_w_in[..., o1:o2]), b_wv=bf(b_w_in[..., o2:o3]),
        b_wg=bf(b_w_in[..., o3:o4]), b_wqi=bf(b_w_in[..., o4:o5]),
        b_wki_pad=bf(pad_cols(b_w_in[..., o5:o6], LANES)),
        b_wwi_pad=bf(pad_cols(b_w_in[..., o6:], LANES)),
        b_w_out=bf(b_w_out), c_w_up=bf(c_w_up), c_w_grp=bf(c_w_grp), c_scale=c_scale,
        c_w_down=bf(c_w_down),
    )

    Bp = x_prompt.shape[0]
    na, nc = state_mlstm_C.shape[0], state_pool.shape[0]
    zeros_a = (jnp.zeros((na, Bp) + state_mlstm_C.shape[2:], F32),
               jnp.zeros((na, Bp) + state_mlstm_n.shape[2:], F32),
               jnp.zeros((na, Bp) + state_mlstm_m.shape[2:], F32),
               jnp.zeros((na, Bp) + state_mlstm_conv.shape[2:], F32))
    zeros_c = jnp.zeros((nc, Bp) + state_pool.shape[2:], F32)
    y_p, a_p, b_p, c_p = _run_group(x_prompt, p_prompt, 0, zeros_a, None, zeros_c, w, dims)

    past = page_table.shape[1] * cache_k.shape[2]
    y_s, a_s, b_s, c_s = _run_group(
        x_sample, p_sample, past,
        (state_mlstm_C, state_mlstm_n, state_mlstm_m, state_mlstm_conv),
        (cache_k, cache_v, cache_kidx, page_table), state_pool, w, dims)
    return (y_p, y_s, *a_p, *b_p, c_p, *a_s, *b_s, c_s)
```

```python
import functools
import math

import numpy as np
import jax
import jax.numpy as jnp
from jax import lax
from jax.experimental import pallas as pl
from jax.experimental.pallas import tpu as pltpu

EPS = 1e-6
ROPE_THETA = 10000.0
TOPK_MAX = 256
POOL_WINDOWS = (2, 4, 8, 16)
CONV_W = 4
MLSTM_CHUNK = 256
NEG = -1e30
LANES = 128
SUBLANES = 8
VMEM_LIMIT_BYTES = 56 * 1024 * 1024
ATT_TQ, ATT_TK = 512, 512
IDX_TQ = 256
PAGES_PER_STEP = 8
ROW_TILES = (1024, 512, 256)
COL_TILES = (1024, 512, 256, 128)

F32 = jnp.float32
BF16 = jnp.bfloat16
INT_MIN = -2 ** 31


def _params(*sem):
    return pltpu.CompilerParams(dimension_semantics=sem, vmem_limit_bytes=VMEM_LIMIT_BYTES)


def _sigmoid(x):
    return 1.0 / (1.0 + jnp.exp(-x))


def _silu(x):
    return x * _sigmoid(x)


def _log_sigmoid(x):
    return jnp.minimum(x, 0.0) - jnp.log(1.0 + jnp.exp(-jnp.abs(x)))


def _dot(a, b):
    return jnp.dot(a, b, preferred_element_type=F32)


def _dot_nt(a, b):
    return lax.dot_general(a, b, (((1,), (1,)), ((), ())), preferred_element_type=F32)


def _dot_tn(a, b):
    return lax.dot_general(a, b, (((0,), (0,)), ((), ())), preferred_element_type=F32)


def _pick(n, cands):
    for c in cands:
        if n % c == 0:
            return c
    return n


def _rope_tile(a, cos, sin_signed):
    lane = lax.broadcasted_iota(jnp.int32, (1, LANES), 1)
    first_half = (lane % 64) < 32
    pieces = []
    for gi in range(a.shape[1] // LANES):
        ag = a[:, gi * LANES:(gi + 1) * LANES]
        ahead = pltpu.roll(ag, LANES - 32, 1)
        behind = pltpu.roll(ag, 32, 1)
        rot = jnp.where(first_half, ahead, behind)
        pieces.append(ag * cos + rot * sin_signed)
    return pieces[0] if len(pieces) == 1 else jnp.concatenate(pieces, axis=1)


def _norm_matmul_body(*refs, rope, n_out, scale, head_dim):
    if rope:
        x_ref, g_ref, w_ref, cos_ref, sin_ref = refs[:5]
        rest = refs[5:]
    else:
        x_ref, g_ref, w_ref = refs[:3]
        rest = refs[3:]
    outs, xn_ref = rest[:n_out], rest[n_out]

    @pl.when(pl.program_id(1) == 0)
    def _():
        xf = x_ref[...]
        y = xf * lax.rsqrt(jnp.mean(xf * xf, axis=-1, keepdims=True) + EPS) * g_ref[...]
        xn_ref[...] = y.astype(BF16)

    acc = _dot(xn_ref[...], w_ref[...])
    if rope:
        acc = _rope_tile(acc, cos_ref[...], sin_ref[...])
    if scale != 1.0:
        acc = acc * scale
    for o in outs:
        if len(o.shape) == 4:
            npad = o.shape[3] - head_dim
            if npad:
                lane = lax.broadcasted_iota(jnp.int32, (acc.shape[0], npad), 1)
                pad = jnp.where(lane == 0, 1.0, 0.0)
            for hh in range(o.shape[1]):
                piece = acc[:, hh * head_dim:(hh + 1) * head_dim]
                if npad:
                    piece = jnp.concatenate([piece, pad], axis=1)
                o[0, hh] = piece.astype(o.dtype)
        else:
            o[...] = acc.astype(o.dtype)


def norm_matmul(x, g, w, out_dtypes, rope_tabs=None, tab_blocks=1, scale=1.0, head_major=None,
                row_tile=None):
    M, D = x.shape
    N = w.shape[1]
    tm = row_tile if row_tile is not None else _pick(M, ROW_TILES)
    tn = _pick(N, COL_TILES)
    out_specs = [pl.BlockSpec((tm, tn), lambda i, j: (i, j)) for _ in out_dtypes]
    out_shape = [jax.ShapeDtypeStruct((M, N), dt) for dt in out_dtypes]
    head_dim = 0
    if head_major is not None:
        hb, ht, head_dim, flags = head_major
        assert ht % tm == 0
        tpb = ht // tm
        for idx, width in enumerate(flags):
            if width:
                out_specs[idx] = pl.BlockSpec((1, tn // head_dim, tm, width),
                                              lambda i, j: (i // tpb, j, i % tpb, 0))
                out_shape[idx] = jax.ShapeDtypeStruct((hb, N // head_dim, ht, width), out_dtypes[idx])
    in_specs = [pl.BlockSpec((tm, D), lambda i, j: (i, 0)),
                pl.BlockSpec((1, D), lambda i, j: (0, 0)),
                pl.BlockSpec((D, tn), lambda i, j: (0, j))]
    args = [x, g.reshape(1, D), w]
    if rope_tabs is not None:
        for t in rope_tabs:
            in_specs.append(pl.BlockSpec((tm, LANES), lambda i, j: (i % tab_blocks, 0)))
            args.append(t)
    return pl.pallas_call(
        functools.partial(_norm_matmul_body, rope=rope_tabs is not None, n_out=len(out_dtypes),
                          scale=scale, head_dim=head_dim),
        grid=(M // tm, N // tn),
        in_specs=in_specs,
        out_specs=out_specs,
        out_shape=out_shape,
        scratch_shapes=[pltpu.VMEM((tm, D), BF16)],
        compiler_params=_params("parallel", "arbitrary"),
        name="norm_matmul",
    )(*args)


def _matmul_body(*refs, has_bias, act):
    if has_bias:
        x_ref, w_ref, b_ref, o_ref = refs
    else:
        x_ref, w_ref, o_ref = refs
    acc = _dot(x_ref[...], w_ref[...])
    if has_bias:
        acc = acc + b_ref[...]
    if act == "sigmoid":
        acc = _sigmoid(acc)
    o_ref[...] = acc.astype(o_ref.dtype)


def matmul(x, w, out_dtype, bias=None, act=None):
    M, K = x.shape
    N = w.shape[1]
    tm = _pick(M, ROW_TILES)
    tn = _pick(N, COL_TILES)
    in_specs = [pl.BlockSpec((tm, K), lambda i, j: (i, 0)),
                pl.BlockSpec((K, tn), lambda i, j: (0, j))]
    args = [x, w]
    if bias is not None:
        in_specs.append(pl.BlockSpec((1, tn), lambda i, j: (0, j)))
        args.append(bias.reshape(1, N).astype(F32))
    return pl.pallas_call(
        functools.partial(_matmul_body, has_bias=bias is not None, act=act),
        grid=(M // tm, N // tn),
        in_specs=in_specs,
        out_specs=pl.BlockSpec((tm, tn), lambda i, j: (i, j)),
        out_shape=jax.ShapeDtypeStruct((M, N), out_dtype),
        compiler_params=_params("parallel", "parallel"),
        name="matmul",
    )(*args)


CONV_HALO = 8
POOL_HALO = 16


def _conv_body(xb_ref, st_ref, w_ref, b_ref, gwh_ref, gwl_ref, gb_ref, xc_ref, gate_ref, ext_ref,
               *, tb, n_li):
    c = pl.program_id(1)
    nst = CONV_W - 1
    lo = CONV_HALO - nst

    @pl.when(c == 0)
    def _():
        ext_ref[:, lo:CONV_HALO, :] = st_ref[...]

    @pl.when(c > 0)
    def _():
        ext_ref[:, lo:CONV_HALO, :] = ext_ref[:, lo + tb:CONV_HALO + tb, :]

    ext_ref[:, CONV_HALO:CONV_HALO + tb, :] = xb_ref[...].astype(F32)
    acc = b_ref[...][None] + ext_ref[:, lo:lo + tb, :] * w_ref[0:1, :][None]
    for j in range(1, CONV_W):
        acc = acc + ext_ref[:, lo + j:lo + j + tb, :] * w_ref[j:j + 1, :][None]
    xc = _silu(acc)
    xc_ref[...] = xc.astype(xc_ref.dtype)
    x2 = xc.reshape(xc.shape[0] * tb, xc.shape[2])
    hi = x2.astype(BF16)
    lo = (x2 - hi.astype(F32)).astype(BF16)
    g = _dot(hi, gwh_ref[...]) + (_dot(hi, gwl_ref[...]) + _dot(lo, gwh_ref[...])) + gb_ref[...]
    col = lax.broadcasted_iota(jnp.int32, g.shape, 1)
    g = jnp.where(col < n_li, g, _log_sigmoid(g))
    gate_ref[...] = g.reshape(gate_ref.shape)


def conv_silu_gates(xb, state, w, b, gate_w, gate_b, n_li):
    B, T, E = xb.shape
    bb, tb = (1, 256) if T % 256 == 0 else (B, T)
    gw_hi = gate_w.astype(BF16)
    gw_lo = (gate_w - gw_hi.astype(F32)).astype(BF16)
    fixed = lambda i, c: (0, 0)
    return pl.pallas_call(
        functools.partial(_conv_body, tb=tb, n_li=n_li),
        grid=(B // bb, T // tb),
        in_specs=[pl.BlockSpec((bb, tb, E), lambda i, c: (i, c, 0)),
                  pl.BlockSpec((bb, CONV_W - 1, E), lambda i, c: (i, 0, 0)),
                  pl.BlockSpec((CONV_W, E), fixed),
                  pl.BlockSpec((1, E), fixed),
                  pl.BlockSpec((E, LANES), fixed),
                  pl.BlockSpec((E, LANES), fixed),
                  pl.BlockSpec((1, LANES), fixed)],
        out_specs=[pl.BlockSpec((bb, tb, E), lambda i, c: (i, c, 0)),
                   pl.BlockSpec((bb, tb, LANES), lambda i, c: (i, c, 0))],
        out_shape=[jax.ShapeDtypeStruct((B, T, E), BF16),
                   jax.ShapeDtypeStruct((B, T, LANES), F32)],
        scratch_shapes=[pltpu.VMEM((bb, CONV_HALO + tb, E), F32)],
        compiler_params=_params("parallel", "arbitrary"),
        name="conv_silu_gates",
    )(xb, state.astype(F32), w, b.reshape(1, E), gw_hi, gw_lo, gate_b.reshape(1, LANES))


def _mlstm_body(q_ref, k_ref, v_ref, gc_ref, gr_ref, c0_ref, n0_ref, m0_ref, ng_ref,
                h_ref, c_out_ref, n_out_ref, m_out_ref, c_s, n_s, m_s, *, nh, dqk, dv, lc):
    c = pl.program_id(1)

    @pl.when(c == 0)
    def _():
        c_s[...] = c0_ref[0]
        n_s[...] = n0_ref[0]
        m_s[...] = m0_ref[0]

    row = lax.broadcasted_iota(jnp.int32, (lc, lc), 0)
    col = lax.broadcasted_iota(jnp.int32, (lc, lc), 1)
    causal = row >= col
    scale = dqk ** -0.5
    gc = gc_ref[0]
    gr = gr_ref[0]
    for h in range(nh):
        q = (q_ref[0, :, h * dqk:(h + 1) * dqk].astype(F32) * scale).astype(BF16)
        k = k_ref[0, :, h * dqk:(h + 1) * dqk]
        v = v_ref[0, :, h * dv:(h + 1) * dv]
        li_c, lf_c = gc[:, h:h + 1], gc[:, nh + h:nh + h + 1]
        li_r, lf_r = gr[h:h + 1, :], gr[nh + h:nh + h + 1, :]
        b_c = jnp.sum(jnp.where(causal, lf_r, 0.0), axis=1, keepdims=True)
        b_r = jnp.sum(jnp.where(row <= col, lf_c, 0.0), axis=0, keepdims=True)
        m_prev = m_s[h:h + 1, 0:1]
        dm = jnp.where(causal, b_c - b_r + li_r, NEG)
        m_inter = b_c + m_prev
        m_t = jnp.maximum(m_inter, jnp.max(dm, axis=1, keepdims=True))
        a = _dot_nt(q, k) * jnp.exp(dm - m_t)
        inter = jnp.exp(m_inter - m_t)
        c_prev = c_s[h]
        n_prev = n_s[h:h + 1, :]
        num = _dot(a.astype(BF16), v) + inter * _dot(q, c_prev.astype(BF16))
        qn = jnp.sum(q.astype(F32) * n_prev, axis=1, keepdims=True)
        den = jnp.sum(a, axis=1, keepdims=True) + inter * qn
        hh = num / jnp.maximum(jnp.abs(den), jnp.exp(-m_t))
        hn = hh * lax.rsqrt(jnp.mean(hh * hh, axis=-1, keepdims=True) + EPS) \
            * ng_ref[:, h * dv:(h + 1) * dv]
        h_ref[0, :, h * dv:(h + 1) * dv] = hn.astype(h_ref.dtype)
        b_last = b_c[lc - 1:lc, :]
        g_r = b_last - b_r + li_r
        g_c = b_last - b_c + li_c
        m_new = jnp.maximum(b_last + m_prev, jnp.max(g_r, axis=1, keepdims=True))
        w_r = jnp.exp(g_r - m_new)
        w_c = jnp.exp(g_c - m_new)
        decay = jnp.exp(b_last + m_prev - m_new)
        wv = (w_c * v.astype(F32)).astype(BF16)
        c_s[h] = decay * c_prev + _dot_tn(k, wv)
        wr8 = jnp.broadcast_to(w_r, (SUBLANES, lc)).astype(BF16)
        n_s[h:h + 1, :] = decay * n_prev + _dot(wr8, k)[0:1, :]
        m_s[h:h + 1, :] = jnp.broadcast_to(m_new, (1, LANES))

    @pl.when(c == pl.num_programs(1) - 1)
    def _():
        c_out_ref[0] = c_s[...]
        n_out_ref[0] = n_s[...]
        m_out_ref[0] = m_s[...]


def mlstm_recurrence(q, k, v, gcol, grow, c0, n0, m0, norm_g):
    B, T, _ = q.shape
    nh, dqk, dv = c0.shape[1], c0.shape[2], c0.shape[3]
    lc = min(MLSTM_CHUNK, T)
    m0b = jnp.broadcast_to(m0[:, :, None], (B, nh, LANES)).astype(F32)
    h, c_new, n_new, m_new = pl.pallas_call(
        functools.partial(_mlstm_body, nh=nh, dqk=dqk, dv=dv, lc=lc),
        grid=(B, T // lc),
        in_specs=[pl.BlockSpec((1, lc, nh * dqk), lambda b, c: (b, c, 0)),
                  pl.BlockSpec((1, lc, nh * dqk), lambda b, c: (b, c, 0)),
                  pl.BlockSpec((1, lc, nh * dv), lambda b, c: (b, c, 0)),
                  pl.BlockSpec((1, lc, LANES), lambda b, c: (b, c, 0)),
                  pl.BlockSpec((1, 2 * nh, lc), lambda b, c: (b, 0, c)),
                  pl.BlockSpec((1, nh, dqk, dv), lambda b, c: (b, 0, 0, 0)),
                  pl.BlockSpec((1, nh, dqk), lambda b, c: (b, 0, 0)),
                  pl.BlockSpec((1, nh, LANES), lambda b, c: (b, 0, 0)),
                  pl.BlockSpec((1, nh * dv), lambda b, c: (0, 0))],
        out_specs=[pl.BlockSpec((1, lc, nh * dv), lambda b, c: (b, c, 0)),
                   pl.BlockSpec((1, nh, dqk, dv), lambda b, c: (b, 0, 0, 0)),
                   pl.BlockSpec((1, nh, dqk), lambda b, c: (b, 0, 0)),
                   pl.BlockSpec((1, nh, LANES), lambda b, c: (b, 0, 0))],
        out_shape=[jax.ShapeDtypeStruct((B, T, nh * dv), BF16),
                   jax.ShapeDtypeStruct((B, nh, dqk, dv), F32),
                   jax.ShapeDtypeStruct((B, nh, dqk), F32),
                   jax.ShapeDtypeStruct((B, nh, LANES), F32)],
        scratch_shapes=[pltpu.VMEM((nh, dqk, dv), F32),
                        pltpu.VMEM((nh, dqk), F32),
                        pltpu.VMEM((nh, LANES), F32)],
        compiler_params=_params("parallel", "arbitrary"),
        name="mlstm_recurrence",
    )(q, k, v, gcol, grow, c0.astype(F32), n0.astype(F32), m0b, norm_g.reshape(1, nh * dv))
    return h, c_new, n_new, m_new[:, :, 0]


def _pool_body(xb_ref, st_ref, wg_ref, sc_ref, y_ref, ext_ref, *, tb, pos0, nbuf):
    c = pl.program_id(1)
    lo = POOL_HALO - nbuf
    bb = xb_ref.shape[0]
    gw = wg_ref.shape[1]

    @pl.when(c == 0)
    def _():
        ext_ref[:, lo:POOL_HALO, :] = st_ref[...]

    @pl.when(c > 0)
    def _():
        ext_ref[:, lo:POOL_HALO, :] = ext_ref[:, lo + tb:POOL_HALO + tb, :]

    ext_ref[:, POOL_HALO:POOL_HALO + tb, :] = xb_ref[...].astype(F32)
    pos = pos0 + c * tb + lax.broadcasted_iota(jnp.int32, (1, tb, 1), 1)
    for g, w in enumerate(POOL_WINDOWS):
        cols = slice(g * gw, (g + 1) * gw)
        cur = ext_ref[:, POOL_HALO:POOL_HALO + tb, cols]
        win = cur
        for i in range(1, w):
            win = win + ext_ref[:, POOL_HALO - i:POOL_HALO - i + tb, cols]
        cnt = jnp.minimum(w, pos + 1).astype(F32)
        d = (win / cnt - cur).reshape(bb * tb, gw).astype(BF16)
        yg = _dot(d, wg_ref[g]) * sc_ref[:, cols]
        y_ref[:, :, cols] = yg.reshape(bb, tb, gw).astype(y_ref.dtype)


def pool_mix(xb, state, w_grp, scale, pos0):
    B, T, E = xb.shape
    nbuf = state.shape[1]
    bb, tb = (1, 256) if T % 256 == 0 else (B, T)
    ng, gw = w_grp.shape[0], w_grp.shape[1]
    return pl.pallas_call(
        functools.partial(_pool_body, tb=tb, pos0=pos0, nbuf=nbuf),
        grid=(B // bb, T // tb),
        in_specs=[pl.BlockSpec((bb, tb, E), lambda i, c: (i, c, 0)),
                  pl.BlockSpec((bb, nbuf, E), lambda i, c: (i, 0, 0)),
                  pl.BlockSpec((ng, gw, gw), lambda i, c: (0, 0, 0)),
                  pl.BlockSpec((1, E), lambda i, c: (0, 0))],
        out_specs=pl.BlockSpec((bb, tb, E), lambda i, c: (i, c, 0)),
        out_shape=jax.ShapeDtypeStruct((B, T, E), BF16),
        scratch_shapes=[pltpu.VMEM((bb, POOL_HALO + tb, E), F32)],
        compiler_params=_params("parallel", "arbitrary"),
        name="pool_mix",
    )(xb, state.astype(F32), w_grp, scale.reshape(1, E))


def _tail_body(*refs, kind, final):
    if kind == "mlstm":
        hn_ref, o_ref, xc_ref, z_ref, skip_ref = refs[:5]
        rest = refs[5:]
        mix = (o_ref[...].astype(F32) * hn_ref[...].astype(F32)
               + skip_ref[...] * xc_ref[...].astype(F32)) * _silu(z_ref[...].astype(F32))
    else:
        y_ref, z_ref = refs[:2]
        rest = refs[2:]
        mix = y_ref[...].astype(F32) * _silu(z_ref[...].astype(F32))
    x_ref, p_ref, wd_ref, gw_ref, pw_ref = rest[:5]
    rest = rest[5:]
    x1 = x_ref[...] + _dot(mix.astype(BF16), wd_ref[...])
    gate = _sigmoid(_dot(x1.astype(BF16), gw_ref[...]))
    x2 = x1 + gate * _dot(p_ref[...].astype(BF16), pw_ref[...])
    if final:
        fg_ref, xo_ref, yo_ref = rest
        yo_ref[...] = x2 * lax.rsqrt(jnp.mean(x2 * x2, axis=-1, keepdims=True) + EPS) * fg_ref[...]
    else:
        (xo_ref,) = rest
    xo_ref[...] = x2


def layer_tail(kind, mix_inputs, x, p, w_down, gate_w, ple_w, skip=None, final_g=None):
    M, D = x.shape
    tm = _pick(M, (512, 256))
    row = lambda i: (i, 0)
    fixed = lambda i: (0, 0)
    in_specs, args = [], []
    for a in mix_inputs:
        in_specs.append(pl.BlockSpec((tm, a.shape[1]), row))
        args.append(a)
    if kind == "mlstm":
        in_specs.append(pl.BlockSpec((1, skip.shape[-1]), fixed))
        args.append(skip.reshape(1, -1))
    in_specs += [pl.BlockSpec((tm, D), row), pl.BlockSpec((tm, p.shape[1]), row),
                 pl.BlockSpec(w_down.shape, fixed), pl.BlockSpec(gate_w.shape, fixed),
                 pl.BlockSpec(ple_w.shape, fixed)]
    args += [x, p, w_down, gate_w, ple_w]
    out_specs = [pl.BlockSpec((tm, D), row)]
    out_shape = [jax.ShapeDtypeStruct((M, D), F32)]
    if final_g is not None:
        in_specs.append(pl.BlockSpec((1, D), fixed))
        args.append(final_g.reshape(1, D))
        out_specs.append(pl.BlockSpec((tm, D), row))
        out_shape.append(jax.ShapeDtypeStruct((M, D), F32))
    return pl.pallas_call(
        functools.partial(_tail_body, kind=kind, final=final_g is not None),
        grid=(M // tm,),
        in_specs=in_specs, out_specs=out_specs, out_shape=out_shape,
        compiler_params=_params("parallel"),
        name="layer_tail_" + kind,
    )(*args)


def _sortable_key(s):
    s = jnp.where(s == 0.0, 0.0, s)
    bits = lax.bitcast_convert_type(s, jnp.int32)
    return jnp.where(bits < 0, bits ^ jnp.int32(0x7FFFFFFF), bits)


COUNT_STRIP_VREGS = 32


def _count(keys_ref, nvalid, pred):
    rows, cw = keys_ref.shape[1], keys_ref.shape[2]
    strip = max(SUBLANES, min(rows, COUNT_STRIP_VREGS * SUBLANES * LANES // cw))

    def body(kc, acc):
        parts = []
        for r0 in range(0, rows, strip):
            rs = slice(r0, min(r0 + strip, rows))
            hit = jnp.where(pred(keys_ref[kc, rs, :], kc, rs), 1.0, 0.0)
            part = hit[:, 0:LANES]
            for g in range(1, cw // LANES):
                part = part + hit[:, g * LANES:(g + 1) * LANES]
            parts.append(part)
        return acc + (parts[0] if len(parts) == 1 else jnp.concatenate(parts, axis=0))

    acc = lax.fori_loop(0, nvalid, body, jnp.zeros((rows, LANES), F32))
    return jnp.sum(acc, axis=1, keepdims=True)


def _select_topk(keys_ref, cidx_ref, nvalid, k_top, idx_bits):
    rows, cw = keys_ref.shape[1], keys_ref.shape[2]
    kf = float(k_top)
    imin = jnp.int32(INT_MIN)

    def bit_body(i, tau_u):
        cand_u = tau_u | jnp.left_shift(jnp.int32(1), 31 - i)
        cand = cand_u ^ imin
        cnt = _count(keys_ref, nvalid, lambda kk, kc, rs: kk >= cand[rs])
        return jnp.where(cnt >= kf, cand_u, tau_u)

    tau_u = lax.fori_loop(0, 32, bit_body, jnp.zeros((rows, 1), jnp.int32))
    tau = tau_u ^ imin
    n_gt = _count(keys_ref, nvalid, lambda kk, kc, rs: kk > tau[rs])
    n_ge = _count(keys_ref, nvalid, lambda kk, kc, rs: kk >= tau[rs])
    need = kf - n_gt
    short = tau_u == 0
    cidx_ref[...] = jnp.broadcast_to(jnp.where(short, -1, 2 ** 30), cidx_ref.shape)
    excess = jnp.max(jnp.where((n_ge > kf) & jnp.logical_not(short), 1.0, 0.0))

    @pl.when(excess > 0.5)
    def _():
        lane = lax.broadcasted_iota(jnp.int32, (1, cw), 1)

        def idx_body(i, cut):
            cand = cut | jnp.left_shift(jnp.int32(1), idx_bits - 1 - i)
            cnt = _count(keys_ref, nvalid,
                         lambda kk, kc, rs: (kk == tau[rs]) & ((kc * cw + lane) < cand[rs]))
            return jnp.where(cnt < need, cand, cut)

        cut = lax.fori_loop(0, idx_bits, idx_body, jnp.zeros((rows, 1), jnp.int32))
        cidx_ref[...] = jnp.broadcast_to(jnp.where(short, -1, cut), cidx_ref.shape)

    return tau


def _selected(kk, colv, tau, cut):
    return (kk > tau) | ((kk == tau) & (colv <= cut))


def _n_causal_chunks(qb, tq, tk):
    return lax.div(qb * tq + (tq - 1), jnp.int32(tk)) + 1


def _idx_prompt_body(qi_ref, wi_ref, ki_ref, bias_ref, keys_ref, cidx_ref, *,
                     nh, dh, tq, tk, nk, k_top, idx_bits):
    per = keys_ref.shape[2] // tk
    gw = per * tk
    qb = pl.program_id(1)
    nvalid = _n_causal_chunks(qb, tq, gw)
    rowpos = qb * tq + lax.broadcasted_iota(jnp.int32, (tq, 1), 0)
    lane = lax.broadcasted_iota(jnp.int32, (1, tk), 1)
    wi = wi_ref[0]

    def score_group(gc, carry):
        for part in range(per):
            kt = ki_ref[0, gc * per + part]
            s = jnp.zeros((tq, tk), F32)
            for h in range(nh):
                rel = jnp.maximum(_dot_nt(qi_ref[0, :, h * dh:(h + 1) * dh], kt), 0.0)
                s = s + rel * wi[:, h:h + 1]
            colv = gc * gw + part * tk + lane
            keys_ref[gc, :, part * tk:(part + 1) * tk] = jnp.where(
                colv <= rowpos, _sortable_key(s), jnp.int32(INT_MIN))
        return carry

    lax.fori_loop(0, nvalid, score_group, 0)
    tau = _select_topk(keys_ref, cidx_ref, nvalid, k_top, idx_bits)
    cut = cidx_ref[:, 0:1]

    def write_group(gc, carry):
        for part in range(per):
            kk = keys_ref[gc, :, part * tk:(part + 1) * tk]
            sel = _selected(kk, gc * gw + part * tk + lane, tau, cut)
            bias_ref[0, 0, gc * per + part] = jnp.where(sel, 0.0, NEG).astype(bias_ref.dtype)
        return carry

    lax.fori_loop(0, nvalid, write_group, 0)

    def fill_chunk(kc, carry):
        bias_ref[0, 0, kc] = jnp.full((tq, tk), NEG, bias_ref.dtype)
        return carry

    lax.fori_loop(nvalid * per, nk, fill_chunk, 0)


def indexer_bias_prompt(qi, wi, ki, k_top):
    B, T, _ = qi.shape
    dh = ki.shape[-1]
    nh = qi.shape[-1] // dh
    tq, tk = min(IDX_TQ, T), min(ATT_TK, T)
    tqa = min(ATT_TQ, T)
    sub = tqa // tq
    nq, nk = T // tq, T // tk
    per = 2 if nk % 2 == 0 else 1
    idx_bits = int(T).bit_length() + 1
    return pl.pallas_call(
        functools.partial(_idx_prompt_body, nh=nh, dh=dh, tq=tq, tk=tk, nk=nk, k_top=k_top,
                          idx_bits=idx_bits),
        grid=(B, nq),
        in_specs=[pl.BlockSpec((1, tq, nh * dh), lambda b, i: (b, i, 0)),
                  pl.BlockSpec((1, tq, LANES), lambda b, i: (b, i, 0)),
                  pl.BlockSpec((1, nk, tk, dh), lambda b, i: (b, 0, 0, 0))],
        out_specs=pl.BlockSpec((1, 1, nk, tq, tk), lambda b, i: (b, i // sub, 0, i % sub, 0)),
        out_shape=jax.ShapeDtypeStruct((B, T // tqa, nk, tqa, tk), BF16),
        scratch_shapes=[pltpu.VMEM((nk // per, tq, per * tk), jnp.int32),
                        pltpu.VMEM((tq, LANES), jnp.int32)],
        compiler_params=_params("parallel", "arbitrary"),
        name="indexer_bias_prompt",
    )(qi, wi, ki.reshape(B, nk, tk, dh))


def _attn_prompt_body(q_ref, k_ref, v_ref, bias_ref, o_ref, m_s, acc_s, *, nh, dh, tq, tk):
    qb, kb = pl.program_id(1), pl.program_id(2)
    nvalid = _n_causal_chunks(qb, tq, tk)

    @pl.when(kb == 0)
    def _():
        m_s[...] = jnp.full(m_s.shape, NEG, F32)
        acc_s[...] = jnp.zeros(acc_s.shape, F32)

    @pl.when(kb < nvalid)
    def _():
        bias = bias_ref[0, 0, 0].astype(F32)
        for h in range(nh):
            s = _dot_nt(q_ref[0, h], k_ref[0, h]) + bias
            m_prev = m_s[h]
            m_new = jnp.maximum(m_prev, jnp.max(s, axis=1, keepdims=True))
            alpha = jnp.exp2(m_prev - m_new)
            p = jnp.exp2(s - jnp.concatenate([m_new] * (tk // LANES), axis=1))
            acc_s[h] = alpha * acc_s[h] + _dot(p.astype(BF16), v_ref[0, h])
            m_s[h] = m_new

    @pl.when(kb == pl.num_programs(2) - 1)
    def _():
        for h in range(nh):
            acc = acc_s[h]
            o_ref[0, :, h * dh:(h + 1) * dh] = (acc[:, :dh] / acc[:, dh:dh + 1]).astype(o_ref.dtype)


def attention_prompt(q, k, v, bias):
    B, nh, T, dh = q.shape
    assert tuple(v.shape) == (B, nh, T, LANES) and dh < LANES
    _, nq, nk, tq, tk = bias.shape

    def kv_map(b, i, j):
        return (b, 0, jnp.minimum(j, _n_causal_chunks(i, tq, tk) - 1), 0)

    def bias_map(b, i, j):
        return (b, i, jnp.minimum(j, _n_causal_chunks(i, tq, tk) - 1), 0, 0)

    return pl.pallas_call(
        functools.partial(_attn_prompt_body, nh=nh, dh=dh, tq=tq, tk=tk),
        grid=(B, nq, nk),
        in_specs=[pl.BlockSpec((1, nh, tq, dh), lambda b, i, j: (b, 0, i, 0)),
                  pl.BlockSpec((1, nh, tk, dh), kv_map),
                  pl.BlockSpec((1, nh, tk, LANES), kv_map),
                  pl.BlockSpec((1, 1, 1, tq, tk), bias_map)],
        out_specs=pl.BlockSpec((1, tq, nh * dh), lambda b, i, j: (b, i, 0)),
        out_shape=jax.ShapeDtypeStruct((B, T, nh * dh), BF16),
        scratch_shapes=[pltpu.VMEM((nh, tq, LANES), F32),
                        pltpu.VMEM((nh, tq, LANES), F32)],
        compiler_params=_params("parallel", "parallel", "arbitrary"),
        name="attention_prompt",
    )(q, k, v, bias)


def _idx_sample_body(pt_ref, qi_ref, wi_ref, knew_ref, *rest, nh, ts, npg, nsteps, k_top,
                     idx_bits):
    page_refs = rest[:npg]
    bias_ref, keys_ref, cidx_ref = rest[npg:]
    s = pl.program_id(1)
    cw = keys_ref.shape[2]
    wi = wi_ref[0]
    qi = qi_ref[0]

    def scores(kt):
        full = jnp.maximum(_dot(qi, kt), 0.0)
        out = jnp.zeros((ts, cw), F32)
        for h in range(nh):
            out = out + full[h * ts:(h + 1) * ts, :] * wi[:, h:h + 1]
        return out

    kt = jnp.concatenate([r[...] for r in page_refs], axis=1).astype(BF16)
    keys_ref[s] = _sortable_key(scores(kt))

    @pl.when(s == nsteps - 1)
    def _():
        lane = lax.broadcasted_iota(jnp.int32, (1, cw), 1)
        past = nsteps * cw
        rowpos = past + lax.broadcasted_iota(jnp.int32, (ts, 1), 0)
        colv_new = past + lane
        keys_ref[nsteps] = jnp.where(colv_new <= rowpos, _sortable_key(scores(knew_ref[0])),
                                     jnp.int32(INT_MIN))
        tau = _select_topk(keys_ref, cidx_ref, nsteps + 1, k_top, idx_bits)
        cut = cidx_ref[:, 0:1]
        for kc in range(nsteps + 1):
            sel = _selected(keys_ref[kc], kc * cw + lane, tau, cut)
            bias_ref[0, kc] = jnp.where(sel, 0.0, NEG).astype(bias_ref.dtype)


def _page_specs(n_pages_per_step, block, layer, first_step):
    specs = []
    for i in range(n_pages_per_step):
        def imap(b, s, pt, i=i):
            step = jnp.maximum(s - first_step, 0)
            return (layer, pt[b, step * n_pages_per_step + i]) + (0,) * len(block)
        specs.append(pl.BlockSpec((None, None) + tuple(block), imap))
    return specs


def indexer_bias_sample(qi_hq, wi, ki_new_t, cache_kidx_t, layer, page_table, k_top):
    B, n_pages = page_table.shape
    _, _, dh, page = cache_kidx_t.shape
    ts = wi.shape[1]
    nh = qi_hq.shape[1] // ts
    npg = min(PAGES_PER_STEP, n_pages)
    nsteps = n_pages // npg
    cw = npg * page
    idx_bits = int(n_pages * page + cw).bit_length() + 1
    grid_spec = pltpu.PrefetchScalarGridSpec(
        num_scalar_prefetch=1,
        grid=(B, nsteps),
        in_specs=[pl.BlockSpec((1, nh * ts, dh), lambda b, s, pt: (b, 0, 0)),
                  pl.BlockSpec((1, ts, LANES), lambda b, s, pt: (b, 0, 0)),
                  pl.BlockSpec((1, dh, cw), lambda b, s, pt: (b, 0, 0))]
        + _page_specs(npg, (dh, page), layer, 0),
        out_specs=pl.BlockSpec((1, nsteps + 1, ts, cw), lambda b, s, pt: (b, 0, 0, 0)),
        scratch_shapes=[pltpu.VMEM((nsteps + 1, ts, cw), jnp.int32),
                        pltpu.VMEM((ts, LANES), jnp.int32)],
    )
    return pl.pallas_call(
        functools.partial(_idx_sample_body, nh=nh, ts=ts, npg=npg, nsteps=nsteps, k_top=k_top,
                          idx_bits=idx_bits),
        grid_spec=grid_spec,
        out_shape=jax.ShapeDtypeStruct((B, nsteps + 1, ts, cw), F32),
        compiler_params=_params("parallel", "arbitrary"),
        name="indexer_bias_sample",
    )(page_table, qi_hq, wi, ki_new_t, *([cache_kidx_t] * npg))


def _attn_sample_body(pt_ref, qbd_ref, knew_ref, vnew_ref, bias_ref, *rest, nh, ts, npg, dh):
    kpages, vpages = rest[:npg], rest[npg:2 * npg]
    o_ref, m_s, l_s, acc_s = rest[2 * npg:]
    s = pl.program_id(1)
    rows = nh * ts

    @pl.when(s == 0)
    def _():
        m_s[...] = jnp.full(m_s.shape, NEG, F32)
        l_s[...] = jnp.zeros(l_s.shape, F32)
        acc_s[...] = jnp.zeros(acc_s.shape, F32)

    def update(kt, vt):
        n = kt.shape[1]
        bias = bias_ref[0, 0][:, :n]
        logits = _dot(qbd_ref[0], kt) + jnp.concatenate([bias] * nh, axis=0)
        m_prev = m_s[...]
        m_new = jnp.maximum(m_prev, jnp.max(logits, axis=1, keepdims=True))
        alpha = jnp.exp2(m_prev - m_new)
        p = jnp.exp2(logits - m_new)
        l_s[...] = alpha * l_s[...] + jnp.sum(p, axis=1, keepdims=True)
        acc_s[...] = alpha * acc_s[...] + _dot_nt(p.astype(BF16), vt)
        m_s[...] = m_new

    @pl.when(s == 0)
    def _():
        update(knew_ref[0], vnew_ref[0])

    def gather_pages(page_refs):
        return jnp.concatenate([r[...].reshape(nh * dh, r.shape[2]) for r in page_refs],
                               axis=1).astype(BF16)

    @pl.when(s > 0)
    def _():
        update(gather_pages(kpages), gather_pages(vpages))

    @pl.when(s == pl.num_programs(1) - 1)
    def _():
        o = acc_s[...] / l_s[...]
        lane = lax.broadcasted_iota(jnp.int32, (1, nh * dh), 1)
        res = jnp.zeros((ts, nh * dh), F32)
        for h in range(nh):
            own = (lane >= h * dh) & (lane < (h + 1) * dh)
            res = res + jnp.where(own, o[h * ts:(h + 1) * ts, :], 0.0)
        o_ref[0] = res.astype(o_ref.dtype)


def attention_sample(qbd, k_new_t, v_new_t, bias, cache_k_t, cache_v_t, layer, page_table, ts):
    B, n_pages = page_table.shape
    _, _, nh, dh, page = cache_k_t.shape
    width = nh * dh
    n_new = k_new_t.shape[2]
    nsteps1, cw = bias.shape[1], bias.shape[3]
    npg = cw // page
    rows = nh * ts

    def bias_map(b, s, pt):
        return (b, jnp.where(s == 0, nsteps1 - 1, s - 1), 0, 0)

    grid_spec = pltpu.PrefetchScalarGridSpec(
        num_scalar_prefetch=1,
        grid=(B, nsteps1),
        in_specs=[pl.BlockSpec((1, rows, width), lambda b, s, pt: (b, 0, 0)),
                  pl.BlockSpec((1, width, n_new), lambda b, s, pt: (b, 0, 0)),
                  pl.BlockSpec((1, width, n_new), lambda b, s, pt: (b, 0, 0)),
                  pl.BlockSpec((1, 1, ts, cw), bias_map)]
        + _page_specs(npg, (nh, dh, page), layer, 1) + _page_specs(npg, (nh, dh, page), layer, 1),
        out_specs=pl.BlockSpec((1, ts, width), lambda b, s, pt: (b, 0, 0)),
        scratch_shapes=[pltpu.VMEM((rows, 1), F32),
                        pltpu.VMEM((rows, 1), F32),
                        pltpu.VMEM((rows, width), F32)],
    )
    return pl.pallas_call(
        functools.partial(_attn_sample_body, nh=nh, ts=ts, npg=npg, dh=dh),
        grid_spec=grid_spec,
        out_shape=jax.ShapeDtypeStruct((B, ts, width), BF16),
        compiler_params=_params("parallel", "arbitrary"),
        name="attention_sample",
    )(page_table, qbd, k_new_t, v_new_t, bias, *([cache_k_t] * npg), *([cache_v_t] * npg))


def _pad_time(a, t_pad, value=0.0):
    return jnp.pad(a, ((0, 0), (0, t_pad - a.shape[1]), (0, 0)), constant_values=value)


def _mlstm_layer(x, g, B, T, state, w, j):
    c0, n0, m0, conv0 = state
    nh = c0.shape[1]
    E = w["a_conv_w"].shape[-1]
    w_up = w["a_w_up"][j]
    xb, = norm_matmul(x, g, w_up[:, :E], (BF16,))
    z, = norm_matmul(x, g, w_up[:, E:], (BF16,))
    xc, gcol = conv_silu_gates(xb.reshape(B, T, E), conv0, w["a_conv_w"][j], w["a_conv_b"][j],
                               w["a_w_if_pad"][j], w["a_b_if_pad"][j], nh)
    xc = xc.reshape(B * T, E)
    q = matmul(xc, w["a_w_q"][j], BF16)
    k = matmul(xc, w["a_w_k"][j], BF16)
    v = matmul(xb, w["a_w_v"][j], BF16)
    o = matmul(xb, w["a_w_o"][j], BF16, bias=w["a_b_o"][j], act="sigmoid")
    q3, k3, v3 = q.reshape(B, T, -1), k.reshape(B, T, -1), v.reshape(B, T, -1)
    tp = T if T % LANES == 0 else ((T + LANES - 1) // LANES) * LANES
    if tp != T:
        q3, k3, v3 = _pad_time(q3, tp), _pad_time(k3, tp), _pad_time(v3, tp)
        pad_row = jnp.where(jnp.arange(LANES) < nh, NEG, 0.0).astype(F32)
        gcol = jnp.concatenate([gcol, jnp.broadcast_to(pad_row, (B, tp - T, LANES))], axis=1)
    grow = jnp.swapaxes(gcol[:, :, :2 * nh], 1, 2)
    hn, c_new, n_new, m_new = mlstm_recurrence(q3, k3, v3, gcol, grow, c0, n0, m0, w["a_norm_g"][j])
    hn = hn[:, :T].reshape(B * T, -1)
    assert T >= CONV_W - 1
    conv_new = xb.reshape(B, T, E)[:, T - (CONV_W - 1):].astype(F32)
    return (hn, o, xc, z), (c_new, n_new, m_new, conv_new)


def _rope_tables(pos):
    half = 32
    inv = ROPE_THETA ** (-np.arange(half, dtype=np.float64) / half)
    ang = np.asarray(pos, np.float64)[:, None] * inv[None, :]
    cos = np.tile(np.cos(ang), (1, 4))
    sin = np.tile(np.concatenate([-np.sin(ang), np.sin(ang)], axis=1), (1, 2))
    return jnp.asarray(cos, F32), jnp.asarray(sin, F32)


def _dsa_project(x, g, B, T, pos, w, j, nh, dh, nhi, di, head_major=False):
    M = B * T
    cos, sin = _rope_tables(pos)
    tm = _pick(T, ROW_TILES) if T % ROW_TILES[-1] == 0 else _pick(M, ROW_TILES)
    if T % tm == 0:
        tabs, tab_blocks = (cos, sin), T // tm
    else:
        tabs, tab_blocks = (jnp.tile(cos, (M // T, 1)), jnp.tile(sin, (M // T, 1))), M // tm
    rope = dict(rope_tabs=tabs, tab_blocks=tab_blocks, row_tile=tm)
    hm = (lambda *widths: (B, T, dh, widths)) if head_major else (lambda *widths: None)
    q, = norm_matmul(x, g, w["b_wq"][j], (BF16,), scale=dh ** -0.5 * math.log2(math.e),
                     head_major=hm(dh), **rope)
    k32, k16 = norm_matmul(x, g, w["b_wk"][j], (F32, BF16), head_major=hm(0, dh), **rope)
    v32, v16 = norm_matmul(x, g, w["b_wv"][j], (F32, BF16), head_major=hm(0, LANES), row_tile=tm)
    gate, = norm_matmul(x, g, w["b_wg"][j], (BF16,))
    qi, = norm_matmul(x, g, w["b_wqi"][j], (BF16,), **rope)
    ki32, = norm_matmul(x, g, w["b_wki_pad"][j], (F32,), **rope)
    wi, = norm_matmul(x, g, w["b_wwi_pad"][j], (F32,), scale=nhi ** -0.5 * di ** -0.5)
    return q, k32, k16, v32, v16, gate, qi, ki32[:, :di], wi


def _dsa_layer_prompt(x, g, B, T, w, j, dims):
    nh, dh, nhi, di = dims
    q, k32, k16, v32, v16, gate, qi, ki32, wi = _dsa_project(
        x, g, B, T, np.arange(T), w, j, *dims, head_major=True)
    k_top = min(TOPK_MAX, T // 4)
    bias = indexer_bias_prompt(qi.reshape(B, T, nhi * di), wi.reshape(B, T, LANES),
                               ki32.astype(BF16).reshape(B, T, di), k_top)
    attn = attention_prompt(q, k16, v16, bias).reshape(B * T, nh * dh)
    new =(k32.reshape(B, T, nh, dh), v32.reshape(B, T, nh, dh), ki32.reshape(B, T, di))
    return (attn, gate), new


def _dsa_layer_sample(x, g, B, T, w, j, dims, ctx):
    nh, dh, nhi, di = dims
    cache_k, cache_v, cache_kidx, page_table = ctx
    n_pages = page_table.shape[1]
    page = cache_kidx.shape[2]
    past = n_pages * page
    q, k32, k16, v32, v16, gate, qi, ki32, wi = _dsa_project(
        x, g, B, T, past + np.arange(T), w, j, *dims)
    k_top = min(TOPK_MAX, (past + T) // 4)
    cw = min(PAGES_PER_STEP, n_pages) * page
    qi_hq = jnp.transpose(qi.reshape(B, T, nhi, di), (0, 2, 1, 3)).reshape(B, nhi * T, di)
    token_last = lambda a, n: jnp.swapaxes(_pad_time(a, n), 1, 2)
    ki_new_t = token_last(ki32.astype(BF16).reshape(B, T, di), cw)
    bias = indexer_bias_sample(qi_hq, wi.reshape(B, T, LANES), ki_new_t,
                               jnp.swapaxes(cache_kidx, 2, 3), j, page_table, k_top)
    q3 = q.reshape(B, T, nh * dh)
    own = (jnp.arange(nh * T)[:, None] // T) == (jnp.arange(nh * dh)[None, :] // dh)
    qbd = jnp.where(own[None], jnp.tile(q3, (1, nh, 1)), 0).astype(BF16)
    width = nh * dh
    n_new = ((T + LANES - 1) // LANES) * LANES
    attn = attention_sample(qbd, token_last(k16.reshape(B, T, width), n_new),
                            token_last(v16.reshape(B, T, width), n_new), bias,
                            jnp.transpose(cache_k, (0, 1, 3, 4, 2)),
                            jnp.transpose(cache_v, (0, 1, 3, 4, 2)), j, page_table, T)
    new = (k32.reshape(B, T, nh, dh), v32.reshape(B, T, nh, dh), ki32.reshape(B, T, di))
    return (attn.reshape(B * T, width), gate), new


def _pool_layer(x, g, B, T, buf, pos0, w, j):
    E = w["c_scale"].shape[-1]
    w_up = w["c_w_up"][j]
    xb, = norm_matmul(x, g, w_up[:, :E], (BF16,))
    z, = norm_matmul(x, g, w_up[:, E:], (BF16,))
    xb3 = xb.reshape(B, T, E)
    y = pool_mix(xb3, buf, w["c_w_grp"][j], w["c_scale"][j], pos0).reshape(B * T, E)
    nbuf = buf.shape[1]
    if T >= nbuf:
        new_buf = xb3[:, T - nbuf:].astype(F32)
    else:
        new_buf = jnp.concatenate([buf[:, T:].astype(F32), xb3.astype(F32)], axis=1)
    return (y, z), new_buf


def _run_group(x3, p4, pos0, mlstm_state, dsa_ctx, pool_state, w, dims):
    B, T, D = x3.shape
    depth = p4.shape[0]
    x = x3.reshape(B * T, D)
    new_a, new_b, new_c = [], [], []
    y = None
    for i in range(depth):
        kind, j = i % 3, i // 3
        g = w["norm_g"][i]
        final_g = w["final_g"] if i == depth - 1 else None
        p = p4[i].reshape(B * T, -1)
        if kind == 0:
            state = tuple(s[j] for s in mlstm_state)
            mix, st = _mlstm_layer(x, g, B, T, state, w, j)
            new_a.append(st)
            out = layer_tail("mlstm", mix, x, p, w["a_w_down"][j], w["ple_gate_w"][i], w["ple_w"][i],
                             skip=w["a_skip"][j], final_g=final_g)
        elif kind == 1:
            if dsa_ctx is None:
                mix, st = _dsa_layer_prompt(x, g, B, T, w, j, dims)
            else:
                mix, st = _dsa_layer_sample(x, g, B, T, w, j, dims, dsa_ctx)
            new_b.append(st)
            out = layer_tail("gated", mix, x, p, w["b_w_out"][j], w["ple_gate_w"][i], w["ple_w"][i],
                             final_g=final_g)
        else:
            mix, st = _pool_layer(x, g, B, T, pool_state[j], pos0, w, j)
            new_c.append(st)
            out = layer_tail("gated", mix, x, p, w["c_w_down"][j], w["ple_gate_w"][i], w["ple_w"][i],
                             final_g=final_g)
        x = out[0]
        if final_g is not None:
            y = out[1]
    a_states = tuple(jnp.stack([s[r] for s in new_a]) for r in range(4))
    b_states = tuple(jnp.stack([s[r] for s in new_b]) for r in range(3))
    c_state = jnp.stack(new_c)
    return y.reshape(B, T, D), a_states, b_states, c_state


def kernel(x_prompt, x_sample, state_mlstm_C, state_mlstm_n, state_mlstm_m, state_mlstm_conv, state_pool,
           cache_k, cache_v, cache_kidx, page_table, p_prompt, p_sample,
           norm_g, final_g, ple_w, ple_gate_w,
           a_w_up, a_conv_w, a_conv_b, a_w_q, a_w_k, a_w_v, a_w_if, a_b_if, a_w_o, a_b_o, a_norm_g,
           a_skip, a_w_down, b_w_in, b_w_out, c_w_up, c_w_grp, c_scale, c_w_down):
    D = x_prompt.shape[-1]
    nh, dh = cache_k.shape[3], cache_k.shape[4]
    di = cache_kidx.shape[-1]
    aw = nh * dh
    nhi = (b_w_in.shape[-1] - 4 * aw - di) // (di + 1)
    dims = (nh, dh, nhi, di)
    nha = state_mlstm_C.shape[2]
    bf = lambda a: a.astype(BF16)

    def pad_cols(a, n):
        return jnp.pad(a, [(0, 0)] * (a.ndim - 1) + [(0, n - a.shape[-1])])

    o1, o2, o3, o4, o5, o6 = np.cumsum([aw, aw, aw, aw, nhi * di, di])
    w = dict(
        norm_g=norm_g, final_g=final_g, ple_w=bf(ple_w), ple_gate_w=bf(ple_gate_w),
        a_w_up=bf(a_w_up), a_conv_w=a_conv_w, a_conv_b=a_conv_b,
        a_w_q=bf(a_w_q), a_w_k=bf(a_w_k), a_w_v=bf(a_w_v), a_w_o=bf(a_w_o), a_b_o=a_b_o,
        a_w_if_pad=pad_cols(a_w_if, LANES), a_b_if_pad=pad_cols(a_b_if, LANES),
        a_norm_g=a_norm_g, a_skip=a_skip, a_w_down=bf(a_w_down),
        b_wq=bf(b_w_in[..., :o1]), b_wk=bf(b_w_in[..., o1:o2]), b_wv=bf(b_w_in[..., o2:o3]),
        b_wg=bf(b_w_in[..., o3:o4]), b_wqi=bf(b_w_in[..., o4:o5]),
        b_wki_pad=bf(pad_cols(b_w_in[..., o5:o6], LANES)),
        b_wwi_pad=bf(pad_cols(b_w_in[..., o6:], LANES)),
        b_w_out=bf(b_w_out), c_w_up=bf(c_w_up), c_w_grp=bf(c_w_grp), c_scale=c_scale,
        c_w_down=bf(c_w_down),
    )

    Bp = x_prompt.shape[0]
    na, nc = state_mlstm_C.shape[0], state_pool.shape[0]
    zeros_a = (jnp.zeros((na, Bp) + state_mlstm_C.shape[2:], F32),
               jnp.zeros((na, Bp) + state_mlstm_n.shape[2:], F32),
               jnp.zeros((na, Bp) + state_mlstm_m.shape[2:], F32),
               jnp.zeros((na, Bp) + state_mlstm_conv.shape[2:], F32))
    zeros_c = jnp.zeros((nc, Bp) + state_pool.shape[2:], F32)
    y_p, a_p, b_p, c_p = _run_group(x_prompt, p_prompt, 0, zeros_a, None, zeros_c, w, dims)

    past = page_table.shape[1] * cache_k.shape[2]
    y_s, a_s, b_s, c_s = _run_group(
        x_sample, p_sample, past,
        (state_mlstm_C, state_mlstm_n, state_mlstm_m, state_mlstm_conv),
        (cache_k, cache_v, cache_kidx, page_table), state_pool, w, dims)
    return (y_p, y_s, *a_p, *b_p, c_p, *a_s, *b_s, c_s)
```

```python
import functools
import math

import numpy as np
import jax
import jax.numpy as jnp
from jax import lax
from jax.experimental import pallas as pl
from jax.experimental.pallas import tpu as pltpu

EPS = 1e-6
ROPE_THETA = 10000.0
TOPK_MAX = 256
POOL_WINDOWS = (2, 4, 8, 16)
CONV_W = 4
MLSTM_CHUNK = 256
NEG = -1e30
LANES = 128
SUBLANES = 8
VMEM_LIMIT_BYTES = 56 * 1024 * 1024
ATT_TQ, ATT_TK = 512, 512
IDX_TQ = 256
PAGES_PER_STEP = 8
IDX_PAGES_PER_STEP = 32
ROW_TILES = (1024, 512, 256)
COL_TILES = (1024, 512, 256, 128)

F32 = jnp.float32
BF16 = jnp.bfloat16
INT_MIN = -2 ** 31


def _params(*sem):
    return pltpu.CompilerParams(dimension_semantics=sem, vmem_limit_bytes=VMEM_LIMIT_BYTES)


def _sigmoid(x):
    return 1.0 / (1.0 + jnp.exp(-x))


def _silu(x):
    return x * _sigmoid(x)


def _log_sigmoid(x):
    return jnp.minimum(x, 0.0) - jnp.log(1.0 + jnp.exp(-jnp.abs(x)))


def _dot(a, b):
    return jnp.dot(a, b, preferred_element_type=F32)


def _dot_nt(a, b):
    return lax.dot_general(a, b, (((1,), (1,)), ((), ())), preferred_element_type=F32)


def _dot_tn(a, b):
    return lax.dot_general(a, b, (((0,), (0,)), ((), ())), preferred_element_type=F32)


def _pick(n, cands):
    for c in cands:
        if n % c == 0:
            return c
    return n


def _rope_tile(a, cos, sin_signed):
    lane = lax.broadcasted_iota(jnp.int32, (1, LANES), 1)
    first_half = (lane % 64) < 32
    pieces = []
    for gi in range(a.shape[1] // LANES):
        ag = a[:, gi * LANES:(gi + 1) * LANES]
        ahead = pltpu.roll(ag, LANES - 32, 1)
        behind = pltpu.roll(ag, 32, 1)
        rot = jnp.where(first_half, ahead, behind)
        pieces.append(ag * cos + rot * sin_signed)
    return pieces[0] if len(pieces) == 1 else jnp.concatenate(pieces, axis=1)


def _norm_matmul_body(*refs, rope, n_out, scale, head_dim):
    if rope:
        x_ref, g_ref, w_ref, cos_ref, sin_ref = refs[:5]
        rest = refs[5:]
    else:
        x_ref, g_ref, w_ref = refs[:3]
        rest = refs[3:]
    outs, xn_ref = rest[:n_out], rest[n_out]

    @pl.when(pl.program_id(1) == 0)
    def _():
        xf = x_ref[...]
        y = xf * lax.rsqrt(jnp.mean(xf * xf, axis=-1, keepdims=True) + EPS) * g_ref[...]
        xn_ref[...] = y.astype(BF16)

    acc = _dot(xn_ref[...], w_ref[...])
    if rope:
        acc = _rope_tile(acc, cos_ref[...], sin_ref[...])
    if scale != 1.0:
        acc = acc * scale
    for o in outs:
        if len(o.shape) == 3:
            o[0] = acc.T.astype(o.dtype)
        elif len(o.shape) == 4:
            npad = o.shape[3] - head_dim
            if npad:
                lane = lax.broadcasted_iota(jnp.int32, (acc.shape[0], npad), 1)
                pad = jnp.where(lane == 0, 1.0, 0.0)
            for hh in range(o.shape[1]):
                piece = acc[:, hh * head_dim:(hh + 1) * head_dim]
                if npad:
                    piece = jnp.concatenate([piece, pad], axis=1)
                o[0, hh] = piece.astype(o.dtype)
        else:
            o[...] = acc.astype(o.dtype)


def norm_matmul(x, g, w, out_dtypes, rope_tabs=None, tab_blocks=1, scale=1.0, head_major=None,
                row_tile=None):
    M, D = x.shape
    N = w.shape[1]
    tm = row_tile if row_tile is not None else _pick(M, ROW_TILES)
    tn = _pick(N, COL_TILES)
    out_specs = [pl.BlockSpec((tm, tn), lambda i, j: (i, j)) for _ in out_dtypes]
    out_shape = [jax.ShapeDtypeStruct((M, N), dt) for dt in out_dtypes]
    head_dim = 0
    if head_major is not None:
        hb, ht, head_dim, flags = head_major
        assert ht % tm == 0
        tpb = ht // tm
        for idx, width in enumerate(flags):
            if width == "T":
                out_specs[idx] = pl.BlockSpec((1, tn, tm), lambda i, j: (i // tpb, j, i % tpb))
                out_shape[idx] = jax.ShapeDtypeStruct((hb, N, ht), out_dtypes[idx])
            elif width:
                out_specs[idx] = pl.BlockSpec((1, tn // head_dim, tm, width),
                                              lambda i, j: (i // tpb, j, i % tpb, 0))
                out_shape[idx] = jax.ShapeDtypeStruct((hb, N // head_dim, ht, width), out_dtypes[idx])
    in_specs = [pl.BlockSpec((tm, D), lambda i, j: (i, 0)),
                pl.BlockSpec((1, D), lambda i, j: (0, 0)),
                pl.BlockSpec((D, tn), lambda i, j: (0, j))]
    args = [x, g.reshape(1, D), w]
    if rope_tabs is not None:
        for t in rope_tabs:
            in_specs.append(pl.BlockSpec((tm, LANES), lambda i, j: (i % tab_blocks, 0)))
            args.append(t)
    return pl.pallas_call(
        functools.partial(_norm_matmul_body, rope=rope_tabs is not None, n_out=len(out_dtypes),
                          scale=scale, head_dim=head_dim),
        grid=(M // tm, N // tn),
        in_specs=in_specs,
        out_specs=out_specs,
        out_shape=out_shape,
        scratch_shapes=[pltpu.VMEM((tm, D), BF16)],
        compiler_params=_params("parallel", "arbitrary"),
        name="norm_matmul",
    )(*args)


def _matmul_body(*refs, has_bias, act):
    if has_bias:
        x_ref, w_ref, b_ref, o_ref = refs
    else:
        x_ref, w_ref, o_ref = refs
    acc = _dot(x_ref[...], w_ref[...])
    if has_bias:
        acc = acc + b_ref[...]
    if act == "sigmoid":
        acc = _sigmoid(acc)
    o_ref[...] = acc.astype(o_ref.dtype)


def matmul(x, w, out_dtype, bias=None, act=None):
    M, K = x.shape
    N = w.shape[1]
    tm = _pick(M, ROW_TILES)
    tn = _pick(N, COL_TILES)
    in_specs = [pl.BlockSpec((tm, K), lambda i, j: (i, 0)),
                pl.BlockSpec((K, tn), lambda i, j: (0, j))]
    args = [x, w]
    if bias is not None:
        in_specs.append(pl.BlockSpec((1, tn), lambda i, j: (0, j)))
        args.append(bias.reshape(1, N).astype(F32))
    return pl.pallas_call(
        functools.partial(_matmul_body, has_bias=bias is not None, act=act),
        grid=(M // tm, N // tn),
        in_specs=in_specs,
        out_specs=pl.BlockSpec((tm, tn), lambda i, j: (i, j)),
        out_shape=jax.ShapeDtypeStruct((M, N), out_dtype),
        compiler_params=_params("parallel", "parallel"),
        name="matmul",
    )(*args)


CONV_HALO = 8
POOL_HALO = 16


def _conv_body(xb_ref, st_ref, w_ref, b_ref, gwh_ref, gwl_ref, gb_ref, xc_ref, gate_ref, ext_ref,
               *, tb, n_li):
    c = pl.program_id(1)
    nst = CONV_W - 1
    lo = CONV_HALO - nst

    @pl.when(c == 0)
    def _():
        ext_ref[:, lo:CONV_HALO, :] = st_ref[...]

    @pl.when(c > 0)
    def _():
        ext_ref[:, lo:CONV_HALO, :] = ext_ref[:, lo + tb:CONV_HALO + tb, :]

    ext_ref[:, CONV_HALO:CONV_HALO + tb, :] = xb_ref[...].astype(F32)
    acc = b_ref[...][None] + ext_ref[:, lo:lo + tb, :] * w_ref[0:1, :][None]
    for j in range(1, CONV_W):
        acc = acc + ext_ref[:, lo + j:lo + j + tb, :] * w_ref[j:j + 1, :][None]
    xc = _silu(acc)
    xc_ref[...] = xc.astype(xc_ref.dtype)
    x2 = xc.reshape(xc.shape[0] * tb, xc.shape[2])
    hi = x2.astype(BF16)
    lo = (x2 - hi.astype(F32)).astype(BF16)
    g = _dot(hi, gwh_ref[...]) + (_dot(hi, gwl_ref[...]) + _dot(lo, gwh_ref[...])) + gb_ref[...]
    col = lax.broadcasted_iota(jnp.int32, g.shape, 1)
    g = jnp.where(col < n_li, g, _log_sigmoid(g))
    gate_ref[...] = g.reshape(gate_ref.shape)


def conv_silu_gates(xb, state, w, b, gate_w, gate_b, n_li):
    B, T, E = xb.shape
    bb, tb = (1, 256) if T % 256 == 0 else (B, T)
    gw_hi = gate_w.astype(BF16)
    gw_lo = (gate_w - gw_hi.astype(F32)).astype(BF16)
    fixed = lambda i, c: (0, 0)
    return pl.pallas_call(
        functools.partial(_conv_body, tb=tb, n_li=n_li),
        grid=(B // bb, T // tb),
        in_specs=[pl.BlockSpec((bb, tb, E), lambda i, c: (i, c, 0)),
                  pl.BlockSpec((bb, CONV_W - 1, E), lambda i, c: (i, 0, 0)),
                  pl.BlockSpec((CONV_W, E), fixed),
                  pl.BlockSpec((1, E), fixed),
                  pl.BlockSpec((E, LANES), fixed),
                  pl.BlockSpec((E, LANES), fixed),
                  pl.BlockSpec((1, LANES), fixed)],
        out_specs=[pl.BlockSpec((bb, tb, E), lambda i, c: (i, c, 0)),
                   pl.BlockSpec((bb, tb, LANES), lambda i, c: (i, c, 0))],
        out_shape=[jax.ShapeDtypeStruct((B, T, E), BF16),
                   jax.ShapeDtypeStruct((B, T, LANES), F32)],
        scratch_shapes=[pltpu.VMEM((bb, CONV_HALO + tb, E), F32)],
        compiler_params=_params("parallel", "arbitrary"),
        name="conv_silu_gates",
    )(xb, state.astype(F32), w, b.reshape(1, E), gw_hi, gw_lo, gate_b.reshape(1, LANES))


def _mlstm_body(q_ref, k_ref, v_ref, gc_ref, gr_ref, c0_ref, n0_ref, m0_ref, ng_ref,
                h_ref, c_out_ref, n_out_ref, m_out_ref, c_s, n_s, m_s, *, nh, dqk, dv, lc):
    c = pl.program_id(1)

    @pl.when(c == 0)
    def _():
        c_s[...] = c0_ref[0]
        n_s[...] = n0_ref[0]
        m_s[...] = m0_ref[0]

    row = lax.broadcasted_iota(jnp.int32, (lc, lc), 0)
    col = lax.broadcasted_iota(jnp.int32, (lc, lc), 1)
    causal = row >= col
    scale = dqk ** -0.5
    gc = gc_ref[0]
    gr = gr_ref[0]
    for h in range(nh):
        q = (q_ref[0, :, h * dqk:(h + 1) * dqk].astype(F32) * scale).astype(BF16)
        k = k_ref[0, :, h * dqk:(h + 1) * dqk]
        v = v_ref[0, :, h * dv:(h + 1) * dv]
        li_c, lf_c = gc[:, h:h + 1], gc[:, nh + h:nh + h + 1]
        li_r, lf_r = gr[h:h + 1, :], gr[nh + h:nh + h + 1, :]
        b_c = jnp.sum(jnp.where(causal, lf_r, 0.0), axis=1, keepdims=True)
        b_r = jnp.sum(jnp.where(row <= col, lf_c, 0.0), axis=0, keepdims=True)
        m_prev = m_s[h:h + 1, 0:1]
        dm = jnp.where(causal, b_c - b_r + li_r, NEG)
        m_inter = b_c + m_prev
        m_t = jnp.maximum(m_inter, jnp.max(dm, axis=1, keepdims=True))
        a = _dot_nt(q, k) * jnp.exp(dm - m_t)
        inter = jnp.exp(m_inter - m_t)
        c_prev = c_s[h]
        n_prev = n_s[h:h + 1, :]
        num = _dot(a.astype(BF16), v) + inter * _dot(q, c_prev.astype(BF16))
        qn = jnp.sum(q.astype(F32) * n_prev, axis=1, keepdims=True)
        den = jnp.sum(a, axis=1, keepdims=True) + inter * qn
        hh = num / jnp.maximum(jnp.abs(den), jnp.exp(-m_t))
        hn = hh * lax.rsqrt(jnp.mean(hh * hh, axis=-1, keepdims=True) + EPS) \
            * ng_ref[:, h * dv:(h + 1) * dv]
        h_ref[0, :, h * dv:(h + 1) * dv] = hn.astype(h_ref.dtype)
        b_last = b_c[lc - 1:lc, :]
        g_r = b_last - b_r + li_r
        g_c = b_last - b_c + li_c
        m_new = jnp.maximum(b_last + m_prev, jnp.max(g_r, axis=1, keepdims=True))
        w_r = jnp.exp(g_r - m_new)
        w_c = jnp.exp(g_c - m_new)
        decay = jnp.exp(b_last + m_prev - m_new)
        wv = (w_c * v.astype(F32)).astype(BF16)
        c_s[h] = decay * c_prev + _dot_tn(k, wv)
        wr8 = jnp.broadcast_to(w_r, (SUBLANES, lc)).astype(BF16)
        n_s[h:h + 1, :] = decay * n_prev + _dot(wr8, k)[0:1, :]
        m_s[h:h + 1, :] = jnp.broadcast_to(m_new, (1, LANES))

    @pl.when(c == pl.num_programs(1) - 1)
    def _():
        c_out_ref[0] = c_s[...]
        n_out_ref[0] = n_s[...]
        m_out_ref[0] = m_s[...]


def mlstm_recurrence(q, k, v, gcol, grow, c0, n0, m0, norm_g):
    B, T, _ = q.shape
    nh, dqk, dv = c0.shape[1], c0.shape[2], c0.shape[3]
    lc = min(MLSTM_CHUNK, T)
    m0b = jnp.broadcast_to(m0[:, :, None], (B, nh, LANES)).astype(F32)
    h, c_new, n_new, m_new = pl.pallas_call(
        functools.partial(_mlstm_body, nh=nh, dqk=dqk, dv=dv, lc=lc),
        grid=(B, T // lc),
        in_specs=[pl.BlockSpec((1, lc, nh * dqk), lambda b, c: (b, c, 0)),
                  pl.BlockSpec((1, lc, nh * dqk), lambda b, c: (b, c, 0)),
                  pl.BlockSpec((1, lc, nh * dv), lambda b, c: (b, c, 0)),
                  pl.BlockSpec((1, lc, LANES), lambda b, c: (b, c, 0)),
                  pl.BlockSpec((1, 2 * nh, lc), lambda b, c: (b, 0, c)),
                  pl.BlockSpec((1, nh, dqk, dv), lambda b, c: (b, 0, 0, 0)),
                  pl.BlockSpec((1, nh, dqk), lambda b, c: (b, 0, 0)),
                  pl.BlockSpec((1, nh, LANES), lambda b, c: (b, 0, 0)),
                  pl.BlockSpec((1, nh * dv), lambda b, c: (0, 0))],
        out_specs=[pl.BlockSpec((1, lc, nh * dv), lambda b, c: (b, c, 0)),
                   pl.BlockSpec((1, nh, dqk, dv), lambda b, c: (b, 0, 0, 0)),
                   pl.BlockSpec((1, nh, dqk), lambda b, c: (b, 0, 0)),
                   pl.BlockSpec((1, nh, LANES), lambda b, c: (b, 0, 0))],
        out_shape=[jax.ShapeDtypeStruct((B, T, nh * dv), BF16),
                   jax.ShapeDtypeStruct((B, nh, dqk, dv), F32),
                   jax.ShapeDtypeStruct((B, nh, dqk), F32),
                   jax.ShapeDtypeStruct((B, nh, LANES), F32)],
        scratch_shapes=[pltpu.VMEM((nh, dqk, dv), F32),
                        pltpu.VMEM((nh, dqk), F32),
                        pltpu.VMEM((nh, LANES), F32)],
        compiler_params=_params("parallel", "arbitrary"),
        name="mlstm_recurrence",
    )(q, k, v, gcol, grow, c0.astype(F32), n0.astype(F32), m0b, norm_g.reshape(1, nh * dv))
    return h, c_new, n_new, m_new[:, :, 0]


def _pool_body(xb_ref, st_ref, wg_ref, sc_ref, y_ref, ext_ref, *, tb, pos0, nbuf):
    c = pl.program_id(1)
    lo = POOL_HALO - nbuf
    bb = xb_ref.shape[0]
    gw = wg_ref.shape[1]

    @pl.when(c == 0)
    def _():
        ext_ref[:, lo:POOL_HALO, :] = st_ref[...]

    @pl.when(c > 0)
    def _():
        ext_ref[:, lo:POOL_HALO, :] = ext_ref[:, lo + tb:POOL_HALO + tb, :]

    ext_ref[:, POOL_HALO:POOL_HALO + tb, :] = xb_ref[...].astype(F32)
    pos = pos0 + c * tb + lax.broadcasted_iota(jnp.int32, (1, tb, 1), 1)
    for g, w in enumerate(POOL_WINDOWS):
        cols = slice(g * gw, (g + 1) * gw)
        cur = ext_ref[:, POOL_HALO:POOL_HALO + tb, cols]
        win = cur
        for i in range(1, w):
            win = win + ext_ref[:, POOL_HALO - i:POOL_HALO - i + tb, cols]
        cnt = jnp.minimum(w, pos + 1).astype(F32)
        d = (win / cnt - cur).reshape(bb * tb, gw).astype(BF16)
        yg = _dot(d, wg_ref[g]) * sc_ref[:, cols]
        y_ref[:, :, cols] = yg.reshape(bb, tb, gw).astype(y_ref.dtype)


def pool_mix(xb, state, w_grp, scale, pos0):
    B, T, E = xb.shape
    nbuf = state.shape[1]
    bb, tb = (1, 256) if T % 256 == 0 else (B, T)
    ng, gw = w_grp.shape[0], w_grp.shape[1]
    return pl.pallas_call(
        functools.partial(_pool_body, tb=tb, pos0=pos0, nbuf=nbuf),
        grid=(B // bb, T // tb),
        in_specs=[pl.BlockSpec((bb, tb, E), lambda i, c: (i, c, 0)),
                  pl.BlockSpec((bb, nbuf, E), lambda i, c: (i, 0, 0)),
                  pl.BlockSpec((ng, gw, gw), lambda i, c: (0, 0, 0)),
                  pl.BlockSpec((1, E), lambda i, c: (0, 0))],
        out_specs=pl.BlockSpec((bb, tb, E), lambda i, c: (i, c, 0)),
        out_shape=jax.ShapeDtypeStruct((B, T, E), BF16),
        scratch_shapes=[pltpu.VMEM((bb, POOL_HALO + tb, E), F32)],
        compiler_params=_params("parallel", "arbitrary"),
        name="pool_mix",
    )(xb, state.astype(F32), w_grp, scale.reshape(1, E))


def _tail_body(*refs, kind, final):
    if kind == "mlstm":
        hn_ref, o_ref, xc_ref, z_ref, skip_ref = refs[:5]
        rest = refs[5:]
        mix = (o_ref[...].astype(F32) * hn_ref[...].astype(F32)
               + skip_ref[...] * xc_ref[...].astype(F32)) * _silu(z_ref[...].astype(F32))
    else:
        y_ref, z_ref = refs[:2]
        rest = refs[2:]
        mix = y_ref[...].astype(F32) * _silu(z_ref[...].astype(F32))
    x_ref, p_ref, wd_ref, gw_ref, pw_ref = rest[:5]
    rest = rest[5:]
    x1 = x_ref[...] + _dot(mix.astype(BF16), wd_ref[...])
    gate = _sigmoid(_dot(x1.astype(BF16), gw_ref[...]))
    x2 = x1 + gate * _dot(p_ref[...].astype(BF16), pw_ref[...])
    if final:
        fg_ref, xo_ref, yo_ref = rest
        yo_ref[...] = x2 * lax.rsqrt(jnp.mean(x2 * x2, axis=-1, keepdims=True) + EPS) * fg_ref[...]
    else:
        (xo_ref,) = rest
    xo_ref[...] = x2


def layer_tail(kind, mix_inputs, x, p, w_down, gate_w, ple_w, skip=None, final_g=None):
    M, D = x.shape
    tm = _pick(M, (512, 256))
    row = lambda i: (i, 0)
    fixed = lambda i: (0, 0)
    in_specs, args = [], []
    for a in mix_inputs:
        in_specs.append(pl.BlockSpec((tm, a.shape[1]), row))
        args.append(a)
    if kind == "mlstm":
        in_specs.append(pl.BlockSpec((1, skip.shape[-1]), fixed))
        args.append(skip.reshape(1, -1))
    in_specs += [pl.BlockSpec((tm, D), row), pl.BlockSpec((tm, p.shape[1]), row),
                 pl.BlockSpec(w_down.shape, fixed), pl.BlockSpec(gate_w.shape, fixed),
                 pl.BlockSpec(ple_w.shape, fixed)]
    args += [x, p, w_down, gate_w, ple_w]
    out_specs = [pl.BlockSpec((tm, D), row)]
    out_shape = [jax.ShapeDtypeStruct((M, D), F32)]
    if final_g is not None:
        in_specs.append(pl.BlockSpec((1, D), fixed))
        args.append(final_g.reshape(1, D))
        out_specs.append(pl.BlockSpec((tm, D), row))
        out_shape.append(jax.ShapeDtypeStruct((M, D), F32))
    return pl.pallas_call(
        functools.partial(_tail_body, kind=kind, final=final_g is not None),
        grid=(M // tm,),
        in_specs=in_specs, out_specs=out_specs, out_shape=out_shape,
        compiler_params=_params("parallel"),
        name="layer_tail_" + kind,
    )(*args)


def _sortable_key(s):
    s = jnp.where(s == 0.0, 0.0, s)
    bits = lax.bitcast_convert_type(s, jnp.int32)
    return jnp.where(bits < 0, bits ^ jnp.int32(0x7FFFFFFF), bits)


COUNT_STRIP_VREGS = 32


def _count(keys_ref, nvalid, pred):
    rows, cw = keys_ref.shape[1], keys_ref.shape[2]
    strip = max(SUBLANES, min(rows, COUNT_STRIP_VREGS * SUBLANES * LANES // cw))

    def body(kc, acc):
        parts = []
        for r0 in range(0, rows, strip):
            rs = slice(r0, min(r0 + strip, rows))
            hit = jnp.where(pred(keys_ref[kc, rs, :], kc, rs), 1.0, 0.0)
            part = hit[:, 0:LANES]
            for g in range(1, cw // LANES):
                part = part + hit[:, g * LANES:(g + 1) * LANES]
            parts.append(part)
        return acc + (parts[0] if len(parts) == 1 else jnp.concatenate(parts, axis=0))

    acc = lax.fori_loop(0, nvalid, body, jnp.zeros((rows, LANES), F32))
    return jnp.sum(acc, axis=1, keepdims=True)


def _select_topk(keys_ref, cidx_ref, nvalid, k_top, idx_bits):
    rows, cw = keys_ref.shape[1], keys_ref.shape[2]
    kf = float(k_top)
    imin = jnp.int32(INT_MIN)

    def bit_body(i, tau_u):
        cand_u = tau_u | jnp.left_shift(jnp.int32(1), 31 - i)
        cand = cand_u ^ imin
        cnt = _count(keys_ref, nvalid, lambda kk, kc, rs: kk >= cand[rs])
        return jnp.where(cnt >= kf, cand_u, tau_u)

    tau_u = lax.fori_loop(0, 32, bit_body, jnp.zeros((rows, 1), jnp.int32))
    tau = tau_u ^ imin
    n_gt = _count(keys_ref, nvalid, lambda kk, kc, rs: kk > tau[rs])
    n_ge = _count(keys_ref, nvalid, lambda kk, kc, rs: kk >= tau[rs])
    need = kf - n_gt
    short = tau_u == 0
    cidx_ref[...] = jnp.broadcast_to(jnp.where(short, -1, 2 ** 30), cidx_ref.shape)
    excess = jnp.max(jnp.where((n_ge > kf) & jnp.logical_not(short), 1.0, 0.0))

    @pl.when(excess > 0.5)
    def _():
        lane = lax.broadcasted_iota(jnp.int32, (1, cw), 1)

        def idx_body(i, cut):
            cand = cut | jnp.left_shift(jnp.int32(1), idx_bits - 1 - i)
            cnt = _count(keys_ref, nvalid,
                         lambda kk, kc, rs: (kk == tau[rs]) & ((kc * cw + lane) < cand[rs]))
            return jnp.where(cnt < need, cand, cut)

        cut = lax.fori_loop(0, idx_bits, idx_body, jnp.zeros((rows, 1), jnp.int32))
        cidx_ref[...] = jnp.broadcast_to(jnp.where(short, -1, cut), cidx_ref.shape)

    return tau


def _selected(kk, colv, tau, cut):
    return (kk > tau) | ((kk == tau) & (colv <= cut))


def _n_causal_chunks(qb, tq, tk):
    return lax.div(qb * tq + (tq - 1), jnp.int32(tk)) + 1


def _idx_prompt_body(qi_ref, wi_ref, ki_ref, bias_ref, keys_ref, cidx_ref, *,
                     nh, dh, tq, tk, nk, k_top, idx_bits):
    per = keys_ref.shape[2] // tk
    gw = per * tk
    qb = pl.program_id(1)
    nvalid = _n_causal_chunks(qb, tq, gw)
    rowpos = qb * tq + lax.broadcasted_iota(jnp.int32, (tq, 1), 0)
    lane = lax.broadcasted_iota(jnp.int32, (1, tk), 1)
    wi = wi_ref[0]

    def score_group(gc, carry):
        for part in range(per):
            kt = ki_ref[0, gc * per + part]
            s = jnp.zeros((tq, tk), F32)
            for h in range(nh):
                rel = jnp.maximum(_dot_nt(qi_ref[0, :, h * dh:(h + 1) * dh], kt), 0.0)
                s = s + rel * wi[:, h:h + 1]
            colv = gc * gw + part * tk + lane
            keys_ref[gc, :, part * tk:(part + 1) * tk] = jnp.where(
                colv <= rowpos, _sortable_key(s), jnp.int32(INT_MIN))
        return carry

    lax.fori_loop(0, nvalid, score_group, 0)
    tau = _select_topk(keys_ref, cidx_ref, nvalid, k_top, idx_bits)
    cut = cidx_ref[:, 0:1]

    def write_group(gc, carry):
        for part in range(per):
            kk = keys_ref[gc, :, part * tk:(part + 1) * tk]
            sel = _selected(kk, gc * gw + part * tk + lane, tau, cut)
            bias_ref[0, 0, gc * per + part] = jnp.where(sel, 0.0, NEG).astype(bias_ref.dtype)
        return carry

    lax.fori_loop(0, nvalid, write_group, 0)

    def fill_chunk(kc, carry):
        bias_ref[0, 0, kc] = jnp.full((tq, tk), NEG, bias_ref.dtype)
        return carry

    lax.fori_loop(nvalid * per, nk, fill_chunk, 0)


def indexer_bias_prompt(qi, wi, ki, k_top):
    B, T, _ = qi.shape
    dh = ki.shape[-1]
    nh = qi.shape[-1] // dh
    tq, tk = min(IDX_TQ, T), min(ATT_TK, T)
    tqa = min(ATT_TQ, T)
    sub = tqa // tq
    nq, nk = T // tq, T // tk
    per = 2 if nk % 2 == 0 else 1
    idx_bits = int(T).bit_length() + 1
    return pl.pallas_call(
        functools.partial(_idx_prompt_body, nh=nh, dh=dh, tq=tq, tk=tk, nk=nk, k_top=k_top,
                          idx_bits=idx_bits),
        grid=(B, nq),
        in_specs=[pl.BlockSpec((1, tq, nh * dh), lambda b, i: (b, i, 0)),
                  pl.BlockSpec((1, tq, LANES), lambda b, i: (b, i, 0)),
                  pl.BlockSpec((1, nk, tk, dh), lambda b, i: (b, 0, 0, 0))],
        out_specs=pl.BlockSpec((1, 1, nk, tq, tk), lambda b, i: (b, i // sub, 0, i % sub, 0)),
        out_shape=jax.ShapeDtypeStruct((B, T // tqa, nk, tqa, tk), BF16),
        scratch_shapes=[pltpu.VMEM((nk // per, tq, per * tk), jnp.int32),
                        pltpu.VMEM((tq, LANES), jnp.int32)],
        compiler_params=_params("parallel", "arbitrary"),
        name="indexer_bias_prompt",
    )(qi, wi, ki.reshape(B, nk, tk, dh))


def _attn_prompt_body(q_ref, k_ref, v_ref, bias_ref, o_ref, m_s, acc_s, *, nh, dh, tq, tk):
    qb, kb = pl.program_id(1), pl.program_id(2)
    nvalid = _n_causal_chunks(qb, tq, tk)

    @pl.when(kb == 0)
    def _():
        m_s[...] = jnp.full(m_s.shape, NEG, F32)
        acc_s[...] = jnp.zeros(acc_s.shape, F32)

    @pl.when(kb < nvalid)
    def _():
        bias = bias_ref[0, 0, 0].astype(F32)
        for h in range(nh):
            s = _dot_nt(q_ref[0, h], k_ref[0, h]) + bias
            m_prev = m_s[h]
            m_new = jnp.maximum(m_prev, jnp.max(s, axis=1, keepdims=True))
            alpha = jnp.exp2(m_prev - m_new)
            p = jnp.exp2(s - jnp.concatenate([m_new] * (tk // LANES), axis=1))
            acc_s[h] = alpha * acc_s[h] + _dot(p.astype(BF16), v_ref[0, h])
            m_s[h] = m_new

    @pl.when(kb == pl.num_programs(2) - 1)
    def _():
        for h in range(nh):
            acc = acc_s[h]
            o_ref[0, :, h * dh:(h + 1) * dh] = (acc[:, :dh] / acc[:, dh:dh + 1]).astype(o_ref.dtype)


def attention_prompt(q, k, v, bias):
    B, nh, T, dh = q.shape
    assert tuple(v.shape) == (B, nh, T, LANES) and dh < LANES
    _, nq, nk, tq, tk = bias.shape

    def kv_map(b, i, j):
        return (b, 0, jnp.minimum(j, _n_causal_chunks(i, tq, tk) - 1), 0)

    def bias_map(b, i, j):
        return (b, i, jnp.minimum(j, _n_causal_chunks(i, tq, tk) - 1), 0, 0)

    return pl.pallas_call(
        functools.partial(_attn_prompt_body, nh=nh, dh=dh, tq=tq, tk=tk),
        grid=(B, nq, nk),
        in_specs=[pl.BlockSpec((1, nh, tq, dh), lambda b, i, j: (b, 0, i, 0)),
                  pl.BlockSpec((1, nh, tk, dh), kv_map),
                  pl.BlockSpec((1, nh, tk, LANES), kv_map),
                  pl.BlockSpec((1, 1, 1, tq, tk), bias_map)],
        out_specs=pl.BlockSpec((1, tq, nh * dh), lambda b, i, j: (b, i, 0)),
        out_shape=jax.ShapeDtypeStruct((B, T, nh * dh), BF16),
        scratch_shapes=[pltpu.VMEM((nh, tq, LANES), F32),
                        pltpu.VMEM((nh, tq, LANES), F32)],
        compiler_params=_params("parallel", "parallel", "arbitrary"),
        name="attention_prompt",
    )(q, k, v, bias)


def _idx_sample_body(pt_ref, qi_ref, wi_ref, knew_ref, *rest, nh, ts, npg, nsteps, k_top,
                     idx_bits):
    page_refs = rest[:npg]
    bias_ref, keys_ref, cidx_ref = rest[npg:]
    s = pl.program_id(1)
    cw = keys_ref.shape[2]
    wi = wi_ref[0]
    qi = qi_ref[0]

    def scores(kt):
        full = jnp.maximum(_dot(qi, kt), 0.0)
        out = jnp.zeros((ts, cw), F32)
        for h in range(nh):
            out = out + full[h * ts:(h + 1) * ts, :] * wi[:, h:h + 1]
        return out

    kt = jnp.concatenate([r[...] for r in page_refs], axis=1).astype(BF16)
    keys_ref[s] = _sortable_key(scores(kt))

    @pl.when(s == nsteps - 1)
    def _():
        lane = lax.broadcasted_iota(jnp.int32, (1, cw), 1)
        past = nsteps * cw
        rowpos = past + lax.broadcasted_iota(jnp.int32, (ts, 1), 0)
        valid = (past + lane) <= rowpos
        keys_ref[nsteps] = jnp.where(valid, _sortable_key(scores(knew_ref[0])),
                                     jnp.int32(INT_MIN))
        tau = _select_topk(keys_ref, cidx_ref, nsteps + 1, k_top, idx_bits)
        cut = cidx_ref[:, 0:1]
        for kc in range(nsteps + 1):
            sel = _selected(keys_ref[kc], kc * cw + lane, tau, cut)
            bias_ref[0, kc] = jnp.where(sel, 0.0, NEG).astype(bias_ref.dtype)


def _page_specs(n_pages_per_step, block, layer, first_step):
    specs = []
    for i in range(n_pages_per_step):
        def imap(b, s, pt, i=i):
            step = jnp.maximum(s - first_step, 0)
            return (layer, pt[b, step * n_pages_per_step + i]) + (0,) * len(block)
        specs.append(pl.BlockSpec((None, None) + tuple(block), imap))
    return specs


def indexer_bias_sample(qi_hq, wi, ki_new_t, cache_kidx_t, layer, page_table, k_top):
    B, n_pages = page_table.shape
    _, _, dh, page = cache_kidx_t.shape
    ts = wi.shape[1]
    nh = qi_hq.shape[1] // ts
    npg = min(IDX_PAGES_PER_STEP, n_pages)
    nsteps = n_pages // npg
    cw = npg * page
    idx_bits = int(n_pages * page + cw).bit_length() + 1
    grid_spec = pltpu.PrefetchScalarGridSpec(
        num_scalar_prefetch=1,
        grid=(B, nsteps),
        in_specs=[pl.BlockSpec((1, nh * ts, dh), lambda b, s, pt: (b, 0, 0)),
                  pl.BlockSpec((1, ts, LANES), lambda b, s, pt: (b, 0, 0)),
                  pl.BlockSpec((1, dh, cw), lambda b, s, pt: (b, 0, 0))]
        + _page_specs(npg, (dh, page), layer, 0),
        out_specs=pl.BlockSpec((1, nsteps + 1, ts, cw), lambda b, s, pt: (b, 0, 0, 0)),
        scratch_shapes=[pltpu.VMEM((nsteps + 1, ts, cw), jnp.int32),
                        pltpu.VMEM((ts, LANES), jnp.int32)],
    )
    return pl.pallas_call(
        functools.partial(_idx_sample_body, nh=nh, ts=ts, npg=npg, nsteps=nsteps, k_top=k_top,
                          idx_bits=idx_bits),
        grid_spec=grid_spec,
        out_shape=jax.ShapeDtypeStruct((B, nsteps + 1, ts, cw), F32),
        compiler_params=_params("parallel", "arbitrary"),
        name="indexer_bias_sample",
    )(page_table, qi_hq, wi, ki_new_t, *([cache_kidx_t] * npg))


def _attn_sample_body(pt_ref, qbd_ref, knew_ref, vnew_ref, bias_ref, *rest, nh, ts, npg, dh):
    kpages, vpages = rest[:npg], rest[npg:2 * npg]
    o_ref, m_s, l_s, acc_s = rest[2 * npg:]
    s = pl.program_id(1)
    rows = nh * ts

    @pl.when(s == 0)
    def _():
        m_s[...] = jnp.full(m_s.shape, NEG, F32)
        l_s[...] = jnp.zeros(l_s.shape, F32)
        acc_s[...] = jnp.zeros(acc_s.shape, F32)

    def update(kt, vt):
        n = kt.shape[1]
        bias = bias_ref[0, 0][:, :n]
        logits = _dot(qbd_ref[0], kt) + jnp.concatenate([bias] * nh, axis=0)
        m_prev = m_s[...]
        m_new = jnp.maximum(m_prev, jnp.max(logits, axis=1, keepdims=True))
        alpha = jnp.exp2(m_prev - m_new)
        p = jnp.exp2(logits - m_new)
        l_s[...] = alpha * l_s[...] + jnp.sum(p, axis=1, keepdims=True)
        acc_s[...] = alpha * acc_s[...] + _dot_nt(p.astype(BF16), vt)
        m_s[...] = m_new

    @pl.when(s == 0)
    def _():
        update(knew_ref[0], vnew_ref[0])

    def gather_pages(page_refs):
        return jnp.concatenate([r[...].reshape(nh * dh, r.shape[2]) for r in page_refs],
                               axis=1).astype(BF16)

    @pl.when(s > 0)
    def _():
        update(gather_pages(kpages), gather_pages(vpages))

    @pl.when(s == pl.num_programs(1) - 1)
    def _():
        o = acc_s[...] / l_s[...]
        lane = lax.broadcasted_iota(jnp.int32, (1, nh * dh), 1)
        res = jnp.zeros((ts, nh * dh), F32)
        for h in range(nh):
            own = (lane >= h * dh) & (lane < (h + 1) * dh)
            res = res + jnp.where(own, o[h * ts:(h + 1) * ts, :], 0.0)
        o_ref[0] = res.astype(o_ref.dtype)


def attention_sample(qbd, k_new_t, v_new_t, bias, cache_k_t, cache_v_t, layer, page_table, ts):
    B, n_pages = page_table.shape
    _, _, nh, dh, page = cache_k_t.shape
    width = nh * dh
    n_new = k_new_t.shape[2]
    npg = min(PAGES_PER_STEP, n_pages)
    cw = npg * page
    nsteps1 = n_pages // npg + 1
    per_chunk = bias.shape[3] // cw
    last_chunk = bias.shape[1] - 1
    rows = nh * ts

    def bias_map(b, s, pt):
        g = jnp.maximum(s - 1, 0)
        return (b, jnp.where(s == 0, last_chunk, g // per_chunk), 0,
                jnp.where(s == 0, 0, g % per_chunk))

    grid_spec = pltpu.PrefetchScalarGridSpec(
        num_scalar_prefetch=1,
        grid=(B, nsteps1),
        in_specs=[pl.BlockSpec((1, rows, width), lambda b, s, pt: (b, 0, 0)),
                  pl.BlockSpec((1, width, n_new), lambda b, s, pt: (b, 0, 0)),
                  pl.BlockSpec((1, width, n_new), lambda b, s, pt: (b, 0, 0)),
                  pl.BlockSpec((1, 1, ts, cw), bias_map)]
        + _page_specs(npg, (nh, dh, page), layer, 1) + _page_specs(npg, (nh, dh, page), layer, 1),
        out_specs=pl.BlockSpec((1, ts, width), lambda b, s, pt: (b, 0, 0)),
        scratch_shapes=[pltpu.VMEM((rows, 1), F32),
                        pltpu.VMEM((rows, 1), F32),
                        pltpu.VMEM((rows, width), F32)],
    )
    return pl.pallas_call(
        functools.partial(_attn_sample_body, nh=nh, ts=ts, npg=npg, dh=dh),
        grid_spec=grid_spec,
        out_shape=jax.ShapeDtypeStruct((B, ts, width), BF16),
        compiler_params=_params("parallel", "arbitrary"),
        name="attention_sample",
    )(page_table, qbd, k_new_t, v_new_t, bias, *([cache_k_t] * npg), *([cache_v_t] * npg))


def _pad_time(a, t_pad, value=0.0):
    return jnp.pad(a, ((0, 0), (0, t_pad - a.shape[1]), (0, 0)), constant_values=value)


def _mlstm_layer(x, g, B, T, state, w, j):
    c0, n0, m0, conv0 = state
    nh = c0.shape[1]
    E = w["a_conv_w"].shape[-1]
    w_up = w["a_w_up"][j]
    xb, = norm_matmul(x, g, w_up[:, :E], (BF16,))
    z, = norm_matmul(x, g, w_up[:, E:], (BF16,))
    xc, gcol = conv_silu_gates(xb.reshape(B, T, E), conv0, w["a_conv_w"][j], w["a_conv_b"][j],
                               w["a_w_if_pad"][j], w["a_b_if_pad"][j], nh)
    xc = xc.reshape(B * T, E)
    q = matmul(xc, w["a_w_q"][j], BF16)
    k = matmul(xc, w["a_w_k"][j], BF16)
    v = matmul(xb, w["a_w_v"][j], BF16)
    o = matmul(xb, w["a_w_o"][j], BF16, bias=w["a_b_o"][j], act="sigmoid")
    q3, k3, v3 = q.reshape(B, T, -1), k.reshape(B, T, -1), v.reshape(B, T, -1)
    tp = T if T % LANES == 0 else ((T + LANES - 1) // LANES) * LANES
    if tp != T:
        q3, k3, v3 = _pad_time(q3, tp), _pad_time(k3, tp), _pad_time(v3, tp)
        pad_row = jnp.where(jnp.arange(LANES) < nh, NEG, 0.0).astype(F32)
        gcol = jnp.concatenate([gcol, jnp.broadcast_to(pad_row, (B, tp - T, LANES))], axis=1)
    grow = jnp.swapaxes(gcol[:, :, :2 * nh], 1, 2)
    hn, c_new, n_new, m_new = mlstm_recurrence(q3, k3, v3, gcol, grow, c0, n0, m0, w["a_norm_g"][j])
    hn = hn[:, :T].reshape(B * T, -1)
    assert T >= CONV_W - 1
    conv_new = xb.reshape(B, T, E)[:, T - (CONV_W - 1):].astype(F32)
    return (hn, o, xc, z), (c_new, n_new, m_new, conv_new)


def _rope_tables(pos):
    half = 32
    inv = ROPE_THETA ** (-np.arange(half, dtype=np.float64) / half)
    ang = np.asarray(pos, np.float64)[:, None] * inv[None, :]
    cos = np.tile(np.cos(ang), (1, 4))
    sin = np.tile(np.concatenate([-np.sin(ang), np.sin(ang)], axis=1), (1, 2))
    return jnp.asarray(cos, F32), jnp.asarray(sin, F32)


def _dsa_project(x, g, B, T, pos, w, j, nh, dh, nhi, di, head_major=False):
    M = B * T
    cos, sin = _rope_tables(pos)
    tm = _pick(T, ROW_TILES) if T % ROW_TILES[-1] == 0 else _pick(M, ROW_TILES)
    if T % tm == 0:
        tabs, tab_blocks = (cos, sin), T // tm
    else:
        tabs, tab_blocks = (jnp.tile(cos, (M // T, 1)), jnp.tile(sin, (M // T, 1))), M // tm
    rope = dict(rope_tabs=tabs, tab_blocks=tab_blocks, row_tile=tm)
    hm = (lambda *widths: (B, T, dh, widths)) if head_major else (lambda *widths: None)
    q, = norm_matmul(x, g, w["b_wq"][j], (BF16,), scale=dh ** -0.5 * math.log2(math.e),
                     head_major=hm(dh), **rope)
    wide = "T" if head_major else 0
    k32, k16 = norm_matmul(x, g, w["b_wk"][j], (F32, BF16), head_major=hm(wide, dh), **rope)
    v32, v16 = norm_matmul(x, g, w["b_wv"][j], (F32, BF16), head_major=hm(wide, LANES),
                           row_tile=tm)
    gate, = norm_matmul(x, g, w["b_wg"][j], (BF16,))
    qi, = norm_matmul(x, g, w["b_wqi"][j], (BF16,), **rope)
    ki32, ki16 = norm_matmul(x, g, w["b_wki_pad"][j], (F32, BF16), head_major=hm(wide, 0), **rope)
    wi, = norm_matmul(x, g, w["b_wwi_pad"][j], (F32,), scale=nhi ** -0.5 * di ** -0.5)
    if head_major:
        new = (jnp.transpose(k32.reshape(B, nh, dh, T), (0, 3, 1, 2)),
               jnp.transpose(v32.reshape(B, nh, dh, T), (0, 3, 1, 2)),
               jnp.transpose(ki32[:, :di], (0, 2, 1)))
    else:
        new = (k32.reshape(B, T, nh, dh), v32.reshape(B, T, nh, dh), ki32[:, :di].reshape(B, T, di))
    return q, k16, v16, gate, qi, ki16[:, :di].reshape(B, T, di), wi, new


def _dsa_layer_prompt(x, g, B, T, w, j, dims):
    nh, dh, nhi, di = dims
    q, k16, v16, gate, qi, ki16, wi, new = _dsa_project(
        x, g, B, T, np.arange(T), w, j, *dims, head_major=True)
    k_top = min(TOPK_MAX, T // 4)
    bias = indexer_bias_prompt(qi.reshape(B, T, nhi * di), wi.reshape(B, T, LANES), ki16, k_top)
    attn = attention_prompt(q, k16, v16, bias).reshape(B * T, nh * dh)
    return (attn, gate), new


def _dsa_layer_sample(x, g, B, T, w, j, dims, ctx):
    nh, dh, nhi, di = dims
    cache_k, cache_v, cache_kidx, page_table = ctx
    n_pages = page_table.shape[1]
    page = cache_kidx.shape[2]
    past = n_pages * page
    q, k16, v16, gate, qi, ki16, wi, new = _dsa_project(
        x, g, B, T, past + np.arange(T), w, j, *dims)
    k_top = min(TOPK_MAX, (past + T) // 4)
    cw = min(IDX_PAGES_PER_STEP, n_pages) * page
    qi_hq = jnp.transpose(qi.reshape(B, T, nhi, di), (0, 2, 1, 3)).reshape(B, nhi * T, di)
    token_last = lambda a, n: jnp.swapaxes(_pad_time(a, n), 1, 2)
    ki_new_t = token_last(ki16, cw)
    bias = indexer_bias_sample(qi_hq, wi.reshape(B, T, LANES), ki_new_t,
                               jnp.swapaxes(cache_kidx, 2, 3), j, page_table, k_top)
    q3 = q.reshape(B, T, nh * dh)
    own = (jnp.arange(nh * T)[:, None] // T) == (jnp.arange(nh * dh)[None, :] // dh)
    qbd = jnp.where(own[None], jnp.tile(q3, (1, nh, 1)), 0).astype(BF16)
    width = nh * dh
    n_new = ((T + LANES - 1) // LANES) * LANES
    attn = attention_sample(qbd, token_last(k16.reshape(B, T, width), n_new),
                            token_last(v16.reshape(B, T, width), n_new), bias,
                            jnp.transpose(cache_k, (0, 1, 3, 4, 2)),
                            jnp.transpose(cache_v, (0, 1, 3, 4, 2)), j, page_table, T)
    return (attn.reshape(B * T, width), gate), new


def _pool_layer(x, g, B, T, buf, pos0, w, j):
    E = w["c_scale"].shape[-1]
    w_up = w["c_w_up"][j]
    xb, = norm_matmul(x, g, w_up[:, :E], (BF16,))
    z, = norm_matmul(x, g, w_up[:, E:], (BF16,))
    xb3 = xb.reshape(B, T, E)
    y = pool_mix(xb3, buf, w["c_w_grp"][j], w["c_scale"][j], pos0).reshape(B * T, E)
    nbuf = buf.shape[1]
    if T >= nbuf:
        new_buf = xb3[:, T - nbuf:].astype(F32)
    else:
        new_buf = jnp.concatenate([buf[:, T:].astype(F32), xb3.astype(F32)], axis=1)
    return (y, z), new_buf


def _run_group(x3, p4, pos0, mlstm_state, dsa_ctx, pool_state, w, dims):
    B, T, D = x3.shape
    depth = p4.shape[0]
    x = x3.reshape(B * T, D)
    new_a, new_b, new_c = [], [], []
    y = None
    for i in range(depth):
        kind, j = i % 3, i // 3
        g = w["norm_g"][i]
        final_g = w["final_g"] if i == depth - 1 else None
        p = p4[i].reshape(B * T, -1)
        if kind == 0:
            state = tuple(s[j] for s in mlstm_state)
            mix, st = _mlstm_layer(x, g, B, T, state, w, j)
            new_a.append(st)
            out = layer_tail("mlstm", mix, x, p, w["a_w_down"][j], w["ple_gate_w"][i], w["ple_w"][i],
                             skip=w["a_skip"][j], final_g=final_g)
        elif kind == 1:
            if dsa_ctx is None:
                mix, st = _dsa_layer_prompt(x, g, B, T, w, j, dims)
            else:
                mix, st = _dsa_layer_sample(x, g, B, T, w, j, dims, dsa_ctx)
            new_b.append(st)
            out = layer_tail("gated", mix, x, p, w["b_w_out"][j], w["ple_gate_w"][i], w["ple_w"][i],
                             final_g=final_g)
        else:
            mix, st = _pool_layer(x, g, B, T, pool_state[j], pos0, w, j)
            new_c.append(st)
            out = layer_tail("gated", mix, x, p, w["c_w_down"][j], w["ple_gate_w"][i], w["ple_w"][i],
                             final_g=final_g)
        x = out[0]
        if final_g is not None:
            y = out[1]
    a_states = tuple(jnp.stack([s[r] for s in new_a]) for r in range(4))
    b_states = tuple(jnp.stack([s[r] for s in new_b]) for r in range(3))
    c_state = jnp.stack(new_c)
    return y.reshape(B, T, D), a_states, b_states, c_state


def kernel(x_prompt, x_sample, state_mlstm_C, state_mlstm_n, state_mlstm_m, state_mlstm_conv, state_pool,
           cache_k, cache_v, cache_kidx, page_table, p_prompt, p_sample,
           norm_g, final_g, ple_w, ple_gate_w,
           a_w_up, a_conv_w, a_conv_b, a_w_q, a_w_k, a_w_v, a_w_if, a_b_if, a_w_o, a_b_o, a_norm_g,
           a_skip, a_w_down, b_w_in, b_w_out, c_w_up, c_w_grp, c_scale, c_w_down):
    D = x_prompt.shape[-1]
    nh, dh = cache_k.shape[3], cache_k.shape[4]
    di = cache_kidx.shape[-1]
    aw = nh * dh
    nhi = (b_w_in.shape[-1] - 4 * aw - di) // (di + 1)
    dims = (nh, dh, nhi, di)
    nha = state_mlstm_C.shape[2]
    bf = lambda a: a.astype(BF16)

    def pad_cols(a, n):
        return jnp.pad(a, [(0, 0)] * (a.ndim - 1) + [(0, n - a.shape[-1])])

    o1, o2, o3, o4, o5, o6 = np.cumsum([aw, aw, aw, aw, nhi * di, di])
    w = dict(
        norm_g=norm_g, final_g=final_g, ple_w=bf(ple_w), ple_gate_w=bf(ple_gate_w),
        a_w_up=bf(a_w_up), a_conv_w=a_conv_w, a_conv_b=a_conv_b,
        a_w_q=bf(a_w_q), a_w_k=bf(a_w_k), a_w_v=bf(a_w_v), a_w_o=bf(a_w_o), a_b_o=a_b_o,
        a_w_if_pad=pad_cols(a_w_if, LANES), a_b_if_pad=pad_cols(a_b_if, LANES),
        a_norm_g=a_norm_g, a_skip=a_skip, a_w_down=bf(a_w_down),
        b_wq=bf(b_w_in[..., :o1]), b_wk=bf(b_w_in[..., o1:o2]), b_wv=bf(b_w_in[..., o2:o3]),
        b_wg=bf(b_w_in[..., o3:o4]), b_wqi=bf(b_w_in[..., o4:o5]),
        b_wki_pad=bf(pad_cols(b_w_in[..., o5:o6], LANES)),
        b_wwi_pad=bf(pad_cols(b_w_in[..., o6:], LANES)),
        b_w_out=bf(b_w_out), c_w_up=bf(c_w_up), c_w_grp=bf(c_w_grp), c_scale=c_scale,
        c_w_down=bf(c_w_down),
    )

    Bp = x_prompt.shape[0]
    na, nc = state_mlstm_C.shape[0], state_pool.shape[0]
    zeros_a = (jnp.zeros((na, Bp) + state_mlstm_C.shape[2:], F32),
               jnp.zeros((na, Bp) + state_mlstm_n.shape[2:], F32),
               jnp.zeros((na, Bp) + state_mlstm_m.shape[2:], F32),
               jnp.zeros((na, Bp) + state_mlstm_conv.shape[2:], F32))
    zeros_c = jnp.zeros((nc, Bp) + state_pool.shape[2:], F32)
    y_p, a_p, b_p, c_p = _run_group(x_prompt, p_prompt, 0, zeros_a, None, zeros_c, w, dims)

    past = page_table.shape[1] * cache_k.shape[2]
    y_s, a_s, b_s, c_s = _run_group(
        x_sample, p_sample, past,
        (state_mlstm_C, state_mlstm_n, state_mlstm_m, state_mlstm_conv),
        (cache_k, cache_v, cache_kidx, page_table), state_pool, w, dims)
    return (y_p, y_s, *a_p, *b_p, c_p, *a_s, *b_s, c_s)
```

```python
import functools
import math

import numpy as np
import jax
import jax.numpy as jnp
from jax import lax
from jax.experimental import pallas as pl
from jax.experimental.pallas import tpu as pltpu

EPS = 1e-6
ROPE_THETA = 10000.0
TOPK_MAX = 256
POOL_WINDOWS = (2, 4, 8, 16)
CONV_W = 4
MLSTM_CHUNK = 256
NEG = -1e30
F32_BIG = 3.0e38
LANES = 128
SUBLANES = 8
VMEM_LIMIT_BYTES = 56 * 1024 * 1024
ATT_TQ, ATT_TK = 1024, 512
IDX_TQ = 256
PAGES_PER_STEP = 8
IDX_PAGES_PER_STEP = 32
ROW_TILES = (1024, 512, 256)
COL_TILES = (1024, 512, 256, 128)

F32 = jnp.float32
BF16 = jnp.bfloat16
INT_MIN = -2 ** 31


def _params(*sem):
    return pltpu.CompilerParams(dimension_semantics=sem, vmem_limit_bytes=VMEM_LIMIT_BYTES)


def _sigmoid(x):
    return 1.0 / (1.0 + jnp.exp(-x))


def _silu(x):
    return x * _sigmoid(x)


def _log_sigmoid(x):
    return jnp.minimum(x, 0.0) - jnp.log(1.0 + jnp.exp(-jnp.abs(x)))


def _dot(a, b):
    return jnp.dot(a, b, preferred_element_type=F32)


def _dot_nt(a, b):
    return lax.dot_general(a, b, (((1,), (1,)), ((), ())), preferred_element_type=F32)


def _dot_tn(a, b):
    return lax.dot_general(a, b, (((0,), (0,)), ((), ())), preferred_element_type=F32)


def _pick(n, cands):
    for c in cands:
        if n % c == 0:
            return c
    return n


def _rope_tile(a, cos, sin_signed):
    lane = lax.broadcasted_iota(jnp.int32, (1, LANES), 1)
    first_half = (lane % 64) < 32
    pieces = []
    for gi in range(a.shape[1] // LANES):
        ag = a[:, gi * LANES:(gi + 1) * LANES]
        ahead = pltpu.roll(ag, LANES - 32, 1)
        behind = pltpu.roll(ag, 32, 1)
        rot = jnp.where(first_half, ahead, behind)
        pieces.append(ag * cos + rot * sin_signed)
    return pieces[0] if len(pieces) == 1 else jnp.concatenate(pieces, axis=1)


def _norm_matmul_body(*refs, rope, n_out, scale, head_dim):
    if rope:
        x_ref, g_ref, w_ref, cos_ref, sin_ref = refs[:5]
        rest = refs[5:]
    else:
        x_ref, g_ref, w_ref = refs[:3]
        rest = refs[3:]
    outs, xn_ref = rest[:n_out], rest[n_out]

    @pl.when(pl.program_id(1) == 0)
    def _():
        xf = x_ref[...]
        y = xf * lax.rsqrt(jnp.mean(xf * xf, axis=-1, keepdims=True) + EPS) * g_ref[...]
        xn_ref[...] = y.astype(BF16)

    acc = _dot(xn_ref[...], w_ref[...])
    if rope:
        acc = _rope_tile(acc, cos_ref[...], sin_ref[...])
    if scale != 1.0:
        acc = acc * scale
    for o in outs:
        if len(o.shape) == 3:
            o[0] = acc.T.astype(o.dtype)
        elif len(o.shape) == 4:
            npad = o.shape[3] - head_dim
            if npad:
                lane = lax.broadcasted_iota(jnp.int32, (acc.shape[0], npad), 1)
                pad = jnp.where(lane == 0, 1.0, 0.0)
            for hh in range(o.shape[1]):
                piece = acc[:, hh * head_dim:(hh + 1) * head_dim]
                if npad:
                    piece = jnp.concatenate([piece, pad], axis=1)
                o[0, hh] = piece.astype(o.dtype)
        else:
            o[...] = acc.astype(o.dtype)


def norm_matmul(x, g, w, out_dtypes, rope_tabs=None, tab_blocks=1, scale=1.0, head_major=None,
                row_tile=None):
    M, D = x.shape
    N = w.shape[1]
    tm = row_tile if row_tile is not None else _pick(M, ROW_TILES)
    tn = _pick(N, COL_TILES)
    out_specs = [pl.BlockSpec((tm, tn), lambda i, j: (i, j)) for _ in out_dtypes]
    out_shape = [jax.ShapeDtypeStruct((M, N), dt) for dt in out_dtypes]
    head_dim = 0
    if head_major is not None:
        hb, ht, head_dim, flags = head_major
        assert ht % tm == 0
        tpb = ht // tm
        for idx, width in enumerate(flags):
            if width == "T":
                out_specs[idx] = pl.BlockSpec((1, tn, tm), lambda i, j: (i // tpb, j, i % tpb))
                out_shape[idx] = jax.ShapeDtypeStruct((hb, N, ht), out_dtypes[idx])
            elif width:
                out_specs[idx] = pl.BlockSpec((1, tn // head_dim, tm, width),
                                              lambda i, j: (i // tpb, j, i % tpb, 0))
                out_shape[idx] = jax.ShapeDtypeStruct((hb, N // head_dim, ht, width), out_dtypes[idx])
    in_specs = [pl.BlockSpec((tm, D), lambda i, j: (i, 0)),
                pl.BlockSpec((1, D), lambda i, j: (0, 0)),
                pl.BlockSpec((D, tn), lambda i, j: (0, j))]
    args = [x, g.reshape(1, D), w]
    if rope_tabs is not None:
        for t in rope_tabs:
            in_specs.append(pl.BlockSpec((tm, LANES), lambda i, j: (i % tab_blocks, 0)))
            args.append(t)
    return pl.pallas_call(
        functools.partial(_norm_matmul_body, rope=rope_tabs is not None, n_out=len(out_dtypes),
                          scale=scale, head_dim=head_dim),
        grid=(M // tm, N // tn),
        in_specs=in_specs,
        out_specs=out_specs,
        out_shape=out_shape,
        scratch_shapes=[pltpu.VMEM((tm, D), BF16)],
        compiler_params=_params("parallel", "arbitrary"),
        name="norm_matmul",
    )(*args)


def _matmul_body(*refs, has_bias, act):
    if has_bias:
        x_ref, w_ref, b_ref, o_ref = refs
    else:
        x_ref, w_ref, o_ref = refs
    acc = _dot(x_ref[...], w_ref[...])
    if has_bias:
        acc = acc + b_ref[...]
    if act == "sigmoid":
        acc = _sigmoid(acc)
    o_ref[...] = acc.astype(o_ref.dtype)


def matmul(x, w, out_dtype, bias=None, act=None):
    M, K = x.shape
    N = w.shape[1]
    tm = _pick(M, ROW_TILES)
    tn = _pick(N, COL_TILES)
    in_specs = [pl.BlockSpec((tm, K), lambda i, j: (i, 0)),
                pl.BlockSpec((K, tn), lambda i, j: (0, j))]
    args = [x, w]
    if bias is not None:
        in_specs.append(pl.BlockSpec((1, tn), lambda i, j: (0, j)))
        args.append(bias.reshape(1, N).astype(F32))
    return pl.pallas_call(
        functools.partial(_matmul_body, has_bias=bias is not None, act=act),
        grid=(M // tm, N // tn),
        in_specs=in_specs,
        out_specs=pl.BlockSpec((tm, tn), lambda i, j: (i, j)),
        out_shape=jax.ShapeDtypeStruct((M, N), out_dtype),
        compiler_params=_params("parallel", "parallel"),
        name="matmul",
    )(*args)


CONV_HALO = 8
POOL_HALO = 16


def _conv_body(xb_ref, st_ref, w_ref, b_ref, gwh_ref, gwl_ref, gb_ref, xc_ref, gate_ref, ext_ref,
               *, tb, n_li):
    c = pl.program_id(1)
    nst = CONV_W - 1
    lo = CONV_HALO - nst

    @pl.when(c == 0)
    def _():
        ext_ref[:, lo:CONV_HALO, :] = st_ref[...]

    @pl.when(c > 0)
    def _():
        ext_ref[:, lo:CONV_HALO, :] = ext_ref[:, lo + tb:CONV_HALO + tb, :]

    ext_ref[:, CONV_HALO:CONV_HALO + tb, :] = xb_ref[...].astype(F32)
    acc = b_ref[...][None] + ext_ref[:, lo:lo + tb, :] * w_ref[0:1, :][None]
    for j in range(1, CONV_W):
        acc = acc + ext_ref[:, lo + j:lo + j + tb, :] * w_ref[j:j + 1, :][None]
    xc = _silu(acc)
    xc_ref[...] = xc.astype(xc_ref.dtype)
    x2 = xc.reshape(xc.shape[0] * tb, xc.shape[2])
    hi = x2.astype(BF16)
    lo = (x2 - hi.astype(F32)).astype(BF16)
    g = _dot(hi, gwh_ref[...]) + (_dot(hi, gwl_ref[...]) + _dot(lo, gwh_ref[...])) + gb_ref[...]
    col = lax.broadcasted_iota(jnp.int32, g.shape, 1)
    g = jnp.where(col < n_li, g, _log_sigmoid(g))
    gate_ref[...] = g.reshape(gate_ref.shape)


def conv_silu_gates(xb, state, w, b, gate_w, gate_b, n_li):
    B, T, E = xb.shape
    bb, tb = (1, 256) if T % 256 == 0 else (B, T)
    gw_hi = gate_w.astype(BF16)
    gw_lo = (gate_w - gw_hi.astype(F32)).astype(BF16)
    fixed = lambda i, c: (0, 0)
    return pl.pallas_call(
        functools.partial(_conv_body, tb=tb, n_li=n_li),
        grid=(B // bb, T // tb),
        in_specs=[pl.BlockSpec((bb, tb, E), lambda i, c: (i, c, 0)),
                  pl.BlockSpec((bb, CONV_W - 1, E), lambda i, c: (i, 0, 0)),
                  pl.BlockSpec((CONV_W, E), fixed),
                  pl.BlockSpec((1, E), fixed),
                  pl.BlockSpec((E, LANES), fixed),
                  pl.BlockSpec((E, LANES), fixed),
                  pl.BlockSpec((1, LANES), fixed)],
        out_specs=[pl.BlockSpec((bb, tb, E), lambda i, c: (i, c, 0)),
                   pl.BlockSpec((bb, tb, LANES), lambda i, c: (i, c, 0))],
        out_shape=[jax.ShapeDtypeStruct((B, T, E), BF16),
                   jax.ShapeDtypeStruct((B, T, LANES), F32)],
        scratch_shapes=[pltpu.VMEM((bb, CONV_HALO + tb, E), F32)],
        compiler_params=_params("parallel", "arbitrary"),
        name="conv_silu_gates",
    )(xb, state.astype(F32), w, b.reshape(1, E), gw_hi, gw_lo, gate_b.reshape(1, LANES))


def _mlstm_body(q_ref, k_ref, v_ref, gc_ref, gr_ref, c0_ref, n0_ref, m0_ref, ng_ref,
                h_ref, c_out_ref, n_out_ref, m_out_ref, c_s, n_s, m_s, *, nh, dqk, dv, lc):
    c = pl.program_id(1)

    @pl.when(c == 0)
    def _():
        c_s[...] = c0_ref[0]
        n_s[...] = n0_ref[0]
        m_s[...] = m0_ref[0]

    row = lax.broadcasted_iota(jnp.int32, (lc, lc), 0)
    col = lax.broadcasted_iota(jnp.int32, (lc, lc), 1)
    causal = row >= col
    scale = dqk ** -0.5
    gc = gc_ref[0]
    gr = gr_ref[0]
    for h in range(nh):
        q = (q_ref[0, :, h * dqk:(h + 1) * dqk].astype(F32) * scale).astype(BF16)
        k = k_ref[0, :, h * dqk:(h + 1) * dqk]
        v = v_ref[0, :, h * dv:(h + 1) * dv]
        li_c, lf_c = gc[:, h:h + 1], gc[:, nh + h:nh + h + 1]
        li_r, lf_r = gr[h:h + 1, :], gr[nh + h:nh + h + 1, :]
        b_c = jnp.sum(jnp.where(causal, lf_r, 0.0), axis=1, keepdims=True)
        b_r = jnp.sum(jnp.where(row <= col, lf_c, 0.0), axis=0, keepdims=True)
        m_prev = m_s[h:h + 1, 0:1]
        dm = jnp.where(causal, b_c - b_r + li_r, NEG)
        m_inter = b_c + m_prev
        m_t = jnp.maximum(m_inter, jnp.max(dm, axis=1, keepdims=True))
        a = _dot_nt(q, k) * jnp.exp(dm - m_t)
        inter = jnp.exp(m_inter - m_t)
        c_prev = c_s[h]
        n_prev = n_s[h:h + 1, :]
        num = _dot(a.astype(BF16), v) + inter * _dot(q, c_prev.astype(BF16))
        qn = jnp.sum(q.astype(F32) * n_prev, axis=1, keepdims=True)
        den = jnp.sum(a, axis=1, keepdims=True) + inter * qn
        hh = num / jnp.maximum(jnp.abs(den), jnp.exp(-m_t))
        hn = hh * lax.rsqrt(jnp.mean(hh * hh, axis=-1, keepdims=True) + EPS) \
            * ng_ref[:, h * dv:(h + 1) * dv]
        h_ref[0, :, h * dv:(h + 1) * dv] = hn.astype(h_ref.dtype)
        b_last = b_c[lc - 1:lc, :]
        g_r = b_last - b_r + li_r
        g_c = b_last - b_c + li_c
        m_new = jnp.maximum(b_last + m_prev, jnp.max(g_r, axis=1, keepdims=True))
        w_r = jnp.exp(g_r - m_new)
        w_c = jnp.exp(g_c - m_new)
        decay = jnp.exp(b_last + m_prev - m_new)
        wv = (w_c * v.astype(F32)).astype(BF16)
        c_s[h] = decay * c_prev + _dot_tn(k, wv)
        wr8 = jnp.broadcast_to(w_r, (SUBLANES, lc)).astype(BF16)
        n_s[h:h + 1, :] = decay * n_prev + _dot(wr8, k)[0:1, :]
        m_s[h:h + 1, :] = jnp.broadcast_to(m_new, (1, LANES))

    @pl.when(c == pl.num_programs(1) - 1)
    def _():
        c_out_ref[0] = c_s[...]
        n_out_ref[0] = n_s[...]
        m_out_ref[0] = m_s[...]


def mlstm_recurrence(q, k, v, gcol, grow, c0, n0, m0, norm_g):
    B, T, _ = q.shape
    nh, dqk, dv = c0.shape[1], c0.shape[2], c0.shape[3]
    lc = min(MLSTM_CHUNK, T)
    m0b = jnp.broadcast_to(m0[:, :, None], (B, nh, LANES)).astype(F32)
    h, c_new, n_new, m_new = pl.pallas_call(
        functools.partial(_mlstm_body, nh=nh, dqk=dqk, dv=dv, lc=lc),
        grid=(B, T // lc),
        in_specs=[pl.BlockSpec((1, lc, nh * dqk), lambda b, c: (b, c, 0)),
                  pl.BlockSpec((1, lc, nh * dqk), lambda b, c: (b, c, 0)),
                  pl.BlockSpec((1, lc, nh * dv), lambda b, c: (b, c, 0)),
                  pl.BlockSpec((1, lc, LANES), lambda b, c: (b, c, 0)),
                  pl.BlockSpec((1, 2 * nh, lc), lambda b, c: (b, 0, c)),
                  pl.BlockSpec((1, nh, dqk, dv), lambda b, c: (b, 0, 0, 0)),
                  pl.BlockSpec((1, nh, dqk), lambda b, c: (b, 0, 0)),
                  pl.BlockSpec((1, nh, LANES), lambda b, c: (b, 0, 0)),
                  pl.BlockSpec((1, nh * dv), lambda b, c: (0, 0))],
        out_specs=[pl.BlockSpec((1, lc, nh * dv), lambda b, c: (b, c, 0)),
                   pl.BlockSpec((1, nh, dqk, dv), lambda b, c: (b, 0, 0, 0)),
                   pl.BlockSpec((1, nh, dqk), lambda b, c: (b, 0, 0)),
                   pl.BlockSpec((1, nh, LANES), lambda b, c: (b, 0, 0))],
        out_shape=[jax.ShapeDtypeStruct((B, T, nh * dv), BF16),
                   jax.ShapeDtypeStruct((B, nh, dqk, dv), F32),
                   jax.ShapeDtypeStruct((B, nh, dqk), F32),
                   jax.ShapeDtypeStruct((B, nh, LANES), F32)],
        scratch_shapes=[pltpu.VMEM((nh, dqk, dv), F32),
                        pltpu.VMEM((nh, dqk), F32),
                        pltpu.VMEM((nh, LANES), F32)],
        compiler_params=_params("parallel", "arbitrary"),
        name="mlstm_recurrence",
    )(q, k, v, gcol, grow, c0.astype(F32), n0.astype(F32), m0b, norm_g.reshape(1, nh * dv))
    return h, c_new, n_new, m_new[:, :, 0]


def _pool_body(xb_ref, st_ref, wg_ref, sc_ref, y_ref, ext_ref, *, tb, pos0, nbuf):
    c = pl.program_id(1)
    lo = POOL_HALO - nbuf
    bb = xb_ref.shape[0]
    gw = wg_ref.shape[1]

    @pl.when(c == 0)
    def _():
        ext_ref[:, lo:POOL_HALO, :] = st_ref[...]

    @pl.when(c > 0)
    def _():
        ext_ref[:, lo:POOL_HALO, :] = ext_ref[:, lo + tb:POOL_HALO + tb, :]

    ext_ref[:, POOL_HALO:POOL_HALO + tb, :] = xb_ref[...].astype(F32)
    pos = pos0 + c * tb + lax.broadcasted_iota(jnp.int32, (1, tb, 1), 1)
    for g, w in enumerate(POOL_WINDOWS):
        cols = slice(g * gw, (g + 1) * gw)
        cur = ext_ref[:, POOL_HALO:POOL_HALO + tb, cols]
        win = cur
        for i in range(1, w):
            win = win + ext_ref[:, POOL_HALO - i:POOL_HALO - i + tb, cols]
        cnt = jnp.minimum(w, pos + 1).astype(F32)
        d = (win / cnt - cur).reshape(bb * tb, gw).astype(BF16)
        yg = _dot(d, wg_ref[g]) * sc_ref[:, cols]
        y_ref[:, :, cols] = yg.reshape(bb, tb, gw).astype(y_ref.dtype)


def pool_mix(xb, state, w_grp, scale, pos0):
    B, T, E = xb.shape
    nbuf = state.shape[1]
    bb, tb = (1, 256) if T % 256 == 0 else (B, T)
    ng, gw = w_grp.shape[0], w_grp.shape[1]
    return pl.pallas_call(
        functools.partial(_pool_body, tb=tb, pos0=pos0, nbuf=nbuf),
        grid=(B // bb, T // tb),
        in_specs=[pl.BlockSpec((bb, tb, E), lambda i, c: (i, c, 0)),
                  pl.BlockSpec((bb, nbuf, E), lambda i, c: (i, 0, 0)),
                  pl.BlockSpec((ng, gw, gw), lambda i, c: (0, 0, 0)),
                  pl.BlockSpec((1, E), lambda i, c: (0, 0))],
        out_specs=pl.BlockSpec((bb, tb, E), lambda i, c: (i, c, 0)),
        out_shape=jax.ShapeDtypeStruct((B, T, E), BF16),
        scratch_shapes=[pltpu.VMEM((bb, POOL_HALO + tb, E), F32)],
        compiler_params=_params("parallel", "arbitrary"),
        name="pool_mix",
    )(xb, state.astype(F32), w_grp, scale.reshape(1, E))


def _tail_body(*refs, kind, final):
    if kind == "mlstm":
        hn_ref, o_ref, xc_ref, z_ref, skip_ref = refs[:5]
        rest = refs[5:]
        mix = (o_ref[...].astype(F32) * hn_ref[...].astype(F32)
               + skip_ref[...] * xc_ref[...].astype(F32)) * _silu(z_ref[...].astype(F32))
    else:
        y_ref, z_ref = refs[:2]
        rest = refs[2:]
        mix = y_ref[...].astype(F32) * _silu(z_ref[...].astype(F32))
    x_ref, p_ref, wd_ref, gw_ref, pw_ref = rest[:5]
    rest = rest[5:]
    x1 = x_ref[...] + _dot(mix.astype(BF16), wd_ref[...])
    gate = _sigmoid(_dot(x1.astype(BF16), gw_ref[...]))
    x2 = x1 + gate * _dot(p_ref[...].astype(BF16), pw_ref[...])
    if final:
        fg_ref, xo_ref, yo_ref = rest
        yo_ref[...] = x2 * lax.rsqrt(jnp.mean(x2 * x2, axis=-1, keepdims=True) + EPS) * fg_ref[...]
    else:
        (xo_ref,) = rest
    xo_ref[...] = x2


def layer_tail(kind, mix_inputs, x, p, w_down, gate_w, ple_w, skip=None, final_g=None):
    M, D = x.shape
    tm = _pick(M, (512, 256))
    row = lambda i: (i, 0)
    fixed = lambda i: (0, 0)
    in_specs, args = [], []
    for a in mix_inputs:
        in_specs.append(pl.BlockSpec((tm, a.shape[1]), row))
        args.append(a)
    if kind == "mlstm":
        in_specs.append(pl.BlockSpec((1, skip.shape[-1]), fixed))
        args.append(skip.reshape(1, -1))
    in_specs += [pl.BlockSpec((tm, D), row), pl.BlockSpec((tm, p.shape[1]), row),
                 pl.BlockSpec(w_down.shape, fixed), pl.BlockSpec(gate_w.shape, fixed),
                 pl.BlockSpec(ple_w.shape, fixed)]
    args += [x, p, w_down, gate_w, ple_w]
    out_specs = [pl.BlockSpec((tm, D), row)]
    out_shape = [jax.ShapeDtypeStruct((M, D), F32)]
    if final_g is not None:
        in_specs.append(pl.BlockSpec((1, D), fixed))
        args.append(final_g.reshape(1, D))
        out_specs.append(pl.BlockSpec((tm, D), row))
        out_shape.append(jax.ShapeDtypeStruct((M, D), F32))
    return pl.pallas_call(
        functools.partial(_tail_body, kind=kind, final=final_g is not None),
        grid=(M // tm,),
        in_specs=in_specs, out_specs=out_specs, out_shape=out_shape,
        compiler_params=_params("parallel"),
        name="layer_tail_" + kind,
    )(*args)


def _sortable_key(s):
    s = jnp.where(s == 0.0, 0.0, s)
    bits = lax.bitcast_convert_type(s, jnp.int32)
    return jnp.where(bits < 0, bits ^ jnp.int32(0x7FFFFFFF), bits)


COUNT_STRIP_VREGS = 32


def _count(keys_ref, nvalid, pred):
    rows, cw = keys_ref.shape[1], keys_ref.shape[2]
    strip = max(SUBLANES, min(rows, COUNT_STRIP_VREGS * SUBLANES * LANES // cw))

    def body(kc, acc):
        parts = []
        for r0 in range(0, rows, strip):
            rs = slice(r0, min(r0 + strip, rows))
            hit = jnp.where(pred(keys_ref[kc, rs, :], kc, rs), 1.0, 0.0)
            part = hit[:, 0:LANES]
            for g in range(1, cw // LANES):
                part = part + hit[:, g * LANES:(g + 1) * LANES]
            parts.append(part)
        return acc + (parts[0] if len(parts) == 1 else jnp.concatenate(parts, axis=0))

    acc = lax.fori_loop(0, nvalid, body, jnp.zeros((rows, LANES), F32))
    return jnp.sum(acc, axis=1, keepdims=True)


def _select_topk(keys_ref, cidx_ref, nvalid, k_top, idx_bits, bounds=None):
    rows, cw = keys_ref.shape[1], keys_ref.shape[2]
    kf = float(k_top)
    imin = jnp.int32(INT_MIN)

    def bit_body(i, tau_u):
        cand_u = tau_u | jnp.left_shift(jnp.int32(1), 31 - i)
        cand = cand_u ^ imin
        cnt = _count(keys_ref, nvalid, lambda kk, kc, rs: kk >= cand[rs])
        return jnp.where(cnt >= kf, cand_u, tau_u)

    first_bit = 0
    tau_u0 = jnp.zeros((rows, 1), jnp.int32)
    if bounds is not None:
        u_lo = _sortable_key(bounds[0]) ^ imin
        u_hi = _sortable_key(bounds[1]) ^ imin
        shared = jnp.minimum(lax.clz(u_lo ^ u_hi), 31)
        first_bit = jnp.min(shared.astype(F32)).astype(jnp.int32)
        keep = jnp.bitwise_not(lax.shift_right_logical(jnp.int32(-1), first_bit))
        tau_u0 = u_lo & keep
    tau_u = lax.fori_loop(first_bit, 32, bit_body, tau_u0)
    tau = tau_u ^ imin
    n_gt = _count(keys_ref, nvalid, lambda kk, kc, rs: kk > tau[rs])
    n_ge = _count(keys_ref, nvalid, lambda kk, kc, rs: kk >= tau[rs])
    need = kf - n_gt
    short = tau_u == 0
    cidx_ref[...] = jnp.broadcast_to(jnp.where(short, -1, 2 ** 30), cidx_ref.shape)
    excess = jnp.max(jnp.where((n_ge > kf) & jnp.logical_not(short), 1.0, 0.0))

    @pl.when(excess > 0.5)
    def _():
        lane = lax.broadcasted_iota(jnp.int32, (1, cw), 1)

        def idx_body(i, cut):
            cand = cut | jnp.left_shift(jnp.int32(1), idx_bits - 1 - i)
            cnt = _count(keys_ref, nvalid,
                         lambda kk, kc, rs: (kk == tau[rs]) & ((kc * cw + lane) < cand[rs]))
            return jnp.where(cnt < need, cand, cut)

        cut = lax.fori_loop(0, idx_bits, idx_body, jnp.zeros((rows, 1), jnp.int32))
        cidx_ref[...] = jnp.broadcast_to(jnp.where(short, -1, cut), cidx_ref.shape)

    return tau


def _selected(kk, colv, tau, cut):
    return (kk > tau) | ((kk == tau) & (colv <= cut))


def _n_causal_chunks(qb, tq, tk):
    return lax.div(qb * tq + (tq - 1), jnp.int32(tk)) + 1


def _idx_prompt_body(qi_ref, wi_ref, ki_ref, bias_ref, keys_ref, cidx_ref, *,
                     nh, dh, tq, tk, nk, k_top, idx_bits):
    per = keys_ref.shape[2] // tk
    gw = per * tk
    qb = pl.program_id(1)
    nvalid = _n_causal_chunks(qb, tq, gw)
    rowpos = qb * tq + lax.broadcasted_iota(jnp.int32, (tq, 1), 0)
    lane = lax.broadcasted_iota(jnp.int32, (1, tk), 1)
    wi = wi_ref[0]

    def score_group(gc, carry):
        top1, top2 = carry
        for part in range(per):
            kt = ki_ref[0, gc * per + part]
            s = jnp.zeros((tq, tk), F32)
            for h in range(nh):
                rel = jnp.maximum(_dot_nt(qi_ref[0, :, h * dh:(h + 1) * dh], kt), 0.0)
                s = s + rel * wi[:, h:h + 1]
            valid = (gc * gw + part * tk + lane) <= rowpos
            keys_ref[gc, :, part * tk:(part + 1) * tk] = jnp.where(
                valid, _sortable_key(s), jnp.int32(INT_MIN))
            sv = jnp.where(valid, s, -F32_BIG)
            for g in range(tk // LANES):
                x = sv[:, g * LANES:(g + 1) * LANES]
                top2 = jnp.maximum(top2, jnp.minimum(top1, x))
                top1 = jnp.maximum(top1, x)
        return top1, top2

    floor = jnp.full((tq, LANES), -F32_BIG, F32)
    top1, top2 = lax.fori_loop(0, nvalid, score_group, (floor, floor))
    bounds = None
    if k_top <= 2 * LANES:
        bounds = (jnp.min(top2, axis=1, keepdims=True), jnp.max(top1, axis=1, keepdims=True))
    tau = _select_topk(keys_ref, cidx_ref, nvalid, k_top, idx_bits, bounds)
    cut = cidx_ref[:, 0:1]

    def write_group(gc, carry):
        for part in range(per):
            kk = keys_ref[gc, :, part * tk:(part + 1) * tk]
            sel = _selected(kk, gc * gw + part * tk + lane, tau, cut)
            bias_ref[0, 0, gc * per + part] = jnp.where(sel, 0.0, NEG).astype(bias_ref.dtype)
        return carry

    lax.fori_loop(0, nvalid, write_group, 0)

    def fill_chunk(kc, carry):
        bias_ref[0, 0, kc] = jnp.full((tq, tk), NEG, bias_ref.dtype)
        return carry

    lax.fori_loop(nvalid * per, nk, fill_chunk, 0)


def indexer_bias_prompt(qi, wi, ki, k_top):
    B, T, _ = qi.shape
    dh = ki.shape[-1]
    nh = qi.shape[-1] // dh
    tq, tk = min(IDX_TQ, T), min(ATT_TK, T)
    tqa = min(ATT_TQ, T)
    sub = tqa // tq
    nq, nk = T // tq, T // tk
    per = 2 if nk % 2 == 0 else 1
    idx_bits = int(T).bit_length() + 1
    return pl.pallas_call(
        functools.partial(_idx_prompt_body, nh=nh, dh=dh, tq=tq, tk=tk, nk=nk, k_top=k_top,
                          idx_bits=idx_bits),
        grid=(B, nq),
        in_specs=[pl.BlockSpec((1, tq, nh * dh), lambda b, i: (b, i, 0)),
                  pl.BlockSpec((1, tq, LANES), lambda b, i: (b, i, 0)),
                  pl.BlockSpec((1, nk, tk, dh), lambda b, i: (b, 0, 0, 0))],
        out_specs=pl.BlockSpec((1, 1, nk, tq, tk), lambda b, i: (b, i // sub, 0, i % sub, 0)),
        out_shape=jax.ShapeDtypeStruct((B, T // tqa, nk, tqa, tk), BF16),
        scratch_shapes=[pltpu.VMEM((nk // per, tq, per * tk), jnp.int32),
                        pltpu.VMEM((tq, LANES), jnp.int32)],
        compiler_params=_params("parallel", "arbitrary"),
        name="indexer_bias_prompt",
    )(qi, wi, ki.reshape(B, nk, tk, dh))


def _attn_prompt_body(q_ref, k_ref, v_ref, bias_ref, o_ref, m_s, acc_s, *, nh, dh, tq, tk):
    qb, kb = pl.program_id(1), pl.program_id(2)
    nvalid = _n_causal_chunks(qb, tq, tk)

    @pl.when(kb == 0)
    def _():
        m_s[...] = jnp.full(m_s.shape, NEG, F32)
        acc_s[...] = jnp.zeros(acc_s.shape, F32)

    @pl.when(kb < nvalid)
    def _():
        bias = bias_ref[0, 0, 0].astype(F32)
        for h in range(nh):
            s = _dot_nt(q_ref[0, h], k_ref[0, h]) + bias
            m_prev = m_s[h]
            m_new = jnp.maximum(m_prev, jnp.max(s, axis=1, keepdims=True))
            alpha = jnp.exp2(m_prev - m_new)
            p = jnp.exp2(s - jnp.concatenate([m_new] * (tk // LANES), axis=1))
            acc_s[h] = alpha * acc_s[h] + _dot(p.astype(BF16), v_ref[0, h])
            m_s[h] = m_new

    @pl.when(kb == pl.num_programs(2) - 1)
    def _():
        for h in range(nh):
            acc = acc_s[h]
            o_ref[0, :, h * dh:(h + 1) * dh] = (acc[:, :dh] / acc[:, dh:dh + 1]).astype(o_ref.dtype)


def attention_prompt(q, k, v, bias):
    B, nh, T, dh = q.shape
    assert tuple(v.shape) == (B, nh, T, LANES) and dh < LANES
    _, nq, nk, tq, tk = bias.shape

    def kv_map(b, i, j):
        return (b, 0, jnp.minimum(j, _n_causal_chunks(i, tq, tk) - 1), 0)

    def bias_map(b, i, j):
        return (b, i, jnp.minimum(j, _n_causal_chunks(i, tq, tk) - 1), 0, 0)

    return pl.pallas_call(
        functools.partial(_attn_prompt_body, nh=nh, dh=dh, tq=tq, tk=tk),
        grid=(B, nq, nk),
        in_specs=[pl.BlockSpec((1, nh, tq, dh), lambda b, i, j: (b, 0, i, 0)),
                  pl.BlockSpec((1, nh, tk, dh), kv_map),
                  pl.BlockSpec((1, nh, tk, LANES), kv_map),
                  pl.BlockSpec((1, 1, 1, tq, tk), bias_map)],
        out_specs=pl.BlockSpec((1, tq, nh * dh), lambda b, i, j: (b, i, 0)),
        out_shape=jax.ShapeDtypeStruct((B, T, nh * dh), BF16),
        scratch_shapes=[pltpu.VMEM((nh, tq, LANES), F32),
                        pltpu.VMEM((nh, tq, LANES), F32)],
        compiler_params=_params("parallel", "parallel", "arbitrary"),
        name="attention_prompt",
    )(q, k, v, bias)


def _idx_sample_body(pt_ref, qi_ref, wi_ref, knew_ref, *rest, nh, ts, npg, nsteps, k_top,
                     idx_bits):
    page_refs = rest[:npg]
    bias_ref, keys_ref, cidx_ref = rest[npg:]
    s = pl.program_id(1)
    cw = keys_ref.shape[2]
    wi = wi_ref[0]
    qi = qi_ref[0]

    def scores(kt):
        full = jnp.maximum(_dot(qi, kt), 0.0)
        out = jnp.zeros((ts, cw), F32)
        for h in range(nh):
            out = out + full[h * ts:(h + 1) * ts, :] * wi[:, h:h + 1]
        return out

    kt = jnp.concatenate([r[...] for r in page_refs], axis=1).astype(BF16)
    keys_ref[s] = _sortable_key(scores(kt))

    @pl.when(s == nsteps - 1)
    def _():
        lane = lax.broadcasted_iota(jnp.int32, (1, cw), 1)
        past = nsteps * cw
        rowpos = past + lax.broadcasted_iota(jnp.int32, (ts, 1), 0)
        valid = (past + lane) <= rowpos
        keys_ref[nsteps] = jnp.where(valid, _sortable_key(scores(knew_ref[0])),
                                     jnp.int32(INT_MIN))
        tau = _select_topk(keys_ref, cidx_ref, nsteps + 1, k_top, idx_bits)
        cut = cidx_ref[:, 0:1]
        for kc in range(nsteps + 1):
            sel = _selected(keys_ref[kc], kc * cw + lane, tau, cut)
            bias_ref[0, kc] = jnp.where(sel, 0.0, NEG).astype(bias_ref.dtype)


def _page_specs(n_pages_per_step, block, layer, first_step):
    specs = []
    for i in range(n_pages_per_step):
        def imap(b, s, pt, i=i):
            step = jnp.maximum(s - first_step, 0)
            return (layer, pt[b, step * n_pages_per_step + i]) + (0,) * len(block)
        specs.append(pl.BlockSpec((None, None) + tuple(block), imap))
    return specs


def indexer_bias_sample(qi_hq, wi, ki_new_t, cache_kidx_t, layer, page_table, k_top):
    B, n_pages = page_table.shape
    _, _, dh, page = cache_kidx_t.shape
    ts = wi.shape[1]
    nh = qi_hq.shape[1] // ts
    npg = min(IDX_PAGES_PER_STEP, n_pages)
    nsteps = n_pages // npg
    cw = npg * page
    idx_bits = int(n_pages * page + cw).bit_length() + 1
    grid_spec = pltpu.PrefetchScalarGridSpec(
        num_scalar_prefetch=1,
        grid=(B, nsteps),
        in_specs=[pl.BlockSpec((1, nh * ts, dh), lambda b, s, pt: (b, 0, 0)),
                  pl.BlockSpec((1, ts, LANES), lambda b, s, pt: (b, 0, 0)),
                  pl.BlockSpec((1, dh, cw), lambda b, s, pt: (b, 0, 0))]
        + _page_specs(npg, (dh, page), layer, 0),
        out_specs=pl.BlockSpec((1, nsteps + 1, ts, cw), lambda b, s, pt: (b, 0, 0, 0)),
        scratch_shapes=[pltpu.VMEM((nsteps + 1, ts, cw), jnp.int32),
                        pltpu.VMEM((ts, LANES), jnp.int32)],
    )
    return pl.pallas_call(
        functools.partial(_idx_sample_body, nh=nh, ts=ts, npg=npg, nsteps=nsteps, k_top=k_top,
                          idx_bits=idx_bits),
        grid_spec=grid_spec,
        out_shape=jax.ShapeDtypeStruct((B, nsteps + 1, ts, cw), F32),
        compiler_params=_params("parallel", "arbitrary"),
        name="indexer_bias_sample",
    )(page_table, qi_hq, wi, ki_new_t, *([cache_kidx_t] * npg))


def _attn_sample_body(pt_ref, qbd_ref, knew_ref, vnew_ref, bias_ref, *rest, nh, ts, npg, dh):
    kpages, vpages = rest[:npg], rest[npg:2 * npg]
    o_ref, m_s, l_s, acc_s = rest[2 * npg:]
    s = pl.program_id(1)
    rows = nh * ts

    @pl.when(s == 0)
    def _():
        m_s[...] = jnp.full(m_s.shape, NEG, F32)
        l_s[...] = jnp.zeros(l_s.shape, F32)
        acc_s[...] = jnp.zeros(acc_s.shape, F32)

    def update(kt, vt):
        n = kt.shape[1]
        bias = bias_ref[0, 0][:, :n]
        logits = _dot(qbd_ref[0], kt) + jnp.concatenate([bias] * nh, axis=0)
        m_prev = m_s[...]
        m_new = jnp.maximum(m_prev, jnp.max(logits, axis=1, keepdims=True))
        alpha = jnp.exp2(m_prev - m_new)
        p = jnp.exp2(logits - m_new)
        l_s[...] = alpha * l_s[...] + jnp.sum(p, axis=1, keepdims=True)
        acc_s[...] = alpha * acc_s[...] + _dot_nt(p.astype(BF16), vt)
        m_s[...] = m_new

    @pl.when(s == 0)
    def _():
        update(knew_ref[0], vnew_ref[0])

    def gather_pages(page_refs):
        return jnp.concatenate([r[...].reshape(nh * dh, r.shape[2]) for r in page_refs],
                               axis=1).astype(BF16)

    @pl.when(s > 0)
    def _():
        update(gather_pages(kpages), gather_pages(vpages))

    @pl.when(s == pl.num_programs(1) - 1)
    def _():
        o = acc_s[...] / l_s[...]
        lane = lax.broadcasted_iota(jnp.int32, (1, nh * dh), 1)
        res = jnp.zeros((ts, nh * dh), F32)
        for h in range(nh):
            own = (lane >= h * dh) & (lane < (h + 1) * dh)
            res = res + jnp.where(own, o[h * ts:(h + 1) * ts, :], 0.0)
        o_ref[0] = res.astype(o_ref.dtype)


def attention_sample(qbd, k_new_t, v_new_t, bias, cache_k_t, cache_v_t, layer, page_table, ts):
    B, n_pages = page_table.shape
    _, _, nh, dh, page = cache_k_t.shape
    width = nh * dh
    n_new = k_new_t.shape[2]
    npg = min(PAGES_PER_STEP, n_pages)
    cw = npg * page
    nsteps1 = n_pages // npg + 1
    per_chunk = bias.shape[3] // cw
    last_chunk = bias.shape[1] - 1
    rows = nh * ts

    def bias_map(b, s, pt):
        g = jnp.maximum(s - 1, 0)
        return (b, jnp.where(s == 0, last_chunk, g // per_chunk), 0,
                jnp.where(s == 0, 0, g % per_chunk))

    grid_spec = pltpu.PrefetchScalarGridSpec(
        num_scalar_prefetch=1,
        grid=(B, nsteps1),
        in_specs=[pl.BlockSpec((1, rows, width), lambda b, s, pt: (b, 0, 0)),
                  pl.BlockSpec((1, width, n_new), lambda b, s, pt: (b, 0, 0)),
                  pl.BlockSpec((1, width, n_new), lambda b, s, pt: (b, 0, 0)),
                  pl.BlockSpec((1, 1, ts, cw), bias_map)]
        + _page_specs(npg, (nh, dh, page), layer, 1) + _page_specs(npg, (nh, dh, page), layer, 1),
        out_specs=pl.BlockSpec((1, ts, width), lambda b, s, pt: (b, 0, 0)),
        scratch_shapes=[pltpu.VMEM((rows, 1), F32),
                        pltpu.VMEM((rows, 1), F32),
                        pltpu.VMEM((rows, width), F32)],
    )
    return pl.pallas_call(
        functools.partial(_attn_sample_body, nh=nh, ts=ts, npg=npg, dh=dh),
        grid_spec=grid_spec,
        out_shape=jax.ShapeDtypeStruct((B, ts, width), BF16),
        compiler_params=_params("parallel", "arbitrary"),
        name="attention_sample",
    )(page_table, qbd, k_new_t, v_new_t, bias, *([cache_k_t] * npg), *([cache_v_t] * npg))


def _pad_time(a, t_pad, value=0.0):
    return jnp.pad(a, ((0, 0), (0, t_pad - a.shape[1]), (0, 0)), constant_values=value)


def _mlstm_layer(x, g, B, T, state, w, j):
    c0, n0, m0, conv0 = state
    nh = c0.shape[1]
    E = w["a_conv_w"].shape[-1]
    w_up = w["a_w_up"][j]
    xb, = norm_matmul(x, g, w_up[:, :E], (BF16,))
    z, = norm_matmul(x, g, w_up[:, E:], (BF16,))
    xc, gcol = conv_silu_gates(xb.reshape(B, T, E), conv0, w["a_conv_w"][j], w["a_conv_b"][j],
                               w["a_w_if_pad"][j], w["a_b_if_pad"][j], nh)
    xc = xc.reshape(B * T, E)
    q = matmul(xc, w["a_w_q"][j], BF16)
    k = matmul(xc, w["a_w_k"][j], BF16)
    v = matmul(xb, w["a_w_v"][j], BF16)
    o = matmul(xb, w["a_w_o"][j], BF16, bias=w["a_b_o"][j], act="sigmoid")
    q3, k3, v3 = q.reshape(B, T, -1), k.reshape(B, T, -1), v.reshape(B, T, -1)
    tp = T if T % LANES == 0 else ((T + LANES - 1) // LANES) * LANES
    if tp != T:
        q3, k3, v3 = _pad_time(q3, tp), _pad_time(k3, tp), _pad_time(v3, tp)
        pad_row = jnp.where(jnp.arange(LANES) < nh, NEG, 0.0).astype(F32)
        gcol = jnp.concatenate([gcol, jnp.broadcast_to(pad_row, (B, tp - T, LANES))], axis=1)
    grow = jnp.swapaxes(gcol[:, :, :2 * nh], 1, 2)
    hn, c_new, n_new, m_new = mlstm_recurrence(q3, k3, v3, gcol, grow, c0, n0, m0, w["a_norm_g"][j])
    hn = hn[:, :T].reshape(B * T, -1)
    assert T >= CONV_W - 1
    conv_new = xb.reshape(B, T, E)[:, T - (CONV_W - 1):].astype(F32)
    return (hn, o, xc, z), (c_new, n_new, m_new, conv_new)


def _rope_tables(pos):
    half = 32
    inv = ROPE_THETA ** (-np.arange(half, dtype=np.float64) / half)
    ang = np.asarray(pos, np.float64)[:, None] * inv[None, :]
    cos = np.tile(np.cos(ang), (1, 4))
    sin = np.tile(np.concatenate([-np.sin(ang), np.sin(ang)], axis=1), (1, 2))
    return jnp.asarray(cos, F32), jnp.asarray(sin, F32)


def _dsa_project(x, g, B, T, pos, w, j, nh, dh, nhi, di, head_major=False):
    M = B * T
    cos, sin = _rope_tables(pos)
    tm = _pick(T, ROW_TILES) if T % ROW_TILES[-1] == 0 else _pick(M, ROW_TILES)
    if T % tm == 0:
        tabs, tab_blocks = (cos, sin), T // tm
    else:
        tabs, tab_blocks = (jnp.tile(cos, (M // T, 1)), jnp.tile(sin, (M // T, 1))), M // tm
    rope = dict(rope_tabs=tabs, tab_blocks=tab_blocks, row_tile=tm)
    hm = (lambda *widths: (B, T, dh, widths)) if head_major else (lambda *widths: None)
    q, = norm_matmul(x, g, w["b_wq"][j], (BF16,), scale=dh ** -0.5 * math.log2(math.e),
                     head_major=hm(dh), **rope)
    wide = "T" if head_major else 0
    k32, k16 = norm_matmul(x, g, w["b_wk"][j], (F32, BF16), head_major=hm(wide, dh), **rope)
    v32, v16 = norm_matmul(x, g, w["b_wv"][j], (F32, BF16), head_major=hm(wide, LANES),
                           row_tile=tm)
    gate, = norm_matmul(x, g, w["b_wg"][j], (BF16,))
    qi, = norm_matmul(x, g, w["b_wqi"][j], (BF16,), **rope)
    ki32, ki16 = norm_matmul(x, g, w["b_wki_pad"][j], (F32, BF16), head_major=hm(wide, 0), **rope)
    wi, = norm_matmul(x, g, w["b_wwi_pad"][j], (F32,), scale=nhi ** -0.5 * di ** -0.5)
    if head_major:
        new = (jnp.transpose(k32.reshape(B, nh, dh, T), (0, 3, 1, 2)),
               jnp.transpose(v32.reshape(B, nh, dh, T), (0, 3, 1, 2)),
               jnp.transpose(ki32[:, :di], (0, 2, 1)))
    else:
        new = (k32.reshape(B, T, nh, dh), v32.reshape(B, T, nh, dh), ki32[:, :di].reshape(B, T, di))
    return q, k16, v16, gate, qi, ki16[:, :di].reshape(B, T, di), wi, new


def _dsa_layer_prompt(x, g, B, T, w, j, dims):
    nh, dh, nhi, di = dims
    q, k16, v16, gate, qi, ki16, wi, new = _dsa_project(
        x, g, B, T, np.arange(T), w, j, *dims, head_major=True)
    k_top = min(TOPK_MAX, T // 4)
    bias = indexer_bias_prompt(qi.reshape(B, T, nhi * di), wi.reshape(B, T, LANES), ki16, k_top)
    attn = attention_prompt(q, k16, v16, bias).reshape(B * T, nh * dh)
    return (attn, gate), new


def _dsa_layer_sample(x, g, B, T, w, j, dims, ctx):
    nh, dh, nhi, di = dims
    cache_k, cache_v, cache_kidx, page_table = ctx
    n_pages = page_table.shape[1]
    page = cache_kidx.shape[2]
    past = n_pages * page
    q, k16, v16, gate, qi, ki16, wi, new = _dsa_project(
        x, g, B, T, past + np.arange(T), w, j, *dims)
    k_top = min(TOPK_MAX, (past + T) // 4)
    cw = min(IDX_PAGES_PER_STEP, n_pages) * page
    qi_hq = jnp.transpose(qi.reshape(B, T, nhi, di), (0, 2, 1, 3)).reshape(B, nhi * T, di)
    token_last = lambda a, n: jnp.swapaxes(_pad_time(a, n), 1, 2)
    ki_new_t = token_last(ki16, cw)
    bias = indexer_bias_sample(qi_hq, wi.reshape(B, T, LANES), ki_new_t,
                               jnp.swapaxes(cache_kidx, 2, 3), j, page_table, k_top)
    q3 = q.reshape(B, T, nh * dh)
    own = (jnp.arange(nh * T)[:, None] // T) == (jnp.arange(nh * dh)[None, :] // dh)
    qbd = jnp.where(own[None], jnp.tile(q3, (1, nh, 1)), 0).astype(BF16)
    width = nh * dh
    n_new = ((T + LANES - 1) // LANES) * LANES
    attn = attention_sample(qbd, token_last(k16.reshape(B, T, width), n_new),
                            token_last(v16.reshape(B, T, width), n_new), bias,
                            jnp.transpose(cache_k, (0, 1, 3, 4, 2)),
                            jnp.transpose(cache_v, (0, 1, 3, 4, 2)), j, page_table, T)
    return (attn.reshape(B * T, width), gate), new


def _pool_layer(x, g, B, T, buf, pos0, w, j):
    E = w["c_scale"].shape[-1]
    w_up = w["c_w_up"][j]
    xb, = norm_matmul(x, g, w_up[:, :E], (BF16,))
    z, = norm_matmul(x, g, w_up[:, E:], (BF16,))
    xb3 = xb.reshape(B, T, E)
    y = pool_mix(xb3, buf, w["c_w_grp"][j], w["c_scale"][j], pos0).reshape(B * T, E)
    nbuf = buf.shape[1]
    if T >= nbuf:
        new_buf = xb3[:, T - nbuf:].astype(F32)
    else:
        new_buf = jnp.concatenate([buf[:, T:].astype(F32), xb3.astype(F32)], axis=1)
    return (y, z), new_buf


def _run_group(x3, p4, pos0, mlstm_state, dsa_ctx, pool_state, w, dims):
    B, T, D = x3.shape
    depth = p4.shape[0]
    x = x3.reshape(B * T, D)
    new_a, new_b, new_c = [], [], []
    y = None
    for i in range(depth):
        kind, j = i % 3, i // 3
        g = w["norm_g"][i]
        final_g = w["final_g"] if i == depth - 1 else None
        p = p4[i].reshape(B * T, -1)
        if kind == 0:
            state = tuple(s[j] for s in mlstm_state)
            mix, st = _mlstm_layer(x, g, B, T, state, w, j)
            new_a.append(st)
            out = layer_tail("mlstm", mix, x, p, w["a_w_down"][j], w["ple_gate_w"][i], w["ple_w"][i],
                             skip=w["a_skip"][j], final_g=final_g)
        elif kind == 1:
            if dsa_ctx is None:
                mix, st = _dsa_layer_prompt(x, g, B, T, w, j, dims)
            else:
                mix, st = _dsa_layer_sample(x, g, B, T, w, j, dims, dsa_ctx)
            new_b.append(st)
            out = layer_tail("gated", mix, x, p, w["b_w_out"][j], w["ple_gate_w"][i], w["ple_w"][i],
                             final_g=final_g)
        else:
            mix, st = _pool_layer(x, g, B, T, pool_state[j], pos0, w, j)
            new_c.append(st)
            out = layer_tail("gated", mix, x, p, w["c_w_down"][j], w["ple_gate_w"][i], w["ple_w"][i],
                             final_g=final_g)
        x = out[0]
        if final_g is not None:
            y = out[1]
    a_states = tuple(jnp.stack([s[r] for s in new_a]) for r in range(4))
    b_states = tuple(jnp.stack([s[r] for s in new_b]) for r in range(3))
    c_state = jnp.stack(new_c)
    return y.reshape(B, T, D), a_states, b_states, c_state


def kernel(x_prompt, x_sample, state_mlstm_C, state_mlstm_n, state_mlstm_m, state_mlstm_conv, state_pool,
           cache_k, cache_v, cache_kidx, page_table, p_prompt, p_sample,
           norm_g, final_g, ple_w, ple_gate_w,
           a_w_up, a_conv_w, a_conv_b, a_w_q, a_w_k, a_w_v, a_w_if, a_b_if, a_w_o, a_b_o, a_norm_g,
           a_skip, a_w_down, b_w_in, b_w_out, c_w_up, c_w_grp, c_scale, c_w_down):
    D = x_prompt.shape[-1]
    nh, dh = cache_k.shape[3], cache_k.shape[4]
    di = cache_kidx.shape[-1]
    aw = nh * dh
    nhi = (b_w_in.shape[-1] - 4 * aw - di) // (di + 1)
    dims = (nh, dh, nhi, di)
    nha = state_mlstm_C.shape[2]
    bf = lambda a: a.astype(BF16)

    def pad_cols(a, n):
        return jnp.pad(a, [(0, 0)] * (a.ndim - 1) + [(0, n - a.shape[-1])])

    o1, o2, o3, o4, o5, o6 = np.cumsum([aw, aw, aw, aw, nhi * di, di])
    w = dict(
        norm_g=norm_g, final_g=final_g, ple_w=bf(ple_w), ple_gate_w=bf(ple_gate_w),
        a_w_up=bf(a_w_up), a_conv_w=a_conv_w, a_conv_b=a_conv_b,
        a_w_q=bf(a_w_q), a_w_k=bf(a_w_k), a_w_v=bf(a_w_v), a_w_o=bf(a_w_o), a_b_o=a_b_o,
        a_w_if_pad=pad_cols(a_w_if, LANES), a_b_if_pad=pad_cols(a_b_if, LANES),
        a_norm_g=a_norm_g, a_skip=a_skip, a_w_down=bf(a_w_down),
        b_wq=bf(b_w_in[..., :o1]), b_wk=bf(b_w_in[..., o1:o2]), b_wv=bf(b_w_in[..., o2:o3]),
        b_wg=bf(b_w_in[..., o3:o4]), b_wqi=bf(b_w_in[..., o4:o5]),
        b_wki_pad=bf(pad_cols(b_w_in[..., o5:o6], LANES)),
        b_wwi_pad=bf(pad_cols(b_w_in[..., o6:], LANES)),
        b_w_out=bf(b_w_out), c_w_up=bf(c_w_up), c_w_grp=bf(c_w_grp), c_scale=c_scale,
        c_w_down=bf(c_w_down),
    )

    Bp = x_prompt.shape[0]
    na, nc = state_mlstm_C.shape[0], state_pool.shape[0]
    zeros_a = (jnp.zeros((na, Bp) + state_mlstm_C.shape[2:], F32),
               jnp.zeros((na, Bp) + state_mlstm_n.shape[2:], F32),
               jnp.zeros((na, Bp) + state_mlstm_m.shape[2:], F32),
               jnp.zeros((na, Bp) + state_mlstm_conv.shape[2:], F32))
    zeros_c = jnp.zeros((nc, Bp) + state_pool.shape[2:], F32)
    y_p, a_p, b_p, c_p = _run_group(x_prompt, p_prompt, 0, zeros_a, None, zeros_c, w, dims)

    past = page_table.shape[1] * cache_k.shape[2]
    y_s, a_s, b_s, c_s = _run_group(
        x_sample, p_sample, past,
        (state_mlstm_C, state_mlstm_n, state_mlstm_m, state_mlstm_conv),
        (cache_k, cache_v, cache_kidx, page_table), state_pool, w, dims)
    return (y_p, y_s, *a_p, *b_p, c_p, *a_s, *b_s, c_s)
```

```python
import functools
import math

import numpy as np
import jax
import jax.numpy as jnp
from jax import lax
from jax.experimental import pallas as pl
from jax.experimental.pallas import tpu as pltpu

EPS = 1e-6
ROPE_THETA = 10000.0
TOPK_MAX = 256
POOL_WINDOWS = (2, 4, 8, 16)
CONV_W = 4
MLSTM_CHUNK = 256
NEG = -1e30
LANES = 128
SUBLANES = 8
VMEM_LIMIT_BYTES = 56 * 1024 * 1024
ATT_TQ, ATT_TK = 1024, 512
IDX_TQ = 256
PAGES_PER_STEP = 8
IDX_PAGES_PER_STEP = 32
ROW_TILES = (1024, 512, 256)
COL_TILES = (1024, 512, 256, 128)

F32 = jnp.float32
BF16 = jnp.bfloat16
INT_MIN = -2 ** 31


def _params(*sem):
    return pltpu.CompilerParams(dimension_semantics=sem, vmem_limit_bytes=VMEM_LIMIT_BYTES)


def _sigmoid(x):
    return 1.0 / (1.0 + jnp.exp(-x))


def _silu(x):
    return x * _sigmoid(x)


def _log_sigmoid(x):
    return jnp.minimum(x, 0.0) - jnp.log(1.0 + jnp.exp(-jnp.abs(x)))


def _dot(a, b):
    return jnp.dot(a, b, preferred_element_type=F32)


def _dot_nt(a, b):
    return lax.dot_general(a, b, (((1,), (1,)), ((), ())), preferred_element_type=F32)


def _dot_tn(a, b):
    return lax.dot_general(a, b, (((0,), (0,)), ((), ())), preferred_element_type=F32)


def _pick(n, cands):
    for c in cands:
        if n % c == 0:
            return c
    return n


def _rope_tile(a, cos, sin_signed):
    lane = lax.broadcasted_iota(jnp.int32, (1, LANES), 1)
    first_half = (lane % 64) < 32
    pieces = []
    for gi in range(a.shape[1] // LANES):
        ag = a[:, gi * LANES:(gi + 1) * LANES]
        ahead = pltpu.roll(ag, LANES - 32, 1)
        behind = pltpu.roll(ag, 32, 1)
        rot = jnp.where(first_half, ahead, behind)
        pieces.append(ag * cos + rot * sin_signed)
    return pieces[0] if len(pieces) == 1 else jnp.concatenate(pieces, axis=1)


def _rmsnorm_body(x_ref, g_ref, o_ref):
    xf = x_ref[...]
    y = xf * lax.rsqrt(jnp.mean(xf * xf, axis=-1, keepdims=True) + EPS) * g_ref[...]
    o_ref[...] = y.astype(o_ref.dtype)


def rmsnorm_cast(x, g):
    M, D = x.shape
    tm = _pick(M, ROW_TILES)
    return pl.pallas_call(
        _rmsnorm_body,
        grid=(M // tm,),
        in_specs=[pl.BlockSpec((tm, D), lambda i: (i, 0)), pl.BlockSpec((1, D), lambda i: (0, 0))],
        out_specs=pl.BlockSpec((tm, D), lambda i: (i, 0)),
        out_shape=jax.ShapeDtypeStruct((M, D), BF16),
        compiler_params=_params("parallel"),
        name="rmsnorm_cast",
    )(x, g.reshape(1, D))


def _project_body(*refs, rope, n_out, scale, head_dim):
    if rope:
        x_ref, w_ref, cos_ref, sin_ref = refs[:4]
        outs = refs[4:4 + n_out]
    else:
        x_ref, w_ref = refs[:2]
        outs = refs[2:2 + n_out]
    acc = _dot(x_ref[...], w_ref[...])
    if rope:
        acc = _rope_tile(acc, cos_ref[...], sin_ref[...])
    if scale != 1.0:
        acc = acc * scale
    for o in outs:
        if len(o.shape) == 3:
            o[0] = acc.T.astype(o.dtype)
        elif len(o.shape) == 4:
            npad = o.shape[3] - head_dim
            if npad:
                lane = lax.broadcasted_iota(jnp.int32, (acc.shape[0], npad), 1)
                pad = jnp.where(lane == 0, 1.0, 0.0)
            for hh in range(o.shape[1]):
                piece = acc[:, hh * head_dim:(hh + 1) * head_dim]
                if npad:
                    piece = jnp.concatenate([piece, pad], axis=1)
                o[0, hh] = piece.astype(o.dtype)
        else:
            o[...] = acc.astype(o.dtype)


def project(x, w, out_dtypes, rope_tabs=None, tab_blocks=1, scale=1.0, head_major=None,
            row_tile=None):
    M, D = x.shape
    N = w.shape[1]
    tm = row_tile if row_tile is not None else _pick(M, ROW_TILES)
    tn = _pick(N, COL_TILES)
    out_specs = [pl.BlockSpec((tm, tn), lambda i, j: (i, j)) for _ in out_dtypes]
    out_shape = [jax.ShapeDtypeStruct((M, N), dt) for dt in out_dtypes]
    head_dim = 0
    if head_major is not None:
        hb, ht, head_dim, flags = head_major
        assert ht % tm == 0
        tpb = ht // tm
        for idx, width in enumerate(flags):
            if width == "T":
                out_specs[idx] = pl.BlockSpec((1, tn, tm), lambda i, j: (i // tpb, j, i % tpb))
                out_shape[idx] = jax.ShapeDtypeStruct((hb, N, ht), out_dtypes[idx])
            elif width:
                out_specs[idx] = pl.BlockSpec((1, tn // head_dim, tm, width),
                                              lambda i, j: (i // tpb, j, i % tpb, 0))
                out_shape[idx] = jax.ShapeDtypeStruct((hb, N // head_dim, ht, width), out_dtypes[idx])
    in_specs = [pl.BlockSpec((tm, D), lambda i, j: (i, 0)),
                pl.BlockSpec((D, tn), lambda i, j: (0, j))]
    args = [x, w]
    if rope_tabs is not None:
        for t in rope_tabs:
            in_specs.append(pl.BlockSpec((tm, LANES), lambda i, j: (i % tab_blocks, 0)))
            args.append(t)
    return pl.pallas_call(
        functools.partial(_project_body, rope=rope_tabs is not None, n_out=len(out_dtypes),
                          scale=scale, head_dim=head_dim),
        grid=(M // tm, N // tn),
        in_specs=in_specs,
        out_specs=out_specs,
        out_shape=out_shape,
        compiler_params=_params("parallel", "parallel"),
        name="project",
    )(*args)


def _matmul_body(*refs, has_bias, act):
    if has_bias:
        x_ref, w_ref, b_ref, o_ref = refs
    else:
        x_ref, w_ref, o_ref = refs
    acc = _dot(x_ref[...], w_ref[...])
    if has_bias:
        acc = acc + b_ref[...]
    if act == "sigmoid":
        acc = _sigmoid(acc)
    o_ref[...] = acc.astype(o_ref.dtype)


def matmul(x, w, out_dtype, bias=None, act=None):
    M, K = x.shape
    N = w.shape[1]
    tm = _pick(M, ROW_TILES)
    tn = _pick(N, COL_TILES)
    in_specs = [pl.BlockSpec((tm, K), lambda i, j: (i, 0)),
                pl.BlockSpec((K, tn), lambda i, j: (0, j))]
    args = [x, w]
    if bias is not None:
        in_specs.append(pl.BlockSpec((1, tn), lambda i, j: (0, j)))
        args.append(bias.reshape(1, N).astype(F32))
    return pl.pallas_call(
        functools.partial(_matmul_body, has_bias=bias is not None, act=act),
        grid=(M // tm, N // tn),
        in_specs=in_specs,
        out_specs=pl.BlockSpec((tm, tn), lambda i, j: (i, j)),
        out_shape=jax.ShapeDtypeStruct((M, N), out_dtype),
        compiler_params=_params("parallel", "parallel"),
        name="matmul",
    )(*args)


CONV_HALO = 8
POOL_HALO = 16


def _conv_body(xb_ref, st_ref, w_ref, b_ref, gwh_ref, gwl_ref, gb_ref, xc_ref, gate_ref, ext_ref,
               *, tb, n_li):
    c = pl.program_id(1)
    nst = CONV_W - 1
    lo = CONV_HALO - nst

    @pl.when(c == 0)
    def _():
        ext_ref[:, lo:CONV_HALO, :] = st_ref[...]

    @pl.when(c > 0)
    def _():
        ext_ref[:, lo:CONV_HALO, :] = ext_ref[:, lo + tb:CONV_HALO + tb, :]

    ext_ref[:, CONV_HALO:CONV_HALO + tb, :] = xb_ref[...].astype(F32)
    acc = b_ref[...][None] + ext_ref[:, lo:lo + tb, :] * w_ref[0:1, :][None]
    for j in range(1, CONV_W):
        acc = acc + ext_ref[:, lo + j:lo + j + tb, :] * w_ref[j:j + 1, :][None]
    xc = _silu(acc)
    xc_ref[...] = xc.astype(xc_ref.dtype)
    x2 = xc.reshape(xc.shape[0] * tb, xc.shape[2])
    hi = x2.astype(BF16)
    lo = (x2 - hi.astype(F32)).astype(BF16)
    g = _dot(hi, gwh_ref[...]) + (_dot(hi, gwl_ref[...]) + _dot(lo, gwh_ref[...])) + gb_ref[...]
    col = lax.broadcasted_iota(jnp.int32, g.shape, 1)
    g = jnp.where(col < n_li, g, _log_sigmoid(g))
    gate_ref[...] = g.reshape(gate_ref.shape)


def conv_silu_gates(xb, state, w, b, gate_w, gate_b, n_li):
    B, T, E = xb.shape
    bb, tb = (1, 256) if T % 256 == 0 else (B, T)
    gw_hi = gate_w.astype(BF16)
    gw_lo = (gate_w - gw_hi.astype(F32)).astype(BF16)
    fixed = lambda i, c: (0, 0)
    return pl.pallas_call(
        functools.partial(_conv_body, tb=tb, n_li=n_li),
        grid=(B // bb, T // tb),
        in_specs=[pl.BlockSpec((bb, tb, E), lambda i, c: (i, c, 0)),
                  pl.BlockSpec((bb, CONV_W - 1, E), lambda i, c: (i, 0, 0)),
                  pl.BlockSpec((CONV_W, E), fixed),
                  pl.BlockSpec((1, E), fixed),
                  pl.BlockSpec((E, LANES), fixed),
                  pl.BlockSpec((E, LANES), fixed),
                  pl.BlockSpec((1, LANES), fixed)],
        out_specs=[pl.BlockSpec((bb, tb, E), lambda i, c: (i, c, 0)),
                   pl.BlockSpec((bb, tb, LANES), lambda i, c: (i, c, 0))],
        out_shape=[jax.ShapeDtypeStruct((B, T, E), BF16),
                   jax.ShapeDtypeStruct((B, T, LANES), F32)],
        scratch_shapes=[pltpu.VMEM((bb, CONV_HALO + tb, E), F32)],
        compiler_params=_params("parallel", "arbitrary"),
        name="conv_silu_gates",
    )(xb, state.astype(F32), w, b.reshape(1, E), gw_hi, gw_lo, gate_b.reshape(1, LANES))


def _mlstm_body(q_ref, k_ref, v_ref, gc_ref, gr_ref, c0_ref, n0_ref, m0_ref, ng_ref,
                h_ref, c_out_ref, n_out_ref, m_out_ref, c_s, n_s, m_s, *, nh, dqk, dv, lc):
    c = pl.program_id(1)

    @pl.when(c == 0)
    def _():
        c_s[...] = c0_ref[0]
        n_s[...] = n0_ref[0]
        m_s[...] = m0_ref[0]

    row = lax.broadcasted_iota(jnp.int32, (lc, lc), 0)
    col = lax.broadcasted_iota(jnp.int32, (lc, lc), 1)
    causal = row >= col
    scale = dqk ** -0.5
    gc = gc_ref[0]
    gr = gr_ref[0]
    for h in range(nh):
        q = (q_ref[0, :, h * dqk:(h + 1) * dqk].astype(F32) * scale).astype(BF16)
        k = k_ref[0, :, h * dqk:(h + 1) * dqk]
        v = v_ref[0, :, h * dv:(h + 1) * dv]
        li_c, lf_c = gc[:, h:h + 1], gc[:, nh + h:nh + h + 1]
        li_r, lf_r = gr[h:h + 1, :], gr[nh + h:nh + h + 1, :]
        b_c = jnp.sum(jnp.where(causal, lf_r, 0.0), axis=1, keepdims=True)
        b_r = jnp.sum(jnp.where(row <= col, lf_c, 0.0), axis=0, keepdims=True)
        m_prev = m_s[h:h + 1, 0:1]
        dm = jnp.where(causal, b_c - b_r + li_r, NEG)
        m_inter = b_c + m_prev
        m_t = jnp.maximum(m_inter, jnp.max(dm, axis=1, keepdims=True))
        a = _dot_nt(q, k) * jnp.exp(dm - m_t)
        inter = jnp.exp(m_inter - m_t)
        c_prev = c_s[h]
        n_prev = n_s[h:h + 1, :]
        num = _dot(a.astype(BF16), v) + inter * _dot(q, c_prev.astype(BF16))
        qn = jnp.sum(q.astype(F32) * n_prev, axis=1, keepdims=True)
        den = jnp.sum(a, axis=1, keepdims=True) + inter * qn
        hh = num / jnp.maximum(jnp.abs(den), jnp.exp(-m_t))
        hn = hh * lax.rsqrt(jnp.mean(hh * hh, axis=-1, keepdims=True) + EPS) \
            * ng_ref[:, h * dv:(h + 1) * dv]
        h_ref[0, :, h * dv:(h + 1) * dv] = hn.astype(h_ref.dtype)
        b_last = b_c[lc - 1:lc, :]
        g_r = b_last - b_r + li_r
        g_c = b_last - b_c + li_c
        m_new = jnp.maximum(b_last + m_prev, jnp.max(g_r, axis=1, keepdims=True))
        w_r = jnp.exp(g_r - m_new)
        w_c = jnp.exp(g_c - m_new)
        decay = jnp.exp(b_last + m_prev - m_new)
        wv = (w_c * v.astype(F32)).astype(BF16)
        c_s[h] = decay * c_prev + _dot_tn(k, wv)
        wr8 = jnp.broadcast_to(w_r, (SUBLANES, lc)).astype(BF16)
        n_s[h:h + 1, :] = decay * n_prev + _dot(wr8, k)[0:1, :]
        m_s[h:h + 1, :] = jnp.broadcast_to(m_new, (1, LANES))

    @pl.when(c == pl.num_programs(1) - 1)
    def _():
        c_out_ref[0] = c_s[...]
        n_out_ref[0] = n_s[...]
        m_out_ref[0] = m_s[...]


def mlstm_recurrence(q, k, v, gcol, grow, c0, n0, m0, norm_g):
    B, T, _ = q.shape
    nh, dqk, dv = c0.shape[1], c0.shape[2], c0.shape[3]
    lc = min(MLSTM_CHUNK, T)
    m0b = jnp.broadcast_to(m0[:, :, None], (B, nh, LANES)).astype(F32)
    h, c_new, n_new, m_new = pl.pallas_call(
        functools.partial(_mlstm_body, nh=nh, dqk=dqk, dv=dv, lc=lc),
        grid=(B, T // lc),
        in_specs=[pl.BlockSpec((1, lc, nh * dqk), lambda b, c: (b, c, 0)),
                  pl.BlockSpec((1, lc, nh * dqk), lambda b, c: (b, c, 0)),
                  pl.BlockSpec((1, lc, nh * dv), lambda b, c: (b, c, 0)),
                  pl.BlockSpec((1, lc, LANES), lambda b, c: (b, c, 0)),
                  pl.BlockSpec((1, 2 * nh, lc), lambda b, c: (b, 0, c)),
                  pl.BlockSpec((1, nh, dqk, dv), lambda b, c: (b, 0, 0, 0)),
                  pl.BlockSpec((1, nh, dqk), lambda b, c: (b, 0, 0)),
                  pl.BlockSpec((1, nh, LANES), lambda b, c: (b, 0, 0)),
                  pl.BlockSpec((1, nh * dv), lambda b, c: (0, 0))],
        out_specs=[pl.BlockSpec((1, lc, nh * dv), lambda b, c: (b, c, 0)),
                   pl.BlockSpec((1, nh, dqk, dv), lambda b, c: (b, 0, 0, 0)),
                   pl.BlockSpec((1, nh, dqk), lambda b, c: (b, 0, 0)),
                   pl.BlockSpec((1, nh, LANES), lambda b, c: (b, 0, 0))],
        out_shape=[jax.ShapeDtypeStruct((B, T, nh * dv), BF16),
                   jax.ShapeDtypeStruct((B, nh, dqk, dv), F32),
                   jax.ShapeDtypeStruct((B, nh, dqk), F32),
                   jax.ShapeDtypeStruct((B, nh, LANES), F32)],
        scratch_shapes=[pltpu.VMEM((nh, dqk, dv), F32),
                        pltpu.VMEM((nh, dqk), F32),
                        pltpu.VMEM((nh, LANES), F32)],
        compiler_params=_params("parallel", "arbitrary"),
        name="mlstm_recurrence",
    )(q, k, v, gcol, grow, c0.astype(F32), n0.astype(F32), m0b, norm_g.reshape(1, nh * dv))
    return h, c_new, n_new, m_new[:, :, 0]


def _pool_body(xb_ref, st_ref, wg_ref, sc_ref, y_ref, ext_ref, *, tb, pos0, nbuf):
    c = pl.program_id(1)
    lo = POOL_HALO - nbuf
    bb = xb_ref.shape[0]
    gw = wg_ref.shape[1]

    @pl.when(c == 0)
    def _():
        ext_ref[:, lo:POOL_HALO, :] = st_ref[...]

    @pl.when(c > 0)
    def _():
        ext_ref[:, lo:POOL_HALO, :] = ext_ref[:, lo + tb:POOL_HALO + tb, :]

    ext_ref[:, POOL_HALO:POOL_HALO + tb, :] = xb_ref[...].astype(F32)
    pos = pos0 + c * tb + lax.broadcasted_iota(jnp.int32, (1, tb, 1), 1)
    for g, w in enumerate(POOL_WINDOWS):
        cols = slice(g * gw, (g + 1) * gw)
        cur = ext_ref[:, POOL_HALO:POOL_HALO + tb, cols]
        win = cur
        for i in range(1, w):
            win = win + ext_ref[:, POOL_HALO - i:POOL_HALO - i + tb, cols]
        cnt = jnp.minimum(w, pos + 1).astype(F32)
        d = (win / cnt - cur).reshape(bb * tb, gw).astype(BF16)
        yg = _dot(d, wg_ref[g]) * sc_ref[:, cols]
        y_ref[:, :, cols] = yg.reshape(bb, tb, gw).astype(y_ref.dtype)


def pool_mix(xb, state, w_grp, scale, pos0):
    B, T, E = xb.shape
    nbuf = state.shape[1]
    bb, tb = (1, 256) if T % 256 == 0 else (B, T)
    ng, gw = w_grp.shape[0], w_grp.shape[1]
    return pl.pallas_call(
        functools.partial(_pool_body, tb=tb, pos0=pos0, nbuf=nbuf),
        grid=(B // bb, T // tb),
        in_specs=[pl.BlockSpec((bb, tb, E), lambda i, c: (i, c, 0)),
                  pl.BlockSpec((bb, nbuf, E), lambda i, c: (i, 0, 0)),
                  pl.BlockSpec((ng, gw, gw), lambda i, c: (0, 0, 0)),
                  pl.BlockSpec((1, E), lambda i, c: (0, 0))],
        out_specs=pl.BlockSpec((bb, tb, E), lambda i, c: (i, c, 0)),
        out_shape=jax.ShapeDtypeStruct((B, T, E), BF16),
        scratch_shapes=[pltpu.VMEM((bb, POOL_HALO + tb, E), F32)],
        compiler_params=_params("parallel", "arbitrary"),
        name="pool_mix",
    )(xb, state.astype(F32), w_grp, scale.reshape(1, E))


def _tail_body(*refs, kind):
    if kind == "mlstm":
        hn_ref, o_ref, xc_ref, z_ref, skip_ref = refs[:5]
        rest = refs[5:]
        mix = (o_ref[...].astype(F32) * hn_ref[...].astype(F32)
               + skip_ref[...] * xc_ref[...].astype(F32)) * _silu(z_ref[...].astype(F32))
    else:
        y_ref, z_ref = refs[:2]
        rest = refs[2:]
        mix = y_ref[...].astype(F32) * _silu(z_ref[...].astype(F32))
    x_ref, p_ref, wd_ref, gw_ref, pw_ref, ng_ref, xo_ref, no_ref = rest
    x1 = x_ref[...] + _dot(mix.astype(BF16), wd_ref[...])
    gate = _sigmoid(_dot(x1.astype(BF16), gw_ref[...]))
    x2 = x1 + gate * _dot(p_ref[...].astype(BF16), pw_ref[...])
    xo_ref[...] = x2
    no_ref[...] = (x2 * lax.rsqrt(jnp.mean(x2 * x2, axis=-1, keepdims=True) + EPS)
                   * ng_ref[...]).astype(no_ref.dtype)


def layer_tail(kind, mix_inputs, x, p, w_down, gate_w, ple_w, next_g, next_dtype, skip=None):
    M, D = x.shape
    tm = _pick(M, (512, 256))
    row = lambda i: (i, 0)
    fixed = lambda i: (0, 0)
    in_specs, args = [], []
    for a in mix_inputs:
        in_specs.append(pl.BlockSpec((tm, a.shape[1]), row))
        args.append(a)
    if kind == "mlstm":
        in_specs.append(pl.BlockSpec((1, skip.shape[-1]), fixed))
        args.append(skip.reshape(1, -1))
    in_specs += [pl.BlockSpec((tm, D), row), pl.BlockSpec((tm, p.shape[1]), row),
                 pl.BlockSpec(w_down.shape, fixed), pl.BlockSpec(gate_w.shape, fixed),
                 pl.BlockSpec(ple_w.shape, fixed), pl.BlockSpec((1, D), fixed)]
    args += [x, p, w_down, gate_w, ple_w, next_g.reshape(1, D)]
    out_specs = [pl.BlockSpec((tm, D), row), pl.BlockSpec((tm, D), row)]
    out_shape = [jax.ShapeDtypeStruct((M, D), F32), jax.ShapeDtypeStruct((M, D), next_dtype)]
    return pl.pallas_call(
        functools.partial(_tail_body, kind=kind),
        grid=(M // tm,),
        in_specs=in_specs, out_specs=out_specs, out_shape=out_shape,
        compiler_params=_params("parallel"),
        name="layer_tail_" + kind,
    )(*args)


def _sortable_key(s):
    s = jnp.where(s == 0.0, 0.0, s)
    bits = lax.bitcast_convert_type(s, jnp.int32)
    return jnp.where(bits < 0, bits ^ jnp.int32(0x7FFFFFFF), bits)


COUNT_STRIP_VREGS = 32


def _count(keys_ref, nvalid, pred):
    rows, cw = keys_ref.shape[1], keys_ref.shape[2]
    strip = max(SUBLANES, min(rows, COUNT_STRIP_VREGS * SUBLANES * LANES // cw))

    def body(kc, acc):
        parts = []
        for r0 in range(0, rows, strip):
            rs = slice(r0, min(r0 + strip, rows))
            hit = jnp.where(pred(keys_ref[kc, rs, :], kc, rs), 1.0, 0.0)
            part = hit[:, 0:LANES]
            for g in range(1, cw // LANES):
                part = part + hit[:, g * LANES:(g + 1) * LANES]
            parts.append(part)
        return acc + (parts[0] if len(parts) == 1 else jnp.concatenate(parts, axis=0))

    acc = lax.fori_loop(0, nvalid, body, jnp.zeros((rows, LANES), F32))
    return jnp.sum(acc, axis=1, keepdims=True)


def _select_topk(keys_ref, cidx_ref, nvalid, k_top, idx_bits):
    rows, cw = keys_ref.shape[1], keys_ref.shape[2]
    kf = float(k_top)
    imin = jnp.int32(INT_MIN)

    def bit_body(i, tau_u):
        cand_u = tau_u | jnp.left_shift(jnp.int32(1), 31 - i)
        cand = cand_u ^ imin
        cnt = _count(keys_ref, nvalid, lambda kk, kc, rs: kk >= cand[rs])
        return jnp.where(cnt >= kf, cand_u, tau_u)

    tau_u = lax.fori_loop(0, 32, bit_body, jnp.zeros((rows, 1), jnp.int32))
    tau = tau_u ^ imin
    n_gt = _count(keys_ref, nvalid, lambda kk, kc, rs: kk > tau[rs])
    n_ge = _count(keys_ref, nvalid, lambda kk, kc, rs: kk >= tau[rs])
    need = kf - n_gt
    short = tau_u == 0
    cidx_ref[...] = jnp.broadcast_to(jnp.where(short, -1, 2 ** 30), cidx_ref.shape)
    excess = jnp.max(jnp.where((n_ge > kf) & jnp.logical_not(short), 1.0, 0.0))

    @pl.when(excess > 0.5)
    def _():
        lane = lax.broadcasted_iota(jnp.int32, (1, cw), 1)

        def idx_body(i, cut):
            cand = cut | jnp.left_shift(jnp.int32(1), idx_bits - 1 - i)
            cnt = _count(keys_ref, nvalid,
                         lambda kk, kc, rs: (kk == tau[rs]) & ((kc * cw + lane) < cand[rs]))
            return jnp.where(cnt < need, cand, cut)

        cut = lax.fori_loop(0, idx_bits, idx_body, jnp.zeros((rows, 1), jnp.int32))
        cidx_ref[...] = jnp.broadcast_to(jnp.where(short, -1, cut), cidx_ref.shape)

    return tau


def _selected(kk, colv, tau, cut):
    return (kk > tau) | ((kk == tau) & (colv <= cut))


def _n_causal_chunks(qb, tq, tk):
    return lax.div(qb * tq + (tq - 1), jnp.int32(tk)) + 1


def _idx_prompt_body(qi_ref, wi_ref, ki_ref, bias_ref, keys_ref, cidx_ref, *,
                     nh, dh, tq, tk, nk, k_top, idx_bits):
    per = keys_ref.shape[2] // tk
    gw = per * tk
    qb = pl.program_id(1)
    nvalid = _n_causal_chunks(qb, tq, gw)
    rowpos = qb * tq + lax.broadcasted_iota(jnp.int32, (tq, 1), 0)
    lane = lax.broadcasted_iota(jnp.int32, (1, tk), 1)
    wi = wi_ref[0]

    def score_group(gc, carry):
        for part in range(per):
            kt = ki_ref[0, gc * per + part]
            s = jnp.zeros((tq, tk), F32)
            for h in range(nh):
                rel = jnp.maximum(_dot_nt(qi_ref[0, :, h * dh:(h + 1) * dh], kt), 0.0)
                s = s + rel * wi[:, h:h + 1]
            colv = gc * gw + part * tk + lane
            keys_ref[gc, :, part * tk:(part + 1) * tk] = jnp.where(
                colv <= rowpos, _sortable_key(s), jnp.int32(INT_MIN))
        return carry

    lax.fori_loop(0, nvalid, score_group, 0)
    tau = _select_topk(keys_ref, cidx_ref, nvalid, k_top, idx_bits)
    cut = cidx_ref[:, 0:1]

    def write_group(gc, carry):
        for part in range(per):
            kk = keys_ref[gc, :, part * tk:(part + 1) * tk]
            sel = _selected(kk, gc * gw + part * tk + lane, tau, cut)
            bias_ref[0, 0, gc * per + part] = jnp.where(sel, 0.0, NEG).astype(bias_ref.dtype)
        return carry

    lax.fori_loop(0, nvalid, write_group, 0)

    def fill_chunk(kc, carry):
        bias_ref[0, 0, kc] = jnp.full((tq, tk), NEG, bias_ref.dtype)
        return carry

    lax.fori_loop(nvalid * per, nk, fill_chunk, 0)


def indexer_bias_prompt(qi, wi, ki, k_top):
    B, T, _ = qi.shape
    dh = ki.shape[-1]
    nh = qi.shape[-1] // dh
    tq, tk = min(IDX_TQ, T), min(ATT_TK, T)
    tqa = min(ATT_TQ, T)
    sub = tqa // tq
    nq, nk = T // tq, T // tk
    per = 2 if nk % 2 == 0 else 1
    idx_bits = int(T).bit_length() + 1
    return pl.pallas_call(
        functools.partial(_idx_prompt_body, nh=nh, dh=dh, tq=tq, tk=tk, nk=nk, k_top=k_top,
                          idx_bits=idx_bits),
        grid=(B, nq),
        in_specs=[pl.BlockSpec((1, tq, nh * dh), lambda b, i: (b, i, 0)),
                  pl.BlockSpec((1, tq, LANES), lambda b, i: (b, i, 0)),
                  pl.BlockSpec((1, nk, tk, dh), lambda b, i: (b, 0, 0, 0))],
        out_specs=pl.BlockSpec((1, 1, nk, tq, tk), lambda b, i: (b, i // sub, 0, i % sub, 0)),
        out_shape=jax.ShapeDtypeStruct((B, T // tqa, nk, tqa, tk), BF16),
        scratch_shapes=[pltpu.VMEM((nk // per, tq, per * tk), jnp.int32),
                        pltpu.VMEM((tq, LANES), jnp.int32)],
        compiler_params=_params("parallel", "arbitrary"),
        name="indexer_bias_prompt",
    )(qi, wi, ki.reshape(B, nk, tk, dh))


def _attn_prompt_body(q_ref, k_ref, v_ref, bias_ref, o_ref, m_s, acc_s, *, nh, dh, tq, tk):
    qb, kb = pl.program_id(1), pl.program_id(2)
    nvalid = _n_causal_chunks(qb, tq, tk)

    @pl.when(kb == 0)
    def _():
        m_s[...] = jnp.full(m_s.shape, NEG, F32)
        acc_s[...] = jnp.zeros(acc_s.shape, F32)

    @pl.when(kb < nvalid)
    def _():
        bias = bias_ref[0, 0, 0].astype(F32)
        for h in range(nh):
            s = _dot_nt(q_ref[0, h], k_ref[0, h]) + bias
            m_prev = m_s[h]
            m_new = jnp.maximum(m_prev, jnp.max(s, axis=1, keepdims=True))
            alpha = jnp.exp2(m_prev - m_new)
            p = jnp.exp2(s - jnp.concatenate([m_new] * (tk // LANES), axis=1))
            acc_s[h] = alpha * acc_s[h] + _dot(p.astype(BF16), v_ref[0, h])
            m_s[h] = m_new

    @pl.when(kb == pl.num_programs(2) - 1)
    def _():
        for h in range(nh):
            acc = acc_s[h]
            o_ref[0, :, h * dh:(h + 1) * dh] = (acc[:, :dh] / acc[:, dh:dh + 1]).astype(o_ref.dtype)


def attention_prompt(q, k, v, bias):
    B, nh, T, dh = q.shape
    assert tuple(v.shape) == (B, nh, T, LANES) and dh < LANES
    _, nq, nk, tq, tk = bias.shape

    def kv_map(b, i, j):
        return (b, 0, jnp.minimum(j, _n_causal_chunks(i, tq, tk) - 1), 0)

    def bias_map(b, i, j):
        return (b, i, jnp.minimum(j, _n_causal_chunks(i, tq, tk) - 1), 0, 0)

    return pl.pallas_call(
        functools.partial(_attn_prompt_body, nh=nh, dh=dh, tq=tq, tk=tk),
        grid=(B, nq, nk),
        in_specs=[pl.BlockSpec((1, nh, tq, dh), lambda b, i, j: (b, 0, i, 0)),
                  pl.BlockSpec((1, nh, tk, dh), kv_map),
                  pl.BlockSpec((1, nh, tk, LANES), kv_map),
                  pl.BlockSpec((1, 1, 1, tq, tk), bias_map)],
        out_specs=pl.BlockSpec((1, tq, nh * dh), lambda b, i, j: (b, i, 0)),
        out_shape=jax.ShapeDtypeStruct((B, T, nh * dh), BF16),
        scratch_shapes=[pltpu.VMEM((nh, tq, LANES), F32),
                        pltpu.VMEM((nh, tq, LANES), F32)],
        compiler_params=_params("parallel", "parallel", "arbitrary"),
        name="attention_prompt",
    )(q, k, v, bias)


def _idx_sample_body(pt_ref, qi_ref, wi_ref, knew_ref, *rest, nh, ts, npg, nsteps, k_top,
                     idx_bits):
    page_refs = rest[:npg]
    bias_ref, keys_ref, cidx_ref = rest[npg:]
    s = pl.program_id(1)
    cw = keys_ref.shape[2]
    wi = wi_ref[0]
    qi = qi_ref[0]

    def scores(kt):
        full = jnp.maximum(_dot(qi, kt), 0.0)
        out = jnp.zeros((ts, cw), F32)
        for h in range(nh):
            out = out + full[h * ts:(h + 1) * ts, :] * wi[:, h:h + 1]
        return out

    kt = jnp.concatenate([r[...] for r in page_refs], axis=1).astype(BF16)
    keys_ref[s] = _sortable_key(scores(kt))

    @pl.when(s == nsteps - 1)
    def _():
        lane = lax.broadcasted_iota(jnp.int32, (1, cw), 1)
        past = nsteps * cw
        rowpos = past + lax.broadcasted_iota(jnp.int32, (ts, 1), 0)
        valid = (past + lane) <= rowpos
        keys_ref[nsteps] = jnp.where(valid, _sortable_key(scores(knew_ref[0])),
                                     jnp.int32(INT_MIN))
        tau = _select_topk(keys_ref, cidx_ref, nsteps + 1, k_top, idx_bits)
        cut = cidx_ref[:, 0:1]
        for kc in range(nsteps + 1):
            sel = _selected(keys_ref[kc], kc * cw + lane, tau, cut)
            bias_ref[0, kc] = jnp.where(sel, 0.0, NEG).astype(bias_ref.dtype)


def _page_specs(n_pages_per_step, block, layer, first_step):
    specs = []
    for i in range(n_pages_per_step):
        def imap(b, s, pt, i=i):
            step = jnp.maximum(s - first_step, 0)
            return (layer, pt[b, step * n_pages_per_step + i]) + (0,) * len(block)
        specs.append(pl.BlockSpec((None, None) + tuple(block), imap))
    return specs


def indexer_bias_sample(qi_hq, wi, ki_new_t, cache_kidx_t, layer, page_table, k_top):
    B, n_pages = page_table.shape
    _, _, dh, page = cache_kidx_t.shape
    ts = wi.shape[1]
    nh = qi_hq.shape[1] // ts
    npg = min(IDX_PAGES_PER_STEP, n_pages)
    nsteps = n_pages // npg
    cw = npg * page
    idx_bits = int(n_pages * page + cw).bit_length() + 1
    grid_spec = pltpu.PrefetchScalarGridSpec(
        num_scalar_prefetch=1,
        grid=(B, nsteps),
        in_specs=[pl.BlockSpec((1, nh * ts, dh), lambda b, s, pt: (b, 0, 0)),
                  pl.BlockSpec((1, ts, LANES), lambda b, s, pt: (b, 0, 0)),
                  pl.BlockSpec((1, dh, cw), lambda b, s, pt: (b, 0, 0))]
        + _page_specs(npg, (dh, page), layer, 0),
        out_specs=pl.BlockSpec((1, nsteps + 1, ts, cw), lambda b, s, pt: (b, 0, 0, 0)),
        scratch_shapes=[pltpu.VMEM((nsteps + 1, ts, cw), jnp.int32),
                        pltpu.VMEM((ts, LANES), jnp.int32)],
    )
    return pl.pallas_call(
        functools.partial(_idx_sample_body, nh=nh, ts=ts, npg=npg, nsteps=nsteps, k_top=k_top,
                          idx_bits=idx_bits),
        grid_spec=grid_spec,
        out_shape=jax.ShapeDtypeStruct((B, nsteps + 1, ts, cw), F32),
        compiler_params=_params("parallel", "arbitrary"),
        name="indexer_bias_sample",
    )(page_table, qi_hq, wi, ki_new_t, *([cache_kidx_t] * npg))


def _attn_sample_body(pt_ref, qbd_ref, knew_ref, vnew_ref, bias_ref, *rest, nh, ts, npg, dh):
    kpages, vpages = rest[:npg], rest[npg:2 * npg]
    o_ref, m_s, l_s, acc_s = rest[2 * npg:]
    s = pl.program_id(1)
    rows = nh * ts

    @pl.when(s == 0)
    def _():
        m_s[...] = jnp.full(m_s.shape, NEG, F32)
        l_s[...] = jnp.zeros(l_s.shape, F32)
        acc_s[...] = jnp.zeros(acc_s.shape, F32)

    def update(kt, vt):
        n = kt.shape[1]
        bias = bias_ref[0, 0][:, :n]
        logits = _dot(qbd_ref[0], kt) + jnp.concatenate([bias] * nh, axis=0)
        m_prev = m_s[...]
        m_new = jnp.maximum(m_prev, jnp.max(logits, axis=1, keepdims=True))
        alpha = jnp.exp2(m_prev - m_new)
        p = jnp.exp2(logits - m_new)
        l_s[...] = alpha * l_s[...] + jnp.sum(p, axis=1, keepdims=True)
        acc_s[...] = alpha * acc_s[...] + _dot_nt(p.astype(BF16), vt)
        m_s[...] = m_new

    @pl.when(s == 0)
    def _():
        update(knew_ref[0], vnew_ref[0])

    def gather_pages(page_refs):
        return jnp.concatenate([r[...].reshape(nh * dh, r.shape[2]) for r in page_refs],
                               axis=1).astype(BF16)

    @pl.when(s > 0)
    def _():
        update(gather_pages(kpages), gather_pages(vpages))

    @pl.when(s == pl.num_programs(1) - 1)
    def _():
        o = acc_s[...] / l_s[...]
        lane = lax.broadcasted_iota(jnp.int32, (1, nh * dh), 1)
        res = jnp.zeros((ts, nh * dh), F32)
        for h in range(nh):
            own = (lane >= h * dh) & (lane < (h + 1) * dh)
            res = res + jnp.where(own, o[h * ts:(h + 1) * ts, :], 0.0)
        o_ref[0] = res.astype(o_ref.dtype)


def attention_sample(qbd, k_new_t, v_new_t, bias, cache_k_t, cache_v_t, layer, page_table, ts):
    B, n_pages = page_table.shape
    _, _, nh, dh, page = cache_k_t.shape
    width = nh * dh
    n_new = k_new_t.shape[2]
    npg = min(PAGES_PER_STEP, n_pages)
    cw = npg * page
    nsteps1 = n_pages // npg + 1
    per_chunk = bias.shape[3] // cw
    last_chunk = bias.shape[1] - 1
    rows = nh * ts

    def bias_map(b, s, pt):
        g = jnp.maximum(s - 1, 0)
        return (b, jnp.where(s == 0, last_chunk, g // per_chunk), 0,
                jnp.where(s == 0, 0, g % per_chunk))

    grid_spec = pltpu.PrefetchScalarGridSpec(
        num_scalar_prefetch=1,
        grid=(B, nsteps1),
        in_specs=[pl.BlockSpec((1, rows, width), lambda b, s, pt: (b, 0, 0)),
                  pl.BlockSpec((1, width, n_new), lambda b, s, pt: (b, 0, 0)),
                  pl.BlockSpec((1, width, n_new), lambda b, s, pt: (b, 0, 0)),
                  pl.BlockSpec((1, 1, ts, cw), bias_map)]
        + _page_specs(npg, (nh, dh, page), layer, 1) + _page_specs(npg, (nh, dh, page), layer, 1),
        out_specs=pl.BlockSpec((1, ts, width), lambda b, s, pt: (b, 0, 0)),
        scratch_shapes=[pltpu.VMEM((rows, 1), F32),
                        pltpu.VMEM((rows, 1), F32),
                        pltpu.VMEM((rows, width), F32)],
    )
    return pl.pallas_call(
        functools.partial(_attn_sample_body, nh=nh, ts=ts, npg=npg, dh=dh),
        grid_spec=grid_spec,
        out_shape=jax.ShapeDtypeStruct((B, ts, width), BF16),
        compiler_params=_params("parallel", "arbitrary"),
        name="attention_sample",
    )(page_table, qbd, k_new_t, v_new_t, bias, *([cache_k_t] * npg), *([cache_v_t] * npg))


def _pad_time(a, t_pad, value=0.0):
    return jnp.pad(a, ((0, 0), (0, t_pad - a.shape[1]), (0, 0)), constant_values=value)


def _mlstm_layer(xn, B, T, state, w, j):
    c0, n0, m0, conv0 = state
    nh = c0.shape[1]
    E = w["a_conv_w"].shape[-1]
    w_up = w["a_w_up"][j]
    xb, = project(xn,w_up[:, :E], (BF16,))
    z, = project(xn,w_up[:, E:], (BF16,))
    xc, gcol = conv_silu_gates(xb.reshape(B, T, E), conv0, w["a_conv_w"][j], w["a_conv_b"][j],
                               w["a_w_if_pad"][j], w["a_b_if_pad"][j], nh)
    xc = xc.reshape(B * T, E)
    q = matmul(xc, w["a_w_q"][j], BF16)
    k = matmul(xc, w["a_w_k"][j], BF16)
    v = matmul(xb, w["a_w_v"][j], BF16)
    o = matmul(xb, w["a_w_o"][j], BF16, bias=w["a_b_o"][j], act="sigmoid")
    q3, k3, v3 = q.reshape(B, T, -1), k.reshape(B, T, -1), v.reshape(B, T, -1)
    tp = T if T % LANES == 0 else ((T + LANES - 1) // LANES) * LANES
    if tp != T:
        q3, k3, v3 = _pad_time(q3, tp), _pad_time(k3, tp), _pad_time(v3, tp)
        pad_row = jnp.where(jnp.arange(LANES) < nh, NEG, 0.0).astype(F32)
        gcol = jnp.concatenate([gcol, jnp.broadcast_to(pad_row, (B, tp - T, LANES))], axis=1)
    grow = jnp.swapaxes(gcol[:, :, :2 * nh], 1, 2)
    hn, c_new, n_new, m_new = mlstm_recurrence(q3, k3, v3, gcol, grow, c0, n0, m0, w["a_norm_g"][j])
    hn = hn[:, :T].reshape(B * T, -1)
    assert T >= CONV_W - 1
    conv_new = xb.reshape(B, T, E)[:, T - (CONV_W - 1):].astype(F32)
    return (hn, o, xc, z), (c_new, n_new, m_new, conv_new)


def _rope_tables(pos):
    half = 32
    inv = ROPE_THETA ** (-np.arange(half, dtype=np.float64) / half)
    ang = np.asarray(pos, np.float64)[:, None] * inv[None, :]
    cos = np.tile(np.cos(ang), (1, 4))
    sin = np.tile(np.concatenate([-np.sin(ang), np.sin(ang)], axis=1), (1, 2))
    return jnp.asarray(cos, F32), jnp.asarray(sin, F32)


def _dsa_project(xn, B, T, pos, w, j, nh, dh, nhi, di, head_major=False):
    M = B * T
    cos, sin = _rope_tables(pos)
    tm = _pick(T, ROW_TILES) if T % ROW_TILES[-1] == 0 else _pick(M, ROW_TILES)
    if T % tm == 0:
        tabs, tab_blocks = (cos, sin), T // tm
    else:
        tabs, tab_blocks = (jnp.tile(cos, (M // T, 1)), jnp.tile(sin, (M // T, 1))), M // tm
    rope = dict(rope_tabs=tabs, tab_blocks=tab_blocks, row_tile=tm)
    hm = (lambda *widths: (B, T, dh, widths)) if head_major else (lambda *widths: None)
    q, = project(xn,w["b_wq"][j], (BF16,), scale=dh ** -0.5 * math.log2(math.e),
                     head_major=hm(dh), **rope)
    wide = "T" if head_major else 0
    k32, k16 = project(xn,w["b_wk"][j], (F32, BF16), head_major=hm(wide, dh), **rope)
    v32, v16 = project(xn,w["b_wv"][j], (F32, BF16), head_major=hm(wide, LANES),
                           row_tile=tm)
    gate, = project(xn,w["b_wg"][j], (BF16,))
    qi, = project(xn,w["b_wqi"][j], (BF16,), **rope)
    ki32, ki16 = project(xn,w["b_wki_pad"][j], (F32, BF16), head_major=hm(wide, 0), **rope)
    wi, = project(xn,w["b_wwi_pad"][j], (F32,), scale=nhi ** -0.5 * di ** -0.5)
    if head_major:
        new = (jnp.transpose(k32.reshape(B, nh, dh, T), (0, 3, 1, 2)),
               jnp.transpose(v32.reshape(B, nh, dh, T), (0, 3, 1, 2)),
               jnp.transpose(ki32[:, :di], (0, 2, 1)))
    else:
        new = (k32.reshape(B, T, nh, dh), v32.reshape(B, T, nh, dh), ki32[:, :di].reshape(B, T, di))
    return q, k16, v16, gate, qi, ki16[:, :di].reshape(B, T, di), wi, new


def _dsa_layer_prompt(xn, B, T, w, j, dims):
    nh, dh, nhi, di = dims
    q, k16, v16, gate, qi, ki16, wi, new = _dsa_project(
        xn, B, T, np.arange(T), w, j, *dims, head_major=True)
    k_top = min(TOPK_MAX, T // 4)
    bias = indexer_bias_prompt(qi.reshape(B, T, nhi * di), wi.reshape(B, T, LANES), ki16, k_top)
    attn = attention_prompt(q, k16, v16, bias).reshape(B * T, nh * dh)
    return (attn, gate), new


def _dsa_layer_sample(xn, B, T, w, j, dims, ctx):
    nh, dh, nhi, di = dims
    cache_k, cache_v, cache_kidx, page_table = ctx
    n_pages = page_table.shape[1]
    page = cache_kidx.shape[2]
    past = n_pages * page
    q, k16, v16, gate, qi, ki16, wi, new = _dsa_project(
        xn, B, T, past + np.arange(T), w, j, *dims)
    k_top = min(TOPK_MAX, (past + T) // 4)
    cw = min(IDX_PAGES_PER_STEP, n_pages) * page
    qi_hq = jnp.transpose(qi.reshape(B, T, nhi, di), (0, 2, 1, 3)).reshape(B, nhi * T, di)
    token_last = lambda a, n: jnp.swapaxes(_pad_time(a, n), 1, 2)
    ki_new_t = token_last(ki16, cw)
    bias = indexer_bias_sample(qi_hq, wi.reshape(B, T, LANES), ki_new_t,
                               jnp.swapaxes(cache_kidx, 2, 3), j, page_table, k_top)
    q3 = q.reshape(B, T, nh * dh)
    own = (jnp.arange(nh * T)[:, None] // T) == (jnp.arange(nh * dh)[None, :] // dh)
    qbd = jnp.where(own[None], jnp.tile(q3, (1, nh, 1)), 0).astype(BF16)
    width = nh * dh
    n_new = ((T + LANES - 1) // LANES) * LANES
    attn = attention_sample(qbd, token_last(k16.reshape(B, T, width), n_new),
                            token_last(v16.reshape(B, T, width), n_new), bias,
                            jnp.transpose(cache_k, (0, 1, 3, 4, 2)),
                            jnp.transpose(cache_v, (0, 1, 3, 4, 2)), j, page_table, T)
    return (attn.reshape(B * T, width), gate), new


def _pool_layer(xn, B, T, buf, pos0, w, j):
    E = w["c_scale"].shape[-1]
    w_up = w["c_w_up"][j]
    xb, = project(xn,w_up[:, :E], (BF16,))
    z, = project(xn,w_up[:, E:], (BF16,))
    xb3 = xb.reshape(B, T, E)
    y = pool_mix(xb3, buf, w["c_w_grp"][j], w["c_scale"][j], pos0).reshape(B * T, E)
    nbuf = buf.shape[1]
    if T >= nbuf:
        new_buf = xb3[:, T - nbuf:].astype(F32)
    else:
        new_buf = jnp.concatenate([buf[:, T:].astype(F32), xb3.astype(F32)], axis=1)
    return (y, z), new_buf


def _run_group(x3, p4, pos0, mlstm_state, dsa_ctx, pool_state, w, dims):
    B, T, D = x3.shape
    depth = p4.shape[0]
    x = x3.reshape(B * T, D)
    new_a, new_b, new_c = [], [], []
    xn = rmsnorm_cast(x, w["norm_g"][0])
    for i in range(depth):
        kind, j = i % 3, i // 3
        last = i == depth - 1
        nxt = dict(next_g=w["final_g"] if last else w["norm_g"][i + 1],
                   next_dtype=F32 if last else BF16)
        p = p4[i].reshape(B * T, -1)
        if kind == 0:
            state = tuple(s[j] for s in mlstm_state)
            mix, st = _mlstm_layer(xn, B, T, state, w, j)
            new_a.append(st)
            x, xn = layer_tail("mlstm", mix, x, p, w["a_w_down"][j], w["ple_gate_w"][i],
                               w["ple_w"][i], skip=w["a_skip"][j], **nxt)
        elif kind == 1:
            if dsa_ctx is None:
                mix, st = _dsa_layer_prompt(xn, B, T, w, j, dims)
            else:
                mix, st = _dsa_layer_sample(xn, B, T, w, j, dims, dsa_ctx)
            new_b.append(st)
            x, xn = layer_tail("gated", mix, x, p, w["b_w_out"][j], w["ple_gate_w"][i],
                               w["ple_w"][i], **nxt)
        else:
            mix, st = _pool_layer(xn, B, T, pool_state[j], pos0, w, j)
            new_c.append(st)
            x, xn = layer_tail("gated", mix, x, p, w["c_w_down"][j], w["ple_gate_w"][i],
                               w["ple_w"][i], **nxt)
    a_states = tuple(jnp.stack([s[r] for s in new_a]) for r in range(4))
    b_states = tuple(jnp.stack([s[r] for s in new_b]) for r in range(3))
    c_state = jnp.stack(new_c)
    return xn.reshape(B, T, D), a_states, b_states, c_state


def kernel(x_prompt, x_sample, state_mlstm_C, state_mlstm_n, state_mlstm_m, state_mlstm_conv, state_pool,
           cache_k, cache_v, cache_kidx, page_table, p_prompt, p_sample,
           norm_g, final_g, ple_w, ple_gate_w,
           a_w_up, a_conv_w, a_conv_b, a_w_q, a_w_k, a_w_v, a_w_if, a_b_if, a_w_o, a_b_o, a_norm_g,
           a_skip, a_w_down, b_w_in, b_w_out, c_w_up, c_w_grp, c_scale, c_w_down):
    D = x_prompt.shape[-1]
    nh, dh = cache_k.shape[3], cache_k.shape[4]
    di = cache_kidx.shape[-1]
    aw = nh * dh
    nhi = (b_w_in.shape[-1] - 4 * aw - di) // (di + 1)
    dims = (nh, dh, nhi, di)
    nha = state_mlstm_C.shape[2]
    bf = lambda a: a.astype(BF16)

    def pad_cols(a, n):
        return jnp.pad(a, [(0, 0)] * (a.ndim - 1) + [(0, n - a.shape[-1])])

    o1, o2, o3, o4, o5, o6 = np.cumsum([aw, aw, aw, aw, nhi * di, di])
    w = dict(
        norm_g=norm_g, final_g=final_g, ple_w=bf(ple_w), ple_gate_w=bf(ple_gate_w),
        a_w_up=bf(a_w_up), a_conv_w=a_conv_w, a_conv_b=a_conv_b,
        a_w_q=bf(a_w_q), a_w_k=bf(a_w_k), a_w_v=bf(a_w_v), a_w_o=bf(a_w_o), a_b_o=a_b_o,
        a_w_if_pad=pad_cols(a_w_if, LANES), a_b_if_pad=pad_cols(a_b_if, LANES),
        a_norm_g=a_norm_g, a_skip=a_skip, a_w_down=bf(a_w_down),
        b_wq=bf(b_w_in[..., :o1]), b_wk=bf(b_w_in[..., o1:o2]), b_wv=bf(b_w_in[..., o2:o3]),
        b_wg=bf(b_w_in[..., o3:o4]), b_wqi=bf(b_w_in[..., o4:o5]),
        b_wki_pad=bf(pad_cols(b_w_in[..., o5:o6], LANES)),
        b_wwi_pad=bf(pad_cols(b_w_in[..., o6:], LANES)),
        b_w_out=bf(b_w_out), c_w_up=bf(c_w_up), c_w_grp=bf(c_w_grp), c_scale=c_scale,
        c_w_down=bf(c_w_down),
    )

    Bp = x_prompt.shape[0]
    na, nc = state_mlstm_C.shape[0], state_pool.shape[0]
    zeros_a = (jnp.zeros((na, Bp) + state_mlstm_C.shape[2:], F32),
               jnp.zeros((na, Bp) + state_mlstm_n.shape[2:], F32),
               jnp.zeros((na, Bp) + state_mlstm_m.shape[2:], F32),
               jnp.zeros((na, Bp) + state_mlstm_conv.shape[2:], F32))
    zeros_c = jnp.zeros((nc, Bp) + state_pool.shape[2:], F32)
    y_p, a_p, b_p, c_p = _run_group(x_prompt, p_prompt, 0, zeros_a, None, zeros_c, w, dims)

    past = page_table.shape[1] * cache_k.shape[2]
    y_s, a_s, b_s, c_s = _run_group(
        x_sample, p_sample, past,
        (state_mlstm_C, state_mlstm_n, state_mlstm_m, state_mlstm_conv),
        (cache_k, cache_v, cache_kidx, page_table), state_pool, w, dims)
    return (y_p, y_s, *a_p, *b_p, c_p, *a_s, *b_s, c_s)
```

```python
import functools
import math

import numpy as np
import jax
import jax.numpy as jnp
from jax import lax
from jax.experimental import pallas as pl
from jax.experimental.pallas import tpu as pltpu

EPS = 1e-6
ROPE_THETA = 10000.0
TOPK_MAX = 256
POOL_WINDOWS = (2, 4, 8, 16)
CONV_W = 4
MLSTM_CHUNK = 256
NEG = -1e30
LANES = 128
SUBLANES = 8
VMEM_LIMIT_BYTES = 56 * 1024 * 1024
ATT_TQ, ATT_TK = 1024, 512
IDX_TQ = 256
PAGES_PER_STEP = 8
IDX_PAGES_PER_STEP = 32
ROW_TILES = (1024, 512, 256)
COL_TILES = (1024, 512, 256, 128)

F32 = jnp.float32
BF16 = jnp.bfloat16
INT_MIN = -2 ** 31


def _params(*sem):
    return pltpu.CompilerParams(dimension_semantics=sem, vmem_limit_bytes=VMEM_LIMIT_BYTES)


def _sigmoid(x):
    return 1.0 / (1.0 + jnp.exp(-x))


def _silu(x):
    return x * _sigmoid(x)


def _log_sigmoid(x):
    return jnp.minimum(x, 0.0) - jnp.log(1.0 + jnp.exp(-jnp.abs(x)))


def _dot(a, b):
    return jnp.dot(a, b, preferred_element_type=F32)


def _dot_nt(a, b):
    return lax.dot_general(a, b, (((1,), (1,)), ((), ())), preferred_element_type=F32)


def _dot_tn(a, b):
    return lax.dot_general(a, b, (((0,), (0,)), ((), ())), preferred_element_type=F32)


def _pick(n, cands):
    for c in cands:
        if n % c == 0:
            return c
    return n


def _rope_tile(a, cos, sin_signed):
    lane = lax.broadcasted_iota(jnp.int32, (1, LANES), 1)
    first_half = (lane % 64) < 32
    pieces = []
    for gi in range(a.shape[1] // LANES):
        ag = a[:, gi * LANES:(gi + 1) * LANES]
        ahead = pltpu.roll(ag, LANES - 32, 1)
        behind = pltpu.roll(ag, 32, 1)
        rot = jnp.where(first_half, ahead, behind)
        pieces.append(ag * cos + rot * sin_signed)
    return pieces[0] if len(pieces) == 1 else jnp.concatenate(pieces, axis=1)


def _rmsnorm_body(x_ref, g_ref, o_ref):
    xf = x_ref[...]
    y = xf * lax.rsqrt(jnp.mean(xf * xf, axis=-1, keepdims=True) + EPS) * g_ref[...]
    o_ref[...] = y.astype(o_ref.dtype)


def rmsnorm_cast(x, g):
    M, D = x.shape
    tm = _pick(M, ROW_TILES)
    return pl.pallas_call(
        _rmsnorm_body,
        grid=(M // tm,),
        in_specs=[pl.BlockSpec((tm, D), lambda i: (i, 0)), pl.BlockSpec((1, D), lambda i: (0, 0))],
        out_specs=pl.BlockSpec((tm, D), lambda i: (i, 0)),
        out_shape=jax.ShapeDtypeStruct((M, D), BF16),
        compiler_params=_params("parallel"),
        name="rmsnorm_cast",
    )(x, g.reshape(1, D))


def _project_body(*refs, rope, n_out, scale, head_dim):
    if rope:
        x_ref, w_ref, cos_ref, sin_ref = refs[:4]
        outs = refs[4:4 + n_out]
    else:
        x_ref, w_ref = refs[:2]
        outs = refs[2:2 + n_out]
    acc = _dot(x_ref[...], w_ref[...])
    if rope:
        acc = _rope_tile(acc, cos_ref[...], sin_ref[...])
    if scale != 1.0:
        acc = acc * scale
    for o in outs:
        if len(o.shape) == 3:
            o[0] = acc.T.astype(o.dtype)
        elif len(o.shape) == 4:
            npad = o.shape[3] - head_dim
            if npad:
                lane = lax.broadcasted_iota(jnp.int32, (acc.shape[0], npad), 1)
                pad = jnp.where(lane == 0, 1.0, 0.0)
            for hh in range(o.shape[1]):
                piece = acc[:, hh * head_dim:(hh + 1) * head_dim]
                if npad:
                    piece = jnp.concatenate([piece, pad], axis=1)
                o[0, hh] = piece.astype(o.dtype)
        else:
            o[...] = acc.astype(o.dtype)


def project(x, w, out_dtypes, rope_tabs=None, tab_blocks=1, scale=1.0, head_major=None,
            row_tile=None):
    M, D = x.shape
    N = w.shape[1]
    tm = row_tile if row_tile is not None else _pick(M, ROW_TILES)
    tn = _pick(N, COL_TILES)
    out_specs = [pl.BlockSpec((tm, tn), lambda i, j: (i, j)) for _ in out_dtypes]
    out_shape = [jax.ShapeDtypeStruct((M, N), dt) for dt in out_dtypes]
    head_dim = 0
    if head_major is not None:
        hb, ht, head_dim, flags = head_major
        assert ht % tm == 0
        tpb = ht // tm
        for idx, width in enumerate(flags):
            if width == "T":
                out_specs[idx] = pl.BlockSpec((1, tn, tm), lambda i, j: (i // tpb, j, i % tpb))
                out_shape[idx] = jax.ShapeDtypeStruct((hb, N, ht), out_dtypes[idx])
            elif width:
                out_specs[idx] = pl.BlockSpec((1, tn // head_dim, tm, width),
                                              lambda i, j: (i // tpb, j, i % tpb, 0))
                out_shape[idx] = jax.ShapeDtypeStruct((hb, N // head_dim, ht, width), out_dtypes[idx])
    in_specs = [pl.BlockSpec((tm, D), lambda i, j: (i, 0)),
                pl.BlockSpec((D, tn), lambda i, j: (0, j))]
    args = [x, w]
    if rope_tabs is not None:
        for t in rope_tabs:
            in_specs.append(pl.BlockSpec((tm, LANES), lambda i, j: (i % tab_blocks, 0)))
            args.append(t)
    return pl.pallas_call(
        functools.partial(_project_body, rope=rope_tabs is not None, n_out=len(out_dtypes),
                          scale=scale, head_dim=head_dim),
        grid=(M // tm, N // tn),
        in_specs=in_specs,
        out_specs=out_specs,
        out_shape=out_shape,
        compiler_params=_params("parallel", "parallel"),
        name="project",
    )(*args)


def _matmul_body(*refs, has_bias, act):
    if has_bias:
        x_ref, w_ref, b_ref, o_ref = refs
    else:
        x_ref, w_ref, o_ref = refs
    acc = _dot(x_ref[...], w_ref[...])
    if has_bias:
        acc = acc + b_ref[...]
    if act == "sigmoid":
        acc = _sigmoid(acc)
    o_ref[...] = acc.astype(o_ref.dtype)


def matmul(x, w, out_dtype, bias=None, act=None):
    M, K = x.shape
    N = w.shape[1]
    tm = _pick(M, ROW_TILES)
    tn = _pick(N, COL_TILES)
    in_specs = [pl.BlockSpec((tm, K), lambda i, j: (i, 0)),
                pl.BlockSpec((K, tn), lambda i, j: (0, j))]
    args = [x, w]
    if bias is not None:
        in_specs.append(pl.BlockSpec((1, tn), lambda i, j: (0, j)))
        args.append(bias.reshape(1, N).astype(F32))
    return pl.pallas_call(
        functools.partial(_matmul_body, has_bias=bias is not None, act=act),
        grid=(M // tm, N // tn),
        in_specs=in_specs,
        out_specs=pl.BlockSpec((tm, tn), lambda i, j: (i, j)),
        out_shape=jax.ShapeDtypeStruct((M, N), out_dtype),
        compiler_params=_params("parallel", "parallel"),
        name="matmul",
    )(*args)


CONV_HALO = 8
POOL_HALO = 16


def _conv_body(xb_ref, st_ref, w_ref, b_ref, gwh_ref, gwl_ref, gb_ref, xc_ref, gate_ref, ext_ref,
               *, tb, n_li):
    c = pl.program_id(1)
    nst = CONV_W - 1
    lo = CONV_HALO - nst

    @pl.when(c == 0)
    def _():
        ext_ref[:, lo:CONV_HALO, :] = st_ref[...]

    @pl.when(c > 0)
    def _():
        ext_ref[:, lo:CONV_HALO, :] = ext_ref[:, lo + tb:CONV_HALO + tb, :]

    ext_ref[:, CONV_HALO:CONV_HALO + tb, :] = xb_ref[...].astype(F32)
    acc = b_ref[...][None] + ext_ref[:, lo:lo + tb, :] * w_ref[0:1, :][None]
    for j in range(1, CONV_W):
        acc = acc + ext_ref[:, lo + j:lo + j + tb, :] * w_ref[j:j + 1, :][None]
    xc = _silu(acc)
    xc_ref[...] = xc.astype(xc_ref.dtype)
    x2 = xc.reshape(xc.shape[0] * tb, xc.shape[2])
    hi = x2.astype(BF16)
    lo = (x2 - hi.astype(F32)).astype(BF16)
    g = _dot(hi, gwh_ref[...]) + (_dot(hi, gwl_ref[...]) + _dot(lo, gwh_ref[...])) + gb_ref[...]
    col = lax.broadcasted_iota(jnp.int32, g.shape, 1)
    g = jnp.where(col < n_li, g, _log_sigmoid(g))
    gate_ref[...] = g.reshape(gate_ref.shape)


def conv_silu_gates(xb, state, w, b, gate_w, gate_b, n_li):
    B, T, E = xb.shape
    bb, tb = (1, 256) if T % 256 == 0 else (B, T)
    gw_hi = gate_w.astype(BF16)
    gw_lo = (gate_w - gw_hi.astype(F32)).astype(BF16)
    fixed = lambda i, c: (0, 0)
    return pl.pallas_call(
        functools.partial(_conv_body, tb=tb, n_li=n_li),
        grid=(B // bb, T // tb),
        in_specs=[pl.BlockSpec((bb, tb, E), lambda i, c: (i, c, 0)),
                  pl.BlockSpec((bb, CONV_W - 1, E), lambda i, c: (i, 0, 0)),
                  pl.BlockSpec((CONV_W, E), fixed),
                  pl.BlockSpec((1, E), fixed),
                  pl.BlockSpec((E, LANES), fixed),
                  pl.BlockSpec((E, LANES), fixed),
                  pl.BlockSpec((1, LANES), fixed)],
        out_specs=[pl.BlockSpec((bb, tb, E), lambda i, c: (i, c, 0)),
                   pl.BlockSpec((bb, tb, LANES), lambda i, c: (i, c, 0))],
        out_shape=[jax.ShapeDtypeStruct((B, T, E), BF16),
                   jax.ShapeDtypeStruct((B, T, LANES), F32)],
        scratch_shapes=[pltpu.VMEM((bb, CONV_HALO + tb, E), F32)],
        compiler_params=_params("parallel", "arbitrary"),
        name="conv_silu_gates",
    )(xb, state.astype(F32), w, b.reshape(1, E), gw_hi, gw_lo, gate_b.reshape(1, LANES))


def _mlstm_body(q_ref, k_ref, v_ref, gc_ref, gr_ref, c0_ref, n0_ref, m0_ref, ng_ref,
                h_ref, c_out_ref, n_out_ref, m_out_ref, c_s, n_s, m_s, *, nh, dqk, dv, lc):
    c = pl.program_id(1)

    @pl.when(c == 0)
    def _():
        c_s[...] = c0_ref[0]
        n_s[...] = n0_ref[0]
        m_s[...] = m0_ref[0]

    row = lax.broadcasted_iota(jnp.int32, (lc, lc), 0)
    col = lax.broadcasted_iota(jnp.int32, (lc, lc), 1)
    causal = row >= col
    scale = dqk ** -0.5
    gc = gc_ref[0]
    gr = gr_ref[0]
    for h in range(nh):
        q = (q_ref[0, :, h * dqk:(h + 1) * dqk].astype(F32) * scale).astype(BF16)
        k = k_ref[0, :, h * dqk:(h + 1) * dqk]
        v = v_ref[0, :, h * dv:(h + 1) * dv]
        li_c, lf_c = gc[:, h:h + 1], gc[:, nh + h:nh + h + 1]
        li_r, lf_r = gr[h:h + 1, :], gr[nh + h:nh + h + 1, :]
        b_c = jnp.sum(jnp.where(causal, lf_r, 0.0), axis=1, keepdims=True)
        b_r = jnp.sum(jnp.where(row <= col, lf_c, 0.0), axis=0, keepdims=True)
        m_prev = m_s[h:h + 1, 0:1]
        dm = jnp.where(causal, b_c - b_r + li_r, NEG)
        m_inter = b_c + m_prev
        m_t = jnp.maximum(m_inter, jnp.max(dm, axis=1, keepdims=True))
        a = _dot_nt(q, k) * jnp.exp(dm - m_t)
        inter = jnp.exp(m_inter - m_t)
        c_prev = c_s[h]
        n_prev = n_s[h:h + 1, :]
        num = _dot(a.astype(BF16), v) + inter * _dot(q, c_prev.astype(BF16))
        qn = jnp.sum(q.astype(F32) * n_prev, axis=1, keepdims=True)
        den = jnp.sum(a, axis=1, keepdims=True) + inter * qn
        hh = num / jnp.maximum(jnp.abs(den), jnp.exp(-m_t))
        hn = hh * lax.rsqrt(jnp.mean(hh * hh, axis=-1, keepdims=True) + EPS) \
            * ng_ref[:, h * dv:(h + 1) * dv]
        h_ref[0, :, h * dv:(h + 1) * dv] = hn.astype(h_ref.dtype)
        b_last = b_c[lc - 1:lc, :]
        g_r = b_last - b_r + li_r
        g_c = b_last - b_c + li_c
        m_new = jnp.maximum(b_last + m_prev, jnp.max(g_r, axis=1, keepdims=True))
        w_r = jnp.exp(g_r - m_new)
        w_c = jnp.exp(g_c - m_new)
        decay = jnp.exp(b_last + m_prev - m_new)
        wv = (w_c * v.astype(F32)).astype(BF16)
        c_s[h] = decay * c_prev + _dot_tn(k, wv)
        wr8 = jnp.broadcast_to(w_r, (SUBLANES, lc)).astype(BF16)
        n_s[h:h + 1, :] = decay * n_prev + _dot(wr8, k)[0:1, :]
        m_s[h:h + 1, :] = jnp.broadcast_to(m_new, (1, LANES))

    @pl.when(c == pl.num_programs(1) - 1)
    def _():
        c_out_ref[0] = c_s[...]
        n_out_ref[0] = n_s[...]
        m_out_ref[0] = m_s[...]


def mlstm_recurrence(q, k, v, gcol, grow, c0_all, layer, n0, m0, norm_g):
    B, T, _ = q.shape
    nh, dqk, dv = c0_all.shape[2], c0_all.shape[3], c0_all.shape[4]
    lc = min(MLSTM_CHUNK, T)
    m0b = jnp.broadcast_to(m0[:, :, None], (B, nh, LANES)).astype(F32)
    h, c_new, n_new, m_new = pl.pallas_call(
        functools.partial(_mlstm_body, nh=nh, dqk=dqk, dv=dv, lc=lc),
        grid=(B, T // lc),
        in_specs=[pl.BlockSpec((1, lc, nh * dqk), lambda b, c: (b, c, 0)),
                  pl.BlockSpec((1, lc, nh * dqk), lambda b, c: (b, c, 0)),
                  pl.BlockSpec((1, lc, nh * dv), lambda b, c: (b, c, 0)),
                  pl.BlockSpec((1, lc, LANES), lambda b, c: (b, c, 0)),
                  pl.BlockSpec((1, 2 * nh, lc), lambda b, c: (b, 0, c)),
                  pl.BlockSpec((None, 1, nh, dqk, dv), lambda b, c: (layer, b, 0, 0, 0)),
                  pl.BlockSpec((1, nh, dqk), lambda b, c: (b, 0, 0)),
                  pl.BlockSpec((1, nh, LANES), lambda b, c: (b, 0, 0)),
                  pl.BlockSpec((1, nh * dv), lambda b, c: (0, 0))],
        out_specs=[pl.BlockSpec((1, lc, nh * dv), lambda b, c: (b, c, 0)),
                   pl.BlockSpec((1, nh, dqk, dv), lambda b, c: (b, 0, 0, 0)),
                   pl.BlockSpec((1, nh, dqk), lambda b, c: (b, 0, 0)),
                   pl.BlockSpec((1, nh, LANES), lambda b, c: (b, 0, 0))],
        out_shape=[jax.ShapeDtypeStruct((B, T, nh * dv), BF16),
                   jax.ShapeDtypeStruct((B, nh, dqk, dv), F32),
                   jax.ShapeDtypeStruct((B, nh, dqk), F32),
                   jax.ShapeDtypeStruct((B, nh, LANES), F32)],
        scratch_shapes=[pltpu.VMEM((nh, dqk, dv), F32),
                        pltpu.VMEM((nh, dqk), F32),
                        pltpu.VMEM((nh, LANES), F32)],
        compiler_params=_params("parallel", "arbitrary"),
        name="mlstm_recurrence",
    )(q, k, v, gcol, grow, c0_all.astype(F32), n0.astype(F32), m0b, norm_g.reshape(1, nh * dv))
    return h, c_new, n_new, m_new[:, :, 0]


def _pool_body(xb_ref, st_ref, wg_ref, sc_ref, y_ref, ext_ref, *, tb, pos0, nbuf):
    c = pl.program_id(1)
    lo = POOL_HALO - nbuf
    bb = xb_ref.shape[0]
    gw = wg_ref.shape[1]

    @pl.when(c == 0)
    def _():
        ext_ref[:, lo:POOL_HALO, :] = st_ref[...]

    @pl.when(c > 0)
    def _():
        ext_ref[:, lo:POOL_HALO, :] = ext_ref[:, lo + tb:POOL_HALO + tb, :]

    ext_ref[:, POOL_HALO:POOL_HALO + tb, :] = xb_ref[...].astype(F32)
    pos = pos0 + c * tb + lax.broadcasted_iota(jnp.int32, (1, tb, 1), 1)
    for g, w in enumerate(POOL_WINDOWS):
        cols = slice(g * gw, (g + 1) * gw)
        cur = ext_ref[:, POOL_HALO:POOL_HALO + tb, cols]
        win = cur
        for i in range(1, w):
            win = win + ext_ref[:, POOL_HALO - i:POOL_HALO - i + tb, cols]
        cnt = jnp.minimum(w, pos + 1).astype(F32)
        d = (win / cnt - cur).reshape(bb * tb, gw).astype(BF16)
        yg = _dot(d, wg_ref[g]) * sc_ref[:, cols]
        y_ref[:, :, cols] = yg.reshape(bb, tb, gw).astype(y_ref.dtype)


def pool_mix(xb, state, w_grp, scale, pos0):
    B, T, E = xb.shape
    nbuf = state.shape[1]
    bb, tb = (1, 256) if T % 256 == 0 else (B, T)
    ng, gw = w_grp.shape[0], w_grp.shape[1]
    return pl.pallas_call(
        functools.partial(_pool_body, tb=tb, pos0=pos0, nbuf=nbuf),
        grid=(B // bb, T // tb),
        in_specs=[pl.BlockSpec((bb, tb, E), lambda i, c: (i, c, 0)),
                  pl.BlockSpec((bb, nbuf, E), lambda i, c: (i, 0, 0)),
                  pl.BlockSpec((ng, gw, gw), lambda i, c: (0, 0, 0)),
                  pl.BlockSpec((1, E), lambda i, c: (0, 0))],
        out_specs=pl.BlockSpec((bb, tb, E), lambda i, c: (i, c, 0)),
        out_shape=jax.ShapeDtypeStruct((B, T, E), BF16),
        scratch_shapes=[pltpu.VMEM((bb, POOL_HALO + tb, E), F32)],
        compiler_params=_params("parallel", "arbitrary"),
        name="pool_mix",
    )(xb, state.astype(F32), w_grp, scale.reshape(1, E))


def _tail_body(*refs, kind):
    if kind == "mlstm":
        hn_ref, o_ref, xc_ref, z_ref, skip_ref = refs[:5]
        rest = refs[5:]
        mix = (o_ref[...].astype(F32) * hn_ref[...].astype(F32)
               + skip_ref[...] * xc_ref[...].astype(F32)) * _silu(z_ref[...].astype(F32))
    else:
        y_ref, z_ref = refs[:2]
        rest = refs[2:]
        mix = y_ref[...].astype(F32) * _silu(z_ref[...].astype(F32))
    x_ref, p_ref, wd_ref, gw_ref, pw_ref, ng_ref, xo_ref, no_ref = rest
    x1 = x_ref[...] + _dot(mix.astype(BF16), wd_ref[...])
    gate = _sigmoid(_dot(x1.astype(BF16), gw_ref[...]))
    x2 = x1 + gate * _dot(p_ref[...].astype(BF16), pw_ref[...])
    xo_ref[...] = x2
    no_ref[...] = (x2 * lax.rsqrt(jnp.mean(x2 * x2, axis=-1, keepdims=True) + EPS)
                   * ng_ref[...]).astype(no_ref.dtype)


def layer_tail(kind, mix_inputs, x, p_all, layer, w_down, gate_w, ple_w, next_g, next_dtype,
               skip=None):
    M, D = x.shape
    tm = _pick(M, (512, 256))
    row = lambda i: (i, 0)
    fixed = lambda i: (0, 0)
    in_specs, args = [], []
    for a in mix_inputs:
        in_specs.append(pl.BlockSpec((tm, a.shape[1]), row))
        args.append(a)
    if kind == "mlstm":
        in_specs.append(pl.BlockSpec((1, skip.shape[-1]), fixed))
        args.append(skip.reshape(1, -1))
    in_specs += [pl.BlockSpec((tm, D), row),
                 pl.BlockSpec((None, tm, p_all.shape[2]), lambda i: (layer, i, 0)),
                 pl.BlockSpec(w_down.shape, fixed), pl.BlockSpec(gate_w.shape, fixed),
                 pl.BlockSpec(ple_w.shape, fixed), pl.BlockSpec((1, D), fixed)]
    args += [x, p_all, w_down, gate_w, ple_w, next_g.reshape(1, D)]
    out_specs = [pl.BlockSpec((tm, D), row), pl.BlockSpec((tm, D), row)]
    out_shape = [jax.ShapeDtypeStruct((M, D), F32), jax.ShapeDtypeStruct((M, D), next_dtype)]
    return pl.pallas_call(
        functools.partial(_tail_body, kind=kind),
        grid=(M // tm,),
        in_specs=in_specs, out_specs=out_specs, out_shape=out_shape,
        compiler_params=_params("parallel"),
        name="layer_tail_" + kind,
    )(*args)


def _sortable_key(s):
    s = jnp.where(s == 0.0, 0.0, s)
    bits = lax.bitcast_convert_type(s, jnp.int32)
    return jnp.where(bits < 0, bits ^ jnp.int32(0x7FFFFFFF), bits)


COUNT_STRIP_VREGS = 32


def _count(keys_ref, nvalid, pred):
    rows, cw = keys_ref.shape[1], keys_ref.shape[2]
    strip = max(SUBLANES, min(rows, COUNT_STRIP_VREGS * SUBLANES * LANES // cw))

    def body(kc, acc):
        parts = []
        for r0 in range(0, rows, strip):
            rs = slice(r0, min(r0 + strip, rows))
            hit = jnp.where(pred(keys_ref[kc, rs, :], kc, rs), 1.0, 0.0)
            part = hit[:, 0:LANES]
            for g in range(1, cw // LANES):
                part = part + hit[:, g * LANES:(g + 1) * LANES]
            parts.append(part)
        return acc + (parts[0] if len(parts) == 1 else jnp.concatenate(parts, axis=0))

    acc = lax.fori_loop(0, nvalid, body, jnp.zeros((rows, LANES), F32))
    return jnp.sum(acc, axis=1, keepdims=True)


def _select_topk(keys_ref, cidx_ref, nvalid, k_top, idx_bits):
    rows, cw = keys_ref.shape[1], keys_ref.shape[2]
    kf = float(k_top)
    imin = jnp.int32(INT_MIN)

    def bit_body(i, tau_u):
        cand_u = tau_u | jnp.left_shift(jnp.int32(1), 31 - i)
        cand = cand_u ^ imin
        cnt = _count(keys_ref, nvalid, lambda kk, kc, rs: kk >= cand[rs])
        return jnp.where(cnt >= kf, cand_u, tau_u)

    tau_u = lax.fori_loop(0, 32, bit_body, jnp.zeros((rows, 1), jnp.int32))
    tau = tau_u ^ imin
    n_gt = _count(keys_ref, nvalid, lambda kk, kc, rs: kk > tau[rs])
    n_ge = _count(keys_ref, nvalid, lambda kk, kc, rs: kk >= tau[rs])
    need = kf - n_gt
    short = tau_u == 0
    cidx_ref[...] = jnp.broadcast_to(jnp.where(short, -1, 2 ** 30), cidx_ref.shape)
    excess = jnp.max(jnp.where((n_ge > kf) & jnp.logical_not(short), 1.0, 0.0))

    @pl.when(excess > 0.5)
    def _():
        lane = lax.broadcasted_iota(jnp.int32, (1, cw), 1)

        def idx_body(i, cut):
            cand = cut | jnp.left_shift(jnp.int32(1), idx_bits - 1 - i)
            cnt = _count(keys_ref, nvalid,
                         lambda kk, kc, rs: (kk == tau[rs]) & ((kc * cw + lane) < cand[rs]))
            return jnp.where(cnt < need, cand, cut)

        cut = lax.fori_loop(0, idx_bits, idx_body, jnp.zeros((rows, 1), jnp.int32))
        cidx_ref[...] = jnp.broadcast_to(jnp.where(short, -1, cut), cidx_ref.shape)

    return tau


def _selected(kk, colv, tau, cut):
    return (kk > tau) | ((kk == tau) & (colv <= cut))


def _n_causal_chunks(qb, tq, tk):
    return lax.div(qb * tq + (tq - 1), jnp.int32(tk)) + 1


def _idx_prompt_body(qi_ref, wi_ref, ki_ref, bias_ref, keys_ref, cidx_ref, *,
                     nh, dh, tq, tk, nk, k_top, idx_bits):
    per = keys_ref.shape[2] // tk
    gw = per * tk
    qb = pl.program_id(1)
    nvalid = _n_causal_chunks(qb, tq, gw)
    rowpos = qb * tq + lax.broadcasted_iota(jnp.int32, (tq, 1), 0)
    lane = lax.broadcasted_iota(jnp.int32, (1, tk), 1)
    wi = wi_ref[0]

    def score_group(gc, carry):
        for part in range(per):
            kt = ki_ref[0, gc * per + part]
            s = jnp.zeros((tq, tk), F32)
            for h in range(nh):
                rel = jnp.maximum(_dot_nt(qi_ref[0, :, h * dh:(h + 1) * dh], kt), 0.0)
                s = s + rel * wi[:, h:h + 1]
            colv = gc * gw + part * tk + lane
            keys_ref[gc, :, part * tk:(part + 1) * tk] = jnp.where(
                colv <= rowpos, _sortable_key(s), jnp.int32(INT_MIN))
        return carry

    lax.fori_loop(0, nvalid, score_group, 0)
    tau = _select_topk(keys_ref, cidx_ref, nvalid, k_top, idx_bits)
    cut = cidx_ref[:, 0:1]

    def write_group(gc, carry):
        for part in range(per):
            kk = keys_ref[gc, :, part * tk:(part + 1) * tk]
            sel = _selected(kk, gc * gw + part * tk + lane, tau, cut)
            bias_ref[0, 0, gc * per + part] = jnp.where(sel, 0.0, NEG).astype(bias_ref.dtype)
        return carry

    lax.fori_loop(0, nvalid, write_group, 0)

    def fill_chunk(kc, carry):
        bias_ref[0, 0, kc] = jnp.full((tq, tk), NEG, bias_ref.dtype)
        return carry

    lax.fori_loop(nvalid * per, nk, fill_chunk, 0)


def indexer_bias_prompt(qi, wi, ki, k_top):
    B, T, _ = qi.shape
    dh = ki.shape[-1]
    nh = qi.shape[-1] // dh
    tq, tk = min(IDX_TQ, T), min(ATT_TK, T)
    tqa = min(ATT_TQ, T)
    sub = tqa // tq
    nq, nk = T // tq, T // tk
    per = 2 if nk % 2 == 0 else 1
    idx_bits = int(T).bit_length() + 1
    return pl.pallas_call(
        functools.partial(_idx_prompt_body, nh=nh, dh=dh, tq=tq, tk=tk, nk=nk, k_top=k_top,
                          idx_bits=idx_bits),
        grid=(B, nq),
        in_specs=[pl.BlockSpec((1, tq, nh * dh), lambda b, i: (b, i, 0)),
                  pl.BlockSpec((1, tq, LANES), lambda b, i: (b, i, 0)),
                  pl.BlockSpec((1, nk, tk, dh), lambda b, i: (b, 0, 0, 0))],
        out_specs=pl.BlockSpec((1, 1, nk, tq, tk), lambda b, i: (b, i // sub, 0, i % sub, 0)),
        out_shape=jax.ShapeDtypeStruct((B, T // tqa, nk, tqa, tk), BF16),
        scratch_shapes=[pltpu.VMEM((nk // per, tq, per * tk), jnp.int32),
                        pltpu.VMEM((tq, LANES), jnp.int32)],
        compiler_params=_params("parallel", "arbitrary"),
        name="indexer_bias_prompt",
    )(qi, wi, ki.reshape(B, nk, tk, dh))


def _attn_prompt_body(q_ref, k_ref, v_ref, bias_ref, o_ref, m_s, acc_s, *, nh, dh, tq, tk):
    qb, kb = pl.program_id(1), pl.program_id(2)
    nvalid = _n_causal_chunks(qb, tq, tk)

    @pl.when(kb == 0)
    def _():
        m_s[...] = jnp.full(m_s.shape, NEG, F32)
        acc_s[...] = jnp.zeros(acc_s.shape, F32)

    @pl.when(kb < nvalid)
    def _():
        bias = bias_ref[0, 0, 0].astype(F32)
        for h in range(nh):
            s = _dot_nt(q_ref[0, h], k_ref[0, h]) + bias
            m_prev = m_s[h]
            m_new = jnp.maximum(m_prev, jnp.max(s, axis=1, keepdims=True))
            alpha = jnp.exp2(m_prev - m_new)
            p = jnp.exp2(s - jnp.concatenate([m_new] * (tk // LANES), axis=1))
            acc_s[h] = alpha * acc_s[h] + _dot(p.astype(BF16), v_ref[0, h])
            m_s[h] = m_new

    @pl.when(kb == pl.num_programs(2) - 1)
    def _():
        for h in range(nh):
            acc = acc_s[h]
            o_ref[0, :, h * dh:(h + 1) * dh] = (acc[:, :dh] / acc[:, dh:dh + 1]).astype(o_ref.dtype)


def attention_prompt(q, k, v, bias):
    B, nh, T, dh = q.shape
    assert tuple(v.shape) == (B, nh, T, LANES) and dh < LANES
    _, nq, nk, tq, tk = bias.shape

    def kv_map(b, i, j):
        return (b, 0, jnp.minimum(j, _n_causal_chunks(i, tq, tk) - 1), 0)

    def bias_map(b, i, j):
        return (b, i, jnp.minimum(j, _n_causal_chunks(i, tq, tk) - 1), 0, 0)

    return pl.pallas_call(
        functools.partial(_attn_prompt_body, nh=nh, dh=dh, tq=tq, tk=tk),
        grid=(B, nq, nk),
        in_specs=[pl.BlockSpec((1, nh, tq, dh), lambda b, i, j: (b, 0, i, 0)),
                  pl.BlockSpec((1, nh, tk, dh), kv_map),
                  pl.BlockSpec((1, nh, tk, LANES), kv_map),
                  pl.BlockSpec((1, 1, 1, tq, tk), bias_map)],
        out_specs=pl.BlockSpec((1, tq, nh * dh), lambda b, i, j: (b, i, 0)),
        out_shape=jax.ShapeDtypeStruct((B, T, nh * dh), BF16),
        scratch_shapes=[pltpu.VMEM((nh, tq, LANES), F32),
                        pltpu.VMEM((nh, tq, LANES), F32)],
        compiler_params=_params("parallel", "parallel", "arbitrary"),
        name="attention_prompt",
    )(q, k, v, bias)


def _idx_sample_body(pt_ref, qi_ref, wi_ref, knew_ref, *rest, nh, ts, npg, nsteps, k_top,
                     idx_bits):
    page_refs = rest[:npg]
    bias_ref, keys_ref, cidx_ref = rest[npg:]
    s = pl.program_id(1)
    cw = keys_ref.shape[2]
    wi = wi_ref[0]
    qi = qi_ref[0]

    def scores(kt):
        full = jnp.maximum(_dot(qi, kt), 0.0)
        out = jnp.zeros((ts, cw), F32)
        for h in range(nh):
            out = out + full[h * ts:(h + 1) * ts, :] * wi[:, h:h + 1]
        return out

    kt = jnp.concatenate([r[...] for r in page_refs], axis=1).astype(BF16)
    keys_ref[s] = _sortable_key(scores(kt))

    @pl.when(s == nsteps - 1)
    def _():
        lane = lax.broadcasted_iota(jnp.int32, (1, cw), 1)
        past = nsteps * cw
        rowpos = past + lax.broadcasted_iota(jnp.int32, (ts, 1), 0)
        valid = (past + lane) <= rowpos
        keys_ref[nsteps] = jnp.where(valid, _sortable_key(scores(knew_ref[0])),
                                     jnp.int32(INT_MIN))
        tau = _select_topk(keys_ref, cidx_ref, nsteps + 1, k_top, idx_bits)
        cut = cidx_ref[:, 0:1]
        for kc in range(nsteps + 1):
            sel = _selected(keys_ref[kc], kc * cw + lane, tau, cut)
            bias_ref[0, kc] = jnp.where(sel, 0.0, NEG).astype(bias_ref.dtype)


def _page_specs(n_pages_per_step, block, layer, first_step):
    specs = []
    for i in range(n_pages_per_step):
        def imap(b, s, pt, i=i):
            step = jnp.maximum(s - first_step, 0)
            return (layer, pt[b, step * n_pages_per_step + i]) + (0,) * len(block)
        specs.append(pl.BlockSpec((None, None) + tuple(block), imap))
    return specs


def indexer_bias_sample(qi_hq, wi, ki_new_t, cache_kidx_t, layer, page_table, k_top):
    B, n_pages = page_table.shape
    _, _, dh, page = cache_kidx_t.shape
    ts = wi.shape[1]
    nh = qi_hq.shape[1] // ts
    npg = min(IDX_PAGES_PER_STEP, n_pages)
    nsteps = n_pages // npg
    cw = npg * page
    idx_bits = int(n_pages * page + cw).bit_length() + 1
    grid_spec = pltpu.PrefetchScalarGridSpec(
        num_scalar_prefetch=1,
        grid=(B, nsteps),
        in_specs=[pl.BlockSpec((1, nh * ts, dh), lambda b, s, pt: (b, 0, 0)),
                  pl.BlockSpec((1, ts, LANES), lambda b, s, pt: (b, 0, 0)),
                  pl.BlockSpec((1, dh, cw), lambda b, s, pt: (b, 0, 0))]
        + _page_specs(npg, (dh, page), layer, 0),
        out_specs=pl.BlockSpec((1, nsteps + 1, ts, cw), lambda b, s, pt: (b, 0, 0, 0)),
        scratch_shapes=[pltpu.VMEM((nsteps + 1, ts, cw), jnp.int32),
                        pltpu.VMEM((ts, LANES), jnp.int32)],
    )
    return pl.pallas_call(
        functools.partial(_idx_sample_body, nh=nh, ts=ts, npg=npg, nsteps=nsteps, k_top=k_top,
                          idx_bits=idx_bits),
        grid_spec=grid_spec,
        out_shape=jax.ShapeDtypeStruct((B, nsteps + 1, ts, cw), F32),
        compiler_params=_params("parallel", "arbitrary"),
        name="indexer_bias_sample",
    )(page_table, qi_hq, wi, ki_new_t, *([cache_kidx_t] * npg))


def _attn_sample_body(pt_ref, qbd_ref, knew_ref, vnew_ref, bias_ref, *rest, nh, ts, npg, dh):
    kpages, vpages = rest[:npg], rest[npg:2 * npg]
    o_ref, m_s, l_s, acc_s = rest[2 * npg:]
    s = pl.program_id(1)
    rows = nh * ts

    @pl.when(s == 0)
    def _():
        m_s[...] = jnp.full(m_s.shape, NEG, F32)
        l_s[...] = jnp.zeros(l_s.shape, F32)
        acc_s[...] = jnp.zeros(acc_s.shape, F32)

    def update(kt, vt):
        n = kt.shape[1]
        bias = bias_ref[0, 0][:, :n]
        logits = _dot(qbd_ref[0], kt) + jnp.concatenate([bias] * nh, axis=0)
        m_prev = m_s[...]
        m_new = jnp.maximum(m_prev, jnp.max(logits, axis=1, keepdims=True))
        alpha = jnp.exp2(m_prev - m_new)
        p = jnp.exp2(logits - m_new)
        l_s[...] = alpha * l_s[...] + jnp.sum(p, axis=1, keepdims=True)
        acc_s[...] = alpha * acc_s[...] + _dot_nt(p.astype(BF16), vt)
        m_s[...] = m_new

    @pl.when(s == 0)
    def _():
        update(knew_ref[0], vnew_ref[0])

    def gather_pages(page_refs):
        return jnp.concatenate([r[...].reshape(nh * dh, r.shape[2]) for r in page_refs],
                               axis=1).astype(BF16)

    @pl.when(s > 0)
    def _():
        update(gather_pages(kpages), gather_pages(vpages))

    @pl.when(s == pl.num_programs(1) - 1)
    def _():
        o = acc_s[...] / l_s[...]
        lane = lax.broadcasted_iota(jnp.int32, (1, nh * dh), 1)
        res = jnp.zeros((ts, nh * dh), F32)
        for h in range(nh):
            own = (lane >= h * dh) & (lane < (h + 1) * dh)
            res = res + jnp.where(own, o[h * ts:(h + 1) * ts, :], 0.0)
        o_ref[0] = res.astype(o_ref.dtype)


def attention_sample(qbd, k_new_t, v_new_t, bias, cache_k_t, cache_v_t, layer, page_table, ts):
    B, n_pages = page_table.shape
    _, _, nh, dh, page = cache_k_t.shape
    width = nh * dh
    n_new = k_new_t.shape[2]
    npg = min(PAGES_PER_STEP, n_pages)
    cw = npg * page
    nsteps1 = n_pages // npg + 1
    per_chunk = bias.shape[3] // cw
    last_chunk = bias.shape[1] - 1
    rows = nh * ts

    def bias_map(b, s, pt):
        g = jnp.maximum(s - 1, 0)
        return (b, jnp.where(s == 0, last_chunk, g // per_chunk), 0,
                jnp.where(s == 0, 0, g % per_chunk))

    grid_spec = pltpu.PrefetchScalarGridSpec(
        num_scalar_prefetch=1,
        grid=(B, nsteps1),
        in_specs=[pl.BlockSpec((1, rows, width), lambda b, s, pt: (b, 0, 0)),
                  pl.BlockSpec((1, width, n_new), lambda b, s, pt: (b, 0, 0)),
                  pl.BlockSpec((1, width, n_new), lambda b, s, pt: (b, 0, 0)),
                  pl.BlockSpec((1, 1, ts, cw), bias_map)]
        + _page_specs(npg, (nh, dh, page), layer, 1) + _page_specs(npg, (nh, dh, page), layer, 1),
        out_specs=pl.BlockSpec((1, ts, width), lambda b, s, pt: (b, 0, 0)),
        scratch_shapes=[pltpu.VMEM((rows, 1), F32),
                        pltpu.VMEM((rows, 1), F32),
                        pltpu.VMEM((rows, width), F32)],
    )
    return pl.pallas_call(
        functools.partial(_attn_sample_body, nh=nh, ts=ts, npg=npg, dh=dh),
        grid_spec=grid_spec,
        out_shape=jax.ShapeDtypeStruct((B, ts, width), BF16),
        compiler_params=_params("parallel", "arbitrary"),
        name="attention_sample",
    )(page_table, qbd, k_new_t, v_new_t, bias, *([cache_k_t] * npg), *([cache_v_t] * npg))


def _pad_time(a, t_pad, value=0.0):
    return jnp.pad(a, ((0, 0), (0, t_pad - a.shape[1]), (0, 0)), constant_values=value)


def _mlstm_layer(xn, B, T, state, w, j):
    c_all, n0, m0, conv0 = state
    nh = c_all.shape[2]
    E = w["a_conv_w"].shape[-1]
    w_up = w["a_w_up"][j]
    xb, = project(xn,w_up[:, :E], (BF16,))
    z, = project(xn,w_up[:, E:], (BF16,))
    xc, gcol = conv_silu_gates(xb.reshape(B, T, E), conv0, w["a_conv_w"][j], w["a_conv_b"][j],
                               w["a_w_if_pad"][j], w["a_b_if_pad"][j], nh)
    xc = xc.reshape(B * T, E)
    q = matmul(xc, w["a_w_q"][j], BF16)
    k = matmul(xc, w["a_w_k"][j], BF16)
    v = matmul(xb, w["a_w_v"][j], BF16)
    o = matmul(xb, w["a_w_o"][j], BF16, bias=w["a_b_o"][j], act="sigmoid")
    q3, k3, v3 = q.reshape(B, T, -1), k.reshape(B, T, -1), v.reshape(B, T, -1)
    tp = T if T % LANES == 0 else ((T + LANES - 1) // LANES) * LANES
    if tp != T:
        q3, k3, v3 = _pad_time(q3, tp), _pad_time(k3, tp), _pad_time(v3, tp)
        pad_row = jnp.where(jnp.arange(LANES) < nh, NEG, 0.0).astype(F32)
        gcol = jnp.concatenate([gcol, jnp.broadcast_to(pad_row, (B, tp - T, LANES))], axis=1)
    grow = jnp.swapaxes(gcol[:, :, :2 * nh], 1, 2)
    hn, c_new, n_new, m_new = mlstm_recurrence(q3, k3, v3, gcol, grow, c_all, j, n0, m0,
                                               w["a_norm_g"][j])
    hn = hn[:, :T].reshape(B * T, -1)
    assert T >= CONV_W - 1
    conv_new = xb.reshape(B, T, E)[:, T - (CONV_W - 1):].astype(F32)
    return (hn, o, xc, z), (c_new, n_new, m_new, conv_new)


def _rope_tables(pos):
    half = 32
    inv = ROPE_THETA ** (-np.arange(half, dtype=np.float64) / half)
    ang = np.asarray(pos, np.float64)[:, None] * inv[None, :]
    cos = np.tile(np.cos(ang), (1, 4))
    sin = np.tile(np.concatenate([-np.sin(ang), np.sin(ang)], axis=1), (1, 2))
    return jnp.asarray(cos, F32), jnp.asarray(sin, F32)


def _dsa_project(xn, B, T, pos, w, j, nh, dh, nhi, di, head_major=False):
    M = B * T
    cos, sin = _rope_tables(pos)
    tm = _pick(T, ROW_TILES) if T % ROW_TILES[-1] == 0 else _pick(M, ROW_TILES)
    if T % tm == 0:
        tabs, tab_blocks = (cos, sin), T // tm
    else:
        tabs, tab_blocks = (jnp.tile(cos, (M // T, 1)), jnp.tile(sin, (M // T, 1))), M // tm
    rope = dict(rope_tabs=tabs, tab_blocks=tab_blocks, row_tile=tm)
    hm = (lambda *widths: (B, T, dh, widths)) if head_major else (lambda *widths: None)
    q, = project(xn,w["b_wq"][j], (BF16,), scale=dh ** -0.5 * math.log2(math.e),
                     head_major=hm(dh), **rope)
    wide = "T" if head_major else 0
    k32, k16 = project(xn,w["b_wk"][j], (F32, BF16), head_major=hm(wide, dh), **rope)
    v32, v16 = project(xn,w["b_wv"][j], (F32, BF16), head_major=hm(wide, LANES),
                           row_tile=tm)
    gate, = project(xn,w["b_wg"][j], (BF16,))
    qi, = project(xn,w["b_wqi"][j], (BF16,), **rope)
    ki32, ki16 = project(xn,w["b_wki_pad"][j], (F32, BF16), head_major=hm(wide, 0), **rope)
    wi, = project(xn,w["b_wwi_pad"][j], (F32,), scale=nhi ** -0.5 * di ** -0.5)
    if head_major:
        new = (jnp.transpose(k32.reshape(B, nh, dh, T), (0, 3, 1, 2)),
               jnp.transpose(v32.reshape(B, nh, dh, T), (0, 3, 1, 2)),
               jnp.transpose(ki32[:, :di], (0, 2, 1)))
    else:
        new = (k32.reshape(B, T, nh, dh), v32.reshape(B, T, nh, dh), ki32[:, :di].reshape(B, T, di))
    return q, k16, v16, gate, qi, ki16[:, :di].reshape(B, T, di), wi, new


def _dsa_layer_prompt(xn, B, T, w, j, dims):
    nh, dh, nhi, di = dims
    q, k16, v16, gate, qi, ki16, wi, new = _dsa_project(
        xn, B, T, np.arange(T), w, j, *dims, head_major=True)
    k_top = min(TOPK_MAX, T // 4)
    bias = indexer_bias_prompt(qi.reshape(B, T, nhi * di), wi.reshape(B, T, LANES), ki16, k_top)
    attn = attention_prompt(q, k16, v16, bias).reshape(B * T, nh * dh)
    return (attn, gate), new


def _dsa_layer_sample(xn, B, T, w, j, dims, ctx):
    nh, dh, nhi, di = dims
    cache_k, cache_v, cache_kidx, page_table = ctx
    n_pages = page_table.shape[1]
    page = cache_kidx.shape[2]
    past = n_pages * page
    q, k16, v16, gate, qi, ki16, wi, new = _dsa_project(
        xn, B, T, past + np.arange(T), w, j, *dims)
    k_top = min(TOPK_MAX, (past + T) // 4)
    cw = min(IDX_PAGES_PER_STEP, n_pages) * page
    qi_hq = jnp.transpose(qi.reshape(B, T, nhi, di), (0, 2, 1, 3)).reshape(B, nhi * T, di)
    token_last = lambda a, n: jnp.swapaxes(_pad_time(a, n), 1, 2)
    ki_new_t = token_last(ki16, cw)
    bias = indexer_bias_sample(qi_hq, wi.reshape(B, T, LANES), ki_new_t,
                               jnp.swapaxes(cache_kidx, 2, 3), j, page_table, k_top)
    q3 = q.reshape(B, T, nh * dh)
    own = (jnp.arange(nh * T)[:, None] // T) == (jnp.arange(nh * dh)[None, :] // dh)
    qbd = jnp.where(own[None], jnp.tile(q3, (1, nh, 1)), 0).astype(BF16)
    width = nh * dh
    n_new = ((T + LANES - 1) // LANES) * LANES
    attn = attention_sample(qbd, token_last(k16.reshape(B, T, width), n_new),
                            token_last(v16.reshape(B, T, width), n_new), bias,
                            jnp.transpose(cache_k, (0, 1, 3, 4, 2)),
                            jnp.transpose(cache_v, (0, 1, 3, 4, 2)), j, page_table, T)
    return (attn.reshape(B * T, width), gate), new


def _pool_layer(xn, B, T, buf, pos0, w, j):
    E = w["c_scale"].shape[-1]
    w_up = w["c_w_up"][j]
    xb, = project(xn,w_up[:, :E], (BF16,))
    z, = project(xn,w_up[:, E:], (BF16,))
    xb3 = xb.reshape(B, T, E)
    y = pool_mix(xb3, buf, w["c_w_grp"][j], w["c_scale"][j], pos0).reshape(B * T, E)
    nbuf = buf.shape[1]
    if T >= nbuf:
        new_buf = xb3[:, T - nbuf:].astype(F32)
    else:
        new_buf = jnp.concatenate([buf[:, T:].astype(F32), xb3.astype(F32)], axis=1)
    return (y, z), new_buf


def _run_group(x3, p4, pos0, mlstm_state, dsa_ctx, pool_state, w, dims):
    B, T, D = x3.shape
    depth = p4.shape[0]
    x = x3.reshape(B * T, D)
    p_all = p4.reshape(depth, B * T, p4.shape[-1])
    new_a, new_b, new_c = [], [], []
    xn = rmsnorm_cast(x, w["norm_g"][0])
    for i in range(depth):
        kind, j = i % 3, i // 3
        last = i == depth - 1
        nxt = dict(next_g=w["final_g"] if last else w["norm_g"][i + 1],
                   next_dtype=F32 if last else BF16)
        if kind == 0:
            state = (mlstm_state[0],) + tuple(s[j] for s in mlstm_state[1:])
            mix, st = _mlstm_layer(xn, B, T, state, w, j)
            new_a.append(st)
            x, xn = layer_tail("mlstm", mix, x, p_all, i, w["a_w_down"][j], w["ple_gate_w"][i],
                               w["ple_w"][i], skip=w["a_skip"][j], **nxt)
        elif kind == 1:
            if dsa_ctx is None:
                mix, st = _dsa_layer_prompt(xn, B, T, w, j, dims)
            else:
                mix, st = _dsa_layer_sample(xn, B, T, w, j, dims, dsa_ctx)
            new_b.append(st)
            x, xn = layer_tail("gated", mix, x, p_all, i, w["b_w_out"][j], w["ple_gate_w"][i],
                               w["ple_w"][i], **nxt)
        else:
            mix, st = _pool_layer(xn, B, T, pool_state[j], pos0, w, j)
            new_c.append(st)
            x, xn = layer_tail("gated", mix, x, p_all, i, w["c_w_down"][j], w["ple_gate_w"][i],
                               w["ple_w"][i], **nxt)
    a_states = tuple(jnp.stack([s[r] for s in new_a]) for r in range(4))
    b_states = tuple(jnp.stack([s[r] for s in new_b]) for r in range(3))
    c_state = jnp.stack(new_c)
    return xn.reshape(B, T, D), a_states, b_states, c_state


def kernel(x_prompt, x_sample, state_mlstm_C, state_mlstm_n, state_mlstm_m, state_mlstm_conv, state_pool,
           cache_k, cache_v, cache_kidx, page_table, p_prompt, p_sample,
           norm_g, final_g, ple_w, ple_gate_w,
           a_w_up, a_conv_w, a_conv_b, a_w_q, a_w_k, a_w_v, a_w_if, a_b_if, a_w_o, a_b_o, a_norm_g,
           a_skip, a_w_down, b_w_in, b_w_out, c_w_up, c_w_grp, c_scale, c_w_down):
    D = x_prompt.shape[-1]
    nh, dh = cache_k.shape[3], cache_k.shape[4]
    di = cache_kidx.shape[-1]
    aw = nh * dh
    nhi = (b_w_in.shape[-1] - 4 * aw - di) // (di + 1)
    dims = (nh, dh, nhi, di)
    nha = state_mlstm_C.shape[2]
    bf = lambda a: a.astype(BF16)

    def pad_cols(a, n):
        return jnp.pad(a, [(0, 0)] * (a.ndim - 1) + [(0, n - a.shape[-1])])

    o1, o2, o3, o4, o5, o6 = np.cumsum([aw, aw, aw, aw, nhi * di, di])
    w = dict(
        norm_g=norm_g, final_g=final_g, ple_w=bf(ple_w), ple_gate_w=bf(ple_gate_w),
        a_w_up=bf(a_w_up), a_conv_w=a_conv_w, a_conv_b=a_conv_b,
        a_w_q=bf(a_w_q), a_w_k=bf(a_w_k), a_w_v=bf(a_w_v), a_w_o=bf(a_w_o), a_b_o=a_b_o,
        a_w_if_pad=pad_cols(a_w_if, LANES), a_b_if_pad=pad_cols(a_b_if, LANES),
        a_norm_g=a_norm_g, a_skip=a_skip, a_w_down=bf(a_w_down),
        b_wq=bf(b_w_in[..., :o1]), b_wk=bf(b_w_in[..., o1:o2]), b_wv=bf(b_w_in[..., o2:o3]),
        b_wg=bf(b_w_in[..., o3:o4]), b_wqi=bf(b_w_in[..., o4:o5]),
        b_wki_pad=bf(pad_cols(b_w_in[..., o5:o6], LANES)),
        b_wwi_pad=bf(pad_cols(b_w_in[..., o6:], LANES)),
        b_w_out=bf(b_w_out), c_w_up=bf(c_w_up), c_w_grp=bf(c_w_grp), c_scale=c_scale,
        c_w_down=bf(c_w_down),
    )

    Bp = x_prompt.shape[0]
    na, nc = state_mlstm_C.shape[0], state_pool.shape[0]
    zeros_a = (jnp.zeros((na, Bp) + state_mlstm_C.shape[2:], F32),
               jnp.zeros((na, Bp) + state_mlstm_n.shape[2:], F32),
               jnp.zeros((na, Bp) + state_mlstm_m.shape[2:], F32),
               jnp.zeros((na, Bp) + state_mlstm_conv.shape[2:], F32))
    zeros_c = jnp.zeros((nc, Bp) + state_pool.shape[2:], F32)
    y_p, a_p, b_p, c_p = _run_group(x_prompt, p_prompt, 0, zeros_a, None, zeros_c, w, dims)

    past = page_table.shape[1] * cache_k.shape[2]
    y_s, a_s, b_s, c_s = _run_group(
        x_sample, p_sample, past,
        (state_mlstm_C, state_mlstm_n, state_mlstm_m, state_mlstm_conv),
        (cache_k, cache_v, cache_kidx, page_table), state_pool, w, dims)
    return (y_p, y_s, *a_p, *b_p, c_p, *a_s, *b_s, c_s)
```

```python
import functools
import math

import numpy as np
import jax
import jax.numpy as jnp
from jax import lax
from jax.experimental import pallas as pl
from jax.experimental.pallas import tpu as pltpu

EPS = 1e-6
ROPE_THETA = 10000.0
TOPK_MAX = 256
POOL_WINDOWS = (2, 4, 8, 16)
CONV_W = 4
MLSTM_CHUNK = 256
NEG = -1e30
LANES = 128
SUBLANES = 8
VMEM_LIMIT_BYTES = 56 * 1024 * 1024
ATT_TQ, ATT_TK = 1024, 512
IDX_TQ = 256
PAGES_PER_STEP = 8
IDX_PAGES_PER_STEP = 32
ROW_TILES = (1024, 512, 256)
COL_TILES = (1024, 512, 256, 128)

F32 = jnp.float32
BF16 = jnp.bfloat16
INT_MIN = -2 ** 31


def _params(*sem):
    return pltpu.CompilerParams(dimension_semantics=sem, vmem_limit_bytes=VMEM_LIMIT_BYTES)


def _sigmoid(x):
    return 1.0 / (1.0 + jnp.exp(-x))


def _silu(x):
    return x * _sigmoid(x)


def _log_sigmoid(x):
    return jnp.minimum(x, 0.0) - jnp.log(1.0 + jnp.exp(-jnp.abs(x)))


def _dot(a, b):
    return jnp.dot(a, b, preferred_element_type=F32)


def _dot_nt(a, b):
    return lax.dot_general(a, b, (((1,), (1,)), ((), ())), preferred_element_type=F32)


def _dot_tn(a, b):
    return lax.dot_general(a, b, (((0,), (0,)), ((), ())), preferred_element_type=F32)


def _pick(n, cands):
    for c in cands:
        if n % c == 0:
            return c
    return n


def _rope_tile(a, cos, sin_signed):
    lane = lax.broadcasted_iota(jnp.int32, (1, LANES), 1)
    first_half = (lane % 64) < 32
    pieces = []
    for gi in range(a.shape[1] // LANES):
        ag = a[:, gi * LANES:(gi + 1) * LANES]
        ahead = pltpu.roll(ag, LANES - 32, 1)
        behind = pltpu.roll(ag, 32, 1)
        rot = jnp.where(first_half, ahead, behind)
        pieces.append(ag * cos + rot * sin_signed)
    return pieces[0] if len(pieces) == 1 else jnp.concatenate(pieces, axis=1)


def _rmsnorm_body(x_ref, g_ref, o_ref):
    xf = x_ref[...]
    y = xf * lax.rsqrt(jnp.mean(xf * xf, axis=-1, keepdims=True) + EPS) * g_ref[...]
    o_ref[...] = y.astype(o_ref.dtype)


def rmsnorm_cast(x, g):
    M, D = x.shape
    tm = _pick(M, ROW_TILES)
    return pl.pallas_call(
        _rmsnorm_body,
        grid=(M // tm,),
        in_specs=[pl.BlockSpec((tm, D), lambda i: (i, 0)), pl.BlockSpec((1, D), lambda i: (0, 0))],
        out_specs=pl.BlockSpec((tm, D), lambda i: (i, 0)),
        out_shape=jax.ShapeDtypeStruct((M, D), BF16),
        compiler_params=_params("parallel"),
        name="rmsnorm_cast",
    )(x, g.reshape(1, D))


def _project_body(*refs, rope, n_out, scale, head_dim):
    if rope:
        x_ref, w_ref, cos_ref, sin_ref = refs[:4]
        outs = refs[4:4 + n_out]
    else:
        x_ref, w_ref = refs[:2]
        outs = refs[2:2 + n_out]
    acc = _dot(x_ref[...], w_ref[...])
    if rope:
        acc = _rope_tile(acc, cos_ref[...], sin_ref[...])
    if scale != 1.0:
        acc = acc * scale
    for o in outs:
        if len(o.shape) == 3:
            o[0] = acc.T.astype(o.dtype)
        elif len(o.shape) == 4:
            npad = o.shape[3] - head_dim
            if npad:
                lane = lax.broadcasted_iota(jnp.int32, (acc.shape[0], npad), 1)
                pad = jnp.where(lane == 0, 1.0, 0.0)
            for hh in range(o.shape[1]):
                piece = acc[:, hh * head_dim:(hh + 1) * head_dim]
                if npad:
                    piece = jnp.concatenate([piece, pad], axis=1)
                o[0, hh] = piece.astype(o.dtype)
        else:
            o[...] = acc.astype(o.dtype)


def project(x, w, out_dtypes, rope_tabs=None, tab_blocks=1, scale=1.0, head_major=None,
            row_tile=None):
    M, D = x.shape
    N = w.shape[1]
    tm = row_tile if row_tile is not None else _pick(M, ROW_TILES)
    tn = _pick(N, COL_TILES)
    out_specs = [pl.BlockSpec((tm, tn), lambda i, j: (i, j)) for _ in out_dtypes]
    out_shape = [jax.ShapeDtypeStruct((M, N), dt) for dt in out_dtypes]
    head_dim = 0
    if head_major is not None:
        hb, ht, head_dim, flags = head_major
        assert ht % tm == 0
        tpb = ht // tm
        for idx, width in enumerate(flags):
            if width == "T":
                out_specs[idx] = pl.BlockSpec((1, tn, tm), lambda i, j: (i // tpb, j, i % tpb))
                out_shape[idx] = jax.ShapeDtypeStruct((hb, N, ht), out_dtypes[idx])
            elif width:
                out_specs[idx] = pl.BlockSpec((1, tn // head_dim, tm, width),
                                              lambda i, j: (i // tpb, j, i % tpb, 0))
                out_shape[idx] = jax.ShapeDtypeStruct((hb, N // head_dim, ht, width), out_dtypes[idx])
    in_specs = [pl.BlockSpec((tm, D), lambda i, j: (i, 0)),
                pl.BlockSpec((D, tn), lambda i, j: (0, j))]
    args = [x, w]
    if rope_tabs is not None:
        for t in rope_tabs:
            in_specs.append(pl.BlockSpec((tm, LANES), lambda i, j: (i % tab_blocks, 0)))
            args.append(t)
    return pl.pallas_call(
        functools.partial(_project_body, rope=rope_tabs is not None, n_out=len(out_dtypes),
                          scale=scale, head_dim=head_dim),
        grid=(M // tm, N // tn),
        in_specs=in_specs,
        out_specs=out_specs,
        out_shape=out_shape,
        compiler_params=_params("parallel", "parallel"),
        name="project",
    )(*args)


def _matmul_body(*refs, has_bias, act):
    if has_bias:
        x_ref, w_ref, b_ref, o_ref = refs
    else:
        x_ref, w_ref, o_ref = refs
    acc = _dot(x_ref[...], w_ref[...])
    if has_bias:
        acc = acc + b_ref[...]
    if act == "sigmoid":
        acc = _sigmoid(acc)
    o_ref[...] = acc.astype(o_ref.dtype)


def matmul(x, w, out_dtype, bias=None, act=None):
    M, K = x.shape
    N = w.shape[1]
    tm = _pick(M, ROW_TILES)
    tn = _pick(N, COL_TILES)
    in_specs = [pl.BlockSpec((tm, K), lambda i, j: (i, 0)),
                pl.BlockSpec((K, tn), lambda i, j: (0, j))]
    args = [x, w]
    if bias is not None:
        in_specs.append(pl.BlockSpec((1, tn), lambda i, j: (0, j)))
        args.append(bias.reshape(1, N).astype(F32))
    return pl.pallas_call(
        functools.partial(_matmul_body, has_bias=bias is not None, act=act),
        grid=(M // tm, N // tn),
        in_specs=in_specs,
        out_specs=pl.BlockSpec((tm, tn), lambda i, j: (i, j)),
        out_shape=jax.ShapeDtypeStruct((M, N), out_dtype),
        compiler_params=_params("parallel", "parallel"),
        name="matmul",
    )(*args)


CONV_HALO = 8
POOL_HALO = 16


def _conv_body(xb_ref, st_ref, w_ref, b_ref, gwh_ref, gwl_ref, gb_ref, xc_ref, gate_ref, ext_ref,
               *, tb, n_li):
    c = pl.program_id(1)
    nst = CONV_W - 1
    lo = CONV_HALO - nst

    @pl.when(c == 0)
    def _():
        ext_ref[:, lo:CONV_HALO, :] = st_ref[...]

    @pl.when(c > 0)
    def _():
        ext_ref[:, lo:CONV_HALO, :] = ext_ref[:, lo + tb:CONV_HALO + tb, :]

    ext_ref[:, CONV_HALO:CONV_HALO + tb, :] = xb_ref[...].astype(F32)
    acc = b_ref[...][None] + ext_ref[:, lo:lo + tb, :] * w_ref[0:1, :][None]
    for j in range(1, CONV_W):
        acc = acc + ext_ref[:, lo + j:lo + j + tb, :] * w_ref[j:j + 1, :][None]
    xc = _silu(acc)
    xc_ref[...] = xc.astype(xc_ref.dtype)
    x2 = xc.reshape(xc.shape[0] * tb, xc.shape[2])
    hi = x2.astype(BF16)
    lo = (x2 - hi.astype(F32)).astype(BF16)
    g = _dot(hi, gwh_ref[...]) + (_dot(hi, gwl_ref[...]) + _dot(lo, gwh_ref[...])) + gb_ref[...]
    col = lax.broadcasted_iota(jnp.int32, g.shape, 1)
    g = jnp.where(col < n_li, g, _log_sigmoid(g))
    gate_ref[...] = g.reshape(gate_ref.shape)


def conv_silu_gates(xb, state, w, b, gate_w, gate_b, n_li):
    B, T, E = xb.shape
    bb, tb = (1, 256) if T % 256 == 0 else (B, T)
    gw_hi = gate_w.astype(BF16)
    gw_lo = (gate_w - gw_hi.astype(F32)).astype(BF16)
    fixed = lambda i, c: (0, 0)
    return pl.pallas_call(
        functools.partial(_conv_body, tb=tb, n_li=n_li),
        grid=(B // bb, T // tb),
        in_specs=[pl.BlockSpec((bb, tb, E), lambda i, c: (i, c, 0)),
                  pl.BlockSpec((bb, CONV_W - 1, E), lambda i, c: (i, 0, 0)),
                  pl.BlockSpec((CONV_W, E), fixed),
                  pl.BlockSpec((1, E), fixed),
                  pl.BlockSpec((E, LANES), fixed),
                  pl.BlockSpec((E, LANES), fixed),
                  pl.BlockSpec((1, LANES), fixed)],
        out_specs=[pl.BlockSpec((bb, tb, E), lambda i, c: (i, c, 0)),
                   pl.BlockSpec((bb, tb, LANES), lambda i, c: (i, c, 0))],
        out_shape=[jax.ShapeDtypeStruct((B, T, E), BF16),
                   jax.ShapeDtypeStruct((B, T, LANES), F32)],
        scratch_shapes=[pltpu.VMEM((bb, CONV_HALO + tb, E), F32)],
        compiler_params=_params("parallel", "arbitrary"),
        name="conv_silu_gates",
    )(xb, state.astype(F32), w, b.reshape(1, E), gw_hi, gw_lo, gate_b.reshape(1, LANES))


def _mlstm_body(q_ref, k_ref, v_ref, gc_ref, gr_ref, c0_ref, n0_ref, m0_ref, ng_ref,
                h_ref, c_out_ref, n_out_ref, m_out_ref, c_s, n_s, m_s, *, nh, dqk, dv, lc):
    c = pl.program_id(1)

    @pl.when(c == 0)
    def _():
        c_s[...] = c0_ref[0]
        n_s[...] = n0_ref[0]
        m_s[...] = m0_ref[0]

    row = lax.broadcasted_iota(jnp.int32, (lc, lc), 0)
    col = lax.broadcasted_iota(jnp.int32, (lc, lc), 1)
    causal = row >= col
    scale = dqk ** -0.5
    gc = gc_ref[0]
    gr = gr_ref[0]
    for h in range(nh):
        q = (q_ref[0, :, h * dqk:(h + 1) * dqk].astype(F32) * scale).astype(BF16)
        k = k_ref[0, :, h * dqk:(h + 1) * dqk]
        v = v_ref[0, :, h * dv:(h + 1) * dv]
        li_c, lf_c = gc[:, h:h + 1], gc[:, nh + h:nh + h + 1]
        li_r, lf_r = gr[h:h + 1, :], gr[nh + h:nh + h + 1, :]
        b_c = jnp.sum(jnp.where(causal, lf_r, 0.0), axis=1, keepdims=True)
        b_r = jnp.sum(jnp.where(row <= col, lf_c, 0.0), axis=0, keepdims=True)
        m_prev = m_s[h:h + 1, 0:1]
        dm = jnp.where(causal, b_c - b_r + li_r, NEG)
        m_inter = b_c + m_prev
        m_t = jnp.maximum(m_inter, jnp.max(dm, axis=1, keepdims=True))
        a = _dot_nt(q, k) * jnp.exp(dm - m_t)
        inter = jnp.exp(m_inter - m_t)
        c_prev = c_s[h]
        n_prev = n_s[h:h + 1, :]
        num = _dot(a.astype(BF16), v) + inter * _dot(q, c_prev.astype(BF16))
        qn = jnp.sum(q.astype(F32) * n_prev, axis=1, keepdims=True)
        den = jnp.sum(a, axis=1, keepdims=True) + inter * qn
        hh = num / jnp.maximum(jnp.abs(den), jnp.exp(-m_t))
        hn = hh * lax.rsqrt(jnp.mean(hh * hh, axis=-1, keepdims=True) + EPS) \
            * ng_ref[:, h * dv:(h + 1) * dv]
        h_ref[0, :, h * dv:(h + 1) * dv] = hn.astype(h_ref.dtype)
        b_last = b_c[lc - 1:lc, :]
        g_r = b_last - b_r + li_r
        g_c = b_last - b_c + li_c
        m_new = jnp.maximum(b_last + m_prev, jnp.max(g_r, axis=1, keepdims=True))
        w_r = jnp.exp(g_r - m_new)
        w_c = jnp.exp(g_c - m_new)
        decay = jnp.exp(b_last + m_prev - m_new)
        wv = (w_c * v.astype(F32)).astype(BF16)
        c_s[h] = decay * c_prev + _dot_tn(k, wv)
        wr8 = jnp.broadcast_to(w_r, (SUBLANES, lc)).astype(BF16)
        n_s[h:h + 1, :] = decay * n_prev + _dot(wr8, k)[0:1, :]
        m_s[h:h + 1, :] = jnp.broadcast_to(m_new, (1, LANES))

    @pl.when(c == pl.num_programs(1) - 1)
    def _():
        c_out_ref[0] = c_s[...]
        n_out_ref[0] = n_s[...]
        m_out_ref[0] = m_s[...]


def mlstm_recurrence(q, k, v, gcol, grow, c0_all, layer, n0, m0, norm_g):
    B, T, _ = q.shape
    nh, dqk, dv = c0_all.shape[2], c0_all.shape[3], c0_all.shape[4]
    lc = min(MLSTM_CHUNK, T)
    m0b = jnp.broadcast_to(m0[:, :, None], (B, nh, LANES)).astype(F32)
    h, c_new, n_new, m_new = pl.pallas_call(
        functools.partial(_mlstm_body, nh=nh, dqk=dqk, dv=dv, lc=lc),
        grid=(B, T // lc),
        in_specs=[pl.BlockSpec((1, lc, nh * dqk), lambda b, c: (b, c, 0)),
                  pl.BlockSpec((1, lc, nh * dqk), lambda b, c: (b, c, 0)),
                  pl.BlockSpec((1, lc, nh * dv), lambda b, c: (b, c, 0)),
                  pl.BlockSpec((1, lc, LANES), lambda b, c: (b, c, 0)),
                  pl.BlockSpec((1, 2 * nh, lc), lambda b, c: (b, 0, c)),
                  pl.BlockSpec((None, 1, nh, dqk, dv), lambda b, c: (layer, b, 0, 0, 0)),
                  pl.BlockSpec((1, nh, dqk), lambda b, c: (b, 0, 0)),
                  pl.BlockSpec((1, nh, LANES), lambda b, c: (b, 0, 0)),
                  pl.BlockSpec((1, nh * dv), lambda b, c: (0, 0))],
        out_specs=[pl.BlockSpec((1, lc, nh * dv), lambda b, c: (b, c, 0)),
                   pl.BlockSpec((1, nh, dqk, dv), lambda b, c: (b, 0, 0, 0)),
                   pl.BlockSpec((1, nh, dqk), lambda b, c: (b, 0, 0)),
                   pl.BlockSpec((1, nh, LANES), lambda b, c: (b, 0, 0))],
        out_shape=[jax.ShapeDtypeStruct((B, T, nh * dv), BF16),
                   jax.ShapeDtypeStruct((B, nh, dqk, dv), F32),
                   jax.ShapeDtypeStruct((B, nh, dqk), F32),
                   jax.ShapeDtypeStruct((B, nh, LANES), F32)],
        scratch_shapes=[pltpu.VMEM((nh, dqk, dv), F32),
                        pltpu.VMEM((nh, dqk), F32),
                        pltpu.VMEM((nh, LANES), F32)],
        compiler_params=_params("parallel", "arbitrary"),
        name="mlstm_recurrence",
    )(q, k, v, gcol, grow, c0_all.astype(F32), n0.astype(F32), m0b, norm_g.reshape(1, nh * dv))
    return h, c_new, n_new, m_new[:, :, 0]


def _pool_body(xb_ref, st_ref, wg_ref, sc_ref, y_ref, ext_ref, *, tb, pos0, nbuf):
    c = pl.program_id(1)
    lo = POOL_HALO - nbuf
    bb = xb_ref.shape[0]
    gw = wg_ref.shape[1]

    @pl.when(c == 0)
    def _():
        ext_ref[:, lo:POOL_HALO, :] = st_ref[...]

    @pl.when(c > 0)
    def _():
        ext_ref[:, lo:POOL_HALO, :] = ext_ref[:, lo + tb:POOL_HALO + tb, :]

    ext_ref[:, POOL_HALO:POOL_HALO + tb, :] = xb_ref[...].astype(F32)
    pos = pos0 + c * tb + lax.broadcasted_iota(jnp.int32, (1, tb, 1), 1)
    for g, w in enumerate(POOL_WINDOWS):
        cols = slice(g * gw, (g + 1) * gw)
        cur = ext_ref[:, POOL_HALO:POOL_HALO + tb, cols]
        win = cur
        for i in range(1, w):
            win = win + ext_ref[:, POOL_HALO - i:POOL_HALO - i + tb, cols]
        cnt = jnp.minimum(w, pos + 1).astype(F32)
        d = (win / cnt - cur).reshape(bb * tb, gw).astype(BF16)
        yg = _dot(d, wg_ref[g]) * sc_ref[:, cols]
        y_ref[:, :, cols] = yg.reshape(bb, tb, gw).astype(y_ref.dtype)


def pool_mix(xb, state, w_grp, scale, pos0):
    B, T, E = xb.shape
    nbuf = state.shape[1]
    bb, tb = (1, 256) if T % 256 == 0 else (B, T)
    ng, gw = w_grp.shape[0], w_grp.shape[1]
    return pl.pallas_call(
        functools.partial(_pool_body, tb=tb, pos0=pos0, nbuf=nbuf),
        grid=(B // bb, T // tb),
        in_specs=[pl.BlockSpec((bb, tb, E), lambda i, c: (i, c, 0)),
                  pl.BlockSpec((bb, nbuf, E), lambda i, c: (i, 0, 0)),
                  pl.BlockSpec((ng, gw, gw), lambda i, c: (0, 0, 0)),
                  pl.BlockSpec((1, E), lambda i, c: (0, 0))],
        out_specs=pl.BlockSpec((bb, tb, E), lambda i, c: (i, c, 0)),
        out_shape=jax.ShapeDtypeStruct((B, T, E), BF16),
        scratch_shapes=[pltpu.VMEM((bb, POOL_HALO + tb, E), F32)],
        compiler_params=_params("parallel", "arbitrary"),
        name="pool_mix",
    )(xb, state.astype(F32), w_grp, scale.reshape(1, E))


def _tail_body(*refs, kind):
    if kind == "mlstm":
        hn_ref, o_ref, xc_ref, z_ref, skip_ref = refs[:5]
        rest = refs[5:]
        mix = (o_ref[...].astype(F32) * hn_ref[...].astype(F32)
               + skip_ref[...] * xc_ref[...].astype(F32)) * _silu(z_ref[...].astype(F32))
    else:
        y_ref, z_ref = refs[:2]
        rest = refs[2:]
        mix = y_ref[...].astype(F32) * _silu(z_ref[...].astype(F32))
    x_ref, p_ref, wd_ref, gw_ref, pw_ref, ng_ref, xo_ref, no_ref = rest
    x1 = x_ref[...] + _dot(mix.astype(BF16), wd_ref[...])
    gate = _sigmoid(_dot(x1.astype(BF16), gw_ref[...]))
    x2 = x1 + gate * _dot(p_ref[...].astype(BF16), pw_ref[...])
    xo_ref[...] = x2
    no_ref[...] = (x2 * lax.rsqrt(jnp.mean(x2 * x2, axis=-1, keepdims=True) + EPS)
                   * ng_ref[...]).astype(no_ref.dtype)


def layer_tail(kind, mix_inputs, x, p_all, layer, w_down, gate_w, ple_w, next_g, next_dtype,
               skip=None):
    M, D = x.shape
    tm = _pick(M, (512, 256))
    row = lambda i: (i, 0)
    fixed = lambda i: (0, 0)
    in_specs, args = [], []
    for a in mix_inputs:
        in_specs.append(pl.BlockSpec((tm, a.shape[1]), row))
        args.append(a)
    if kind == "mlstm":
        in_specs.append(pl.BlockSpec((1, skip.shape[-1]), fixed))
        args.append(skip.reshape(1, -1))
    in_specs += [pl.BlockSpec((tm, D), row),
                 pl.BlockSpec((None, tm, p_all.shape[2]), lambda i: (layer, i, 0)),
                 pl.BlockSpec(w_down.shape, fixed), pl.BlockSpec(gate_w.shape, fixed),
                 pl.BlockSpec(ple_w.shape, fixed), pl.BlockSpec((1, D), fixed)]
    args += [x, p_all, w_down, gate_w, ple_w, next_g.reshape(1, D)]
    out_specs = [pl.BlockSpec((tm, D), row), pl.BlockSpec((tm, D), row)]
    out_shape = [jax.ShapeDtypeStruct((M, D), F32), jax.ShapeDtypeStruct((M, D), next_dtype)]
    return pl.pallas_call(
        functools.partial(_tail_body, kind=kind),
        grid=(M // tm,),
        in_specs=in_specs, out_specs=out_specs, out_shape=out_shape,
        compiler_params=_params("parallel"),
        name="layer_tail_" + kind,
    )(*args)


def _sortable_key(s):
    s = jnp.where(s == 0.0, 0.0, s)
    bits = lax.bitcast_convert_type(s, jnp.int32)
    return jnp.where(bits < 0, bits ^ jnp.int32(0x7FFFFFFF), bits)


COUNT_STRIP_VREGS = 32


def _count(keys_ref, nvalid, pred):
    rows, cw = keys_ref.shape[1], keys_ref.shape[2]
    strip = max(SUBLANES, min(rows, COUNT_STRIP_VREGS * SUBLANES * LANES // cw))

    def body(kc, acc):
        parts = []
        for r0 in range(0, rows, strip):
            rs = slice(r0, min(r0 + strip, rows))
            hit = jnp.where(pred(keys_ref[kc, rs, :], kc, rs), 1.0, 0.0)
            part = hit[:, 0:LANES]
            for g in range(1, cw // LANES):
                part = part + hit[:, g * LANES:(g + 1) * LANES]
            parts.append(part)
        return acc + (parts[0] if len(parts) == 1 else jnp.concatenate(parts, axis=0))

    acc = lax.fori_loop(0, nvalid, body, jnp.zeros((rows, LANES), F32))
    return jnp.sum(acc, axis=1, keepdims=True)


def _select_topk(keys_ref, cidx_ref, nvalid, k_top, idx_bits):
    rows, cw = keys_ref.shape[1], keys_ref.shape[2]
    kf = float(k_top)
    imin = jnp.int32(INT_MIN)

    def bit_body(i, tau_u):
        cand_u = tau_u | jnp.left_shift(jnp.int32(1), 31 - i)
        cand = cand_u ^ imin
        cnt = _count(keys_ref, nvalid, lambda kk, kc, rs: kk >= cand[rs])
        return jnp.where(cnt >= kf, cand_u, tau_u)

    tau_u = lax.fori_loop(0, 32, bit_body, jnp.zeros((rows, 1), jnp.int32))
    tau = tau_u ^ imin
    n_gt = _count(keys_ref, nvalid, lambda kk, kc, rs: kk > tau[rs])
    n_ge = _count(keys_ref, nvalid, lambda kk, kc, rs: kk >= tau[rs])
    need = kf - n_gt
    short = tau_u == 0
    cidx_ref[...] = jnp.broadcast_to(jnp.where(short, -1, 2 ** 30), cidx_ref.shape)
    excess = jnp.max(jnp.where((n_ge > kf) & jnp.logical_not(short), 1.0, 0.0))

    @pl.when(excess > 0.5)
    def _():
        lane = lax.broadcasted_iota(jnp.int32, (1, cw), 1)

        def idx_body(i, cut):
            cand = cut | jnp.left_shift(jnp.int32(1), idx_bits - 1 - i)
            cnt = _count(keys_ref, nvalid,
                         lambda kk, kc, rs: (kk == tau[rs]) & ((kc * cw + lane) < cand[rs]))
            return jnp.where(cnt < need, cand, cut)

        cut = lax.fori_loop(0, idx_bits, idx_body, jnp.zeros((rows, 1), jnp.int32))
        cidx_ref[...] = jnp.broadcast_to(jnp.where(short, -1, cut), cidx_ref.shape)

    return tau


def _selected(kk, colv, tau, cut):
    return (kk > tau) | ((kk == tau) & (colv <= cut))


def _n_causal_chunks(qb, tq, tk):
    return lax.div(qb * tq + (tq - 1), jnp.int32(tk)) + 1


def _idx_prompt_body(qi_ref, wi_ref, ki_ref, bias_ref, keys_ref, cidx_ref, *,
                     nh, dh, tq, tk, nk, k_top, idx_bits):
    per = keys_ref.shape[2] // tk
    gw = per * tk
    qb = pl.program_id(1)
    nvalid = _n_causal_chunks(qb, tq, gw)
    rowpos = qb * tq + lax.broadcasted_iota(jnp.int32, (tq, 1), 0)
    lane = lax.broadcasted_iota(jnp.int32, (1, tk), 1)
    wi = wi_ref[0]

    def score_group(gc, carry):
        for part in range(per):
            kt = ki_ref[0, gc * per + part]
            s = jnp.zeros((tq, tk), F32)
            for h in range(nh):
                rel = jnp.maximum(_dot_nt(qi_ref[0, :, h * dh:(h + 1) * dh], kt), 0.0)
                s = s + rel * wi[:, h:h + 1]
            colv = gc * gw + part * tk + lane
            keys_ref[gc, :, part * tk:(part + 1) * tk] = jnp.where(
                colv <= rowpos, _sortable_key(s), jnp.int32(INT_MIN))
        return carry

    lax.fori_loop(0, nvalid, score_group, 0)
    tau = _select_topk(keys_ref, cidx_ref, nvalid, k_top, idx_bits)
    cut = cidx_ref[:, 0:1]

    def write_group(gc, carry):
        for part in range(per):
            kk = keys_ref[gc, :, part * tk:(part + 1) * tk]
            sel = _selected(kk, gc * gw + part * tk + lane, tau, cut)
            bias_ref[0, 0, gc * per + part] = jnp.where(sel, 0.0, NEG).astype(bias_ref.dtype)
        return carry

    lax.fori_loop(0, nvalid, write_group, 0)

    def fill_chunk(kc, carry):
        bias_ref[0, 0, kc] = jnp.full((tq, tk), NEG, bias_ref.dtype)
        return carry

    lax.fori_loop(nvalid * per, nk, fill_chunk, 0)


def indexer_bias_prompt(qi, wi, ki, k_top):
    B, T, _ = qi.shape
    dh = ki.shape[-1]
    nh = qi.shape[-1] // dh
    tq, tk = min(IDX_TQ, T), min(ATT_TK, T)
    tqa = min(ATT_TQ, T)
    sub = tqa // tq
    nq, nk = T // tq, T // tk
    per = 2 if nk % 2 == 0 else 1
    idx_bits = int(T).bit_length() + 1
    return pl.pallas_call(
        functools.partial(_idx_prompt_body, nh=nh, dh=dh, tq=tq, tk=tk, nk=nk, k_top=k_top,
                          idx_bits=idx_bits),
        grid=(B, nq),
        in_specs=[pl.BlockSpec((1, tq, nh * dh), lambda b, i: (b, i, 0)),
                  pl.BlockSpec((1, tq, LANES), lambda b, i: (b, i, 0)),
                  pl.BlockSpec((1, nk, tk, dh), lambda b, i: (b, 0, 0, 0))],
        out_specs=pl.BlockSpec((1, 1, nk, tq, tk), lambda b, i: (b, i // sub, 0, i % sub, 0)),
        out_shape=jax.ShapeDtypeStruct((B, T // tqa, nk, tqa, tk), BF16),
        scratch_shapes=[pltpu.VMEM((nk // per, tq, per * tk), jnp.int32),
                        pltpu.VMEM((tq, LANES), jnp.int32)],
        compiler_params=_params("parallel", "arbitrary"),
        name="indexer_bias_prompt",
    )(qi, wi, ki.reshape(B, nk, tk, dh))


def _attn_prompt_body(sched_ref, q_ref, k_ref, v_ref, bias_ref, o_ref, m_s, acc_s, *,
                      nh, dh, tq, tk):
    step = pl.program_id(1)
    kb = sched_ref[1, step]

    @pl.when(kb == 0)
    def _():
        m_s[...] = jnp.full(m_s.shape, NEG, F32)
        acc_s[...] = jnp.zeros(acc_s.shape, F32)

    bias = bias_ref[0, 0, 0].astype(F32)
    for h in range(nh):
        s = _dot_nt(q_ref[0, h], k_ref[0, h]) + bias
        m_prev = m_s[h]
        m_new = jnp.maximum(m_prev, jnp.max(s, axis=1, keepdims=True))
        alpha = jnp.exp2(m_prev - m_new)
        p = jnp.exp2(s - jnp.concatenate([m_new] * (tk // LANES), axis=1))
        acc_s[h] = alpha * acc_s[h] + _dot(p.astype(BF16), v_ref[0, h])
        m_s[h] = m_new

    @pl.when(sched_ref[2, step] == 1)
    def _():
        for h in range(nh):
            acc = acc_s[h]
            o_ref[0, :, h * dh:(h + 1) * dh] = (acc[:, :dh] / acc[:, dh:dh + 1]).astype(o_ref.dtype)


def attention_prompt(q, k, v, bias):
    B, nh, T, dh = q.shape
    assert tuple(v.shape) == (B, nh, T, LANES) and dh < LANES
    _, nq, nk, tq, tk = bias.shape
    sched = [(i, j, int(j == (i * tq + tq - 1) // tk))
             for i in range(nq) for j in range((i * tq + tq - 1) // tk + 1)]
    sched = jnp.asarray(np.array(sched, np.int32).T)
    grid_spec = pltpu.PrefetchScalarGridSpec(
        num_scalar_prefetch=1,
        grid=(B, sched.shape[1]),
        in_specs=[pl.BlockSpec((1, nh, tq, dh), lambda b, s, sc: (b, 0, sc[0, s], 0)),
                  pl.BlockSpec((1, nh, tk, dh), lambda b, s, sc: (b, 0, sc[1, s], 0)),
                  pl.BlockSpec((1, nh, tk, LANES), lambda b, s, sc: (b, 0, sc[1, s], 0)),
                  pl.BlockSpec((1, 1, 1, tq, tk), lambda b, s, sc: (b, sc[0, s], sc[1, s], 0, 0))],
        out_specs=pl.BlockSpec((1, tq, nh * dh), lambda b, s, sc: (b, sc[0, s], 0)),
        scratch_shapes=[pltpu.VMEM((nh, tq, LANES), F32),
                        pltpu.VMEM((nh, tq, LANES), F32)],
    )
    return pl.pallas_call(
        functools.partial(_attn_prompt_body, nh=nh, dh=dh, tq=tq, tk=tk),
        grid_spec=grid_spec,
        out_shape=jax.ShapeDtypeStruct((B, T, nh * dh), BF16),
        compiler_params=_params("parallel", "arbitrary"),
        name="attention_prompt",
    )(sched, q, k, v, bias)


def _idx_sample_body(pt_ref, qi_ref, wi_ref, knew_ref, *rest, nh, ts, npg, nsteps, k_top,
                     idx_bits):
    page_refs = rest[:npg]
    bias_ref, keys_ref, cidx_ref = rest[npg:]
    s = pl.program_id(1)
    cw = keys_ref.shape[2]
    wi = wi_ref[0]
    qi = qi_ref[0]

    def scores(kt):
        full = jnp.maximum(_dot(qi, kt), 0.0)
        out = jnp.zeros((ts, cw), F32)
        for h in range(nh):
            out = out + full[h * ts:(h + 1) * ts, :] * wi[:, h:h + 1]
        return out

    kt = jnp.concatenate([r[...] for r in page_refs], axis=1).astype(BF16)
    keys_ref[s] = _sortable_key(scores(kt))

    @pl.when(s == nsteps - 1)
    def _():
        lane = lax.broadcasted_iota(jnp.int32, (1, cw), 1)
        past = nsteps * cw
        rowpos = past + lax.broadcasted_iota(jnp.int32, (ts, 1), 0)
        valid = (past + lane) <= rowpos
        keys_ref[nsteps] = jnp.where(valid, _sortable_key(scores(knew_ref[0])),
                                     jnp.int32(INT_MIN))
        tau = _select_topk(keys_ref, cidx_ref, nsteps + 1, k_top, idx_bits)
        cut = cidx_ref[:, 0:1]
        for kc in range(nsteps + 1):
            sel = _selected(keys_ref[kc], kc * cw + lane, tau, cut)
            bias_ref[0, kc] = jnp.where(sel, 0.0, NEG).astype(bias_ref.dtype)


def _page_specs(n_pages_per_step, block, layer, first_step):
    specs = []
    for i in range(n_pages_per_step):
        def imap(b, s, pt, i=i):
            step = jnp.maximum(s - first_step, 0)
            return (layer, pt[b, step * n_pages_per_step + i]) + (0,) * len(block)
        specs.append(pl.BlockSpec((None, None) + tuple(block), imap))
    return specs


def indexer_bias_sample(qi_hq, wi, ki_new_t, cache_kidx_t, layer, page_table, k_top):
    B, n_pages = page_table.shape
    _, _, dh, page = cache_kidx_t.shape
    ts = wi.shape[1]
    nh = qi_hq.shape[1] // ts
    npg = min(IDX_PAGES_PER_STEP, n_pages)
    nsteps = n_pages // npg
    cw = npg * page
    idx_bits = int(n_pages * page + cw).bit_length() + 1
    grid_spec = pltpu.PrefetchScalarGridSpec(
        num_scalar_prefetch=1,
        grid=(B, nsteps),
        in_specs=[pl.BlockSpec((1, nh * ts, dh), lambda b, s, pt: (b, 0, 0)),
                  pl.BlockSpec((1, ts, LANES), lambda b, s, pt: (b, 0, 0)),
                  pl.BlockSpec((1, dh, cw), lambda b, s, pt: (b, 0, 0))]
        + _page_specs(npg, (dh, page), layer, 0),
        out_specs=pl.BlockSpec((1, nsteps + 1, ts, cw), lambda b, s, pt: (b, 0, 0, 0)),
        scratch_shapes=[pltpu.VMEM((nsteps + 1, ts, cw), jnp.int32),
                        pltpu.VMEM((ts, LANES), jnp.int32)],
    )
    return pl.pallas_call(
        functools.partial(_idx_sample_body, nh=nh, ts=ts, npg=npg, nsteps=nsteps, k_top=k_top,
                          idx_bits=idx_bits),
        grid_spec=grid_spec,
        out_shape=jax.ShapeDtypeStruct((B, nsteps + 1, ts, cw), F32),
        compiler_params=_params("parallel", "arbitrary"),
        name="indexer_bias_sample",
    )(page_table, qi_hq, wi, ki_new_t, *([cache_kidx_t] * npg))


def _attn_sample_body(pt_ref, qbd_ref, knew_ref, vnew_ref, bias_ref, *rest, nh, ts, npg, dh):
    kpages, vpages = rest[:npg], rest[npg:2 * npg]
    o_ref, m_s, l_s, acc_s = rest[2 * npg:]
    s = pl.program_id(1)
    rows = nh * ts

    @pl.when(s == 0)
    def _():
        m_s[...] = jnp.full(m_s.shape, NEG, F32)
        l_s[...] = jnp.zeros(l_s.shape, F32)
        acc_s[...] = jnp.zeros(acc_s.shape, F32)

    def update(kt, vt):
        n = kt.shape[1]
        bias = bias_ref[0, 0][:, :n]
        logits = _dot(qbd_ref[0], kt) + jnp.concatenate([bias] * nh, axis=0)
        m_prev = m_s[...]
        m_new = jnp.maximum(m_prev, jnp.max(logits, axis=1, keepdims=True))
        alpha = jnp.exp2(m_prev - m_new)
        p = jnp.exp2(logits - m_new)
        l_s[...] = alpha * l_s[...] + jnp.sum(p, axis=1, keepdims=True)
        acc_s[...] = alpha * acc_s[...] + _dot_nt(p.astype(BF16), vt)
        m_s[...] = m_new

    @pl.when(s == 0)
    def _():
        update(knew_ref[0], vnew_ref[0])

    def gather_pages(page_refs):
        return jnp.concatenate([r[...].reshape(nh * dh, r.shape[2]) for r in page_refs],
                               axis=1).astype(BF16)

    @pl.when(s > 0)
    def _():
        update(gather_pages(kpages), gather_pages(vpages))

    @pl.when(s == pl.num_programs(1) - 1)
    def _():
        o = acc_s[...] / l_s[...]
        lane = lax.broadcasted_iota(jnp.int32, (1, nh * dh), 1)
        res = jnp.zeros((ts, nh * dh), F32)
        for h in range(nh):
            own = (lane >= h * dh) & (lane < (h + 1) * dh)
            res = res + jnp.where(own, o[h * ts:(h + 1) * ts, :], 0.0)
        o_ref[0] = res.astype(o_ref.dtype)


def attention_sample(qbd, k_new_t, v_new_t, bias, cache_k_t, cache_v_t, layer, page_table, ts):
    B, n_pages = page_table.shape
    _, _, nh, dh, page = cache_k_t.shape
    width = nh * dh
    n_new = k_new_t.shape[2]
    npg = min(PAGES_PER_STEP, n_pages)
    cw = npg * page
    nsteps1 = n_pages // npg + 1
    per_chunk = bias.shape[3] // cw
    last_chunk = bias.shape[1] - 1
    rows = nh * ts

    def bias_map(b, s, pt):
        g = jnp.maximum(s - 1, 0)
        return (b, jnp.where(s == 0, last_chunk, g // per_chunk), 0,
                jnp.where(s == 0, 0, g % per_chunk))

    grid_spec = pltpu.PrefetchScalarGridSpec(
        num_scalar_prefetch=1,
        grid=(B, nsteps1),
        in_specs=[pl.BlockSpec((1, rows, width), lambda b, s, pt: (b, 0, 0)),
                  pl.BlockSpec((1, width, n_new), lambda b, s, pt: (b, 0, 0)),
                  pl.BlockSpec((1, width, n_new), lambda b, s, pt: (b, 0, 0)),
                  pl.BlockSpec((1, 1, ts, cw), bias_map)]
        + _page_specs(npg, (nh, dh, page), layer, 1) + _page_specs(npg, (nh, dh, page), layer, 1),
        out_specs=pl.BlockSpec((1, ts, width), lambda b, s, pt: (b, 0, 0)),
        scratch_shapes=[pltpu.VMEM((rows, 1), F32),
                        pltpu.VMEM((rows, 1), F32),
                        pltpu.VMEM((rows, width), F32)],
    )
    return pl.pallas_call(
        functools.partial(_attn_sample_body, nh=nh, ts=ts, npg=npg, dh=dh),
        grid_spec=grid_spec,
        out_shape=jax.ShapeDtypeStruct((B, ts, width), BF16),
        compiler_params=_params("parallel", "arbitrary"),
        name="attention_sample",
    )(page_table, qbd, k_new_t, v_new_t, bias, *([cache_k_t] * npg), *([cache_v_t] * npg))


def _pad_time(a, t_pad, value=0.0):
    return jnp.pad(a, ((0, 0), (0, t_pad - a.shape[1]), (0, 0)), constant_values=value)


def _mlstm_layer(xn, B, T, state, w, j):
    c_all, n0, m0, conv0 = state
    nh = c_all.shape[2]
    E = w["a_conv_w"].shape[-1]
    w_up = w["a_w_up"][j]
    xb, = project(xn,w_up[:, :E], (BF16,))
    z, = project(xn,w_up[:, E:], (BF16,))
    xc, gcol = conv_silu_gates(xb.reshape(B, T, E), conv0, w["a_conv_w"][j], w["a_conv_b"][j],
                               w["a_w_if_pad"][j], w["a_b_if_pad"][j], nh)
    xc = xc.reshape(B * T, E)
    q = matmul(xc, w["a_w_q"][j], BF16)
    k = matmul(xc, w["a_w_k"][j], BF16)
    v = matmul(xb, w["a_w_v"][j], BF16)
    o = matmul(xb, w["a_w_o"][j], BF16, bias=w["a_b_o"][j], act="sigmoid")
    q3, k3, v3 = q.reshape(B, T, -1), k.reshape(B, T, -1), v.reshape(B, T, -1)
    tp = T if T % LANES == 0 else ((T + LANES - 1) // LANES) * LANES
    if tp != T:
        q3, k3, v3 = _pad_time(q3, tp), _pad_time(k3, tp), _pad_time(v3, tp)
        pad_row = jnp.where(jnp.arange(LANES) < nh, NEG, 0.0).astype(F32)
        gcol = jnp.concatenate([gcol, jnp.broadcast_to(pad_row, (B, tp - T, LANES))], axis=1)
    grow = jnp.swapaxes(gcol[:, :, :2 * nh], 1, 2)
    hn, c_new, n_new, m_new = mlstm_recurrence(q3, k3, v3, gcol, grow, c_all, j, n0, m0,
                                               w["a_norm_g"][j])
    hn = hn[:, :T].reshape(B * T, -1)
    assert T >= CONV_W - 1
    conv_new = xb.reshape(B, T, E)[:, T - (CONV_W - 1):].astype(F32)
    return (hn, o, xc, z), (c_new, n_new, m_new, conv_new)


def _rope_tables(pos):
    half = 32
    inv = ROPE_THETA ** (-np.arange(half, dtype=np.float64) / half)
    ang = np.asarray(pos, np.float64)[:, None] * inv[None, :]
    cos = np.tile(np.cos(ang), (1, 4))
    sin = np.tile(np.concatenate([-np.sin(ang), np.sin(ang)], axis=1), (1, 2))
    return jnp.asarray(cos, F32), jnp.asarray(sin, F32)


def _dsa_project(xn, B, T, pos, w, j, nh, dh, nhi, di, head_major=False):
    M = B * T
    cos, sin = _rope_tables(pos)
    tm = _pick(T, ROW_TILES) if T % ROW_TILES[-1] == 0 else _pick(M, ROW_TILES)
    if T % tm == 0:
        tabs, tab_blocks = (cos, sin), T // tm
    else:
        tabs, tab_blocks = (jnp.tile(cos, (M // T, 1)), jnp.tile(sin, (M // T, 1))), M // tm
    rope = dict(rope_tabs=tabs, tab_blocks=tab_blocks, row_tile=tm)
    hm = (lambda *widths: (B, T, dh, widths)) if head_major else (lambda *widths: None)
    q, = project(xn,w["b_wq"][j], (BF16,), scale=dh ** -0.5 * math.log2(math.e),
                     head_major=hm(dh), **rope)
    wide = "T" if head_major else 0
    k32, k16 = project(xn,w["b_wk"][j], (F32, BF16), head_major=hm(wide, dh), **rope)
    v32, v16 = project(xn,w["b_wv"][j], (F32, BF16), head_major=hm(wide, LANES),
                           row_tile=tm)
    gate, = project(xn,w["b_wg"][j], (BF16,))
    qi, = project(xn,w["b_wqi"][j], (BF16,), **rope)
    ki32, ki16 = project(xn,w["b_wki_pad"][j], (F32, BF16), head_major=hm(wide, 0), **rope)
    wi, = project(xn,w["b_wwi_pad"][j], (F32,), scale=nhi ** -0.5 * di ** -0.5)
    if head_major:
        new = (jnp.transpose(k32.reshape(B, nh, dh, T), (0, 3, 1, 2)),
               jnp.transpose(v32.reshape(B, nh, dh, T), (0, 3, 1, 2)),
               jnp.transpose(ki32[:, :di], (0, 2, 1)))
    else:
        new = (k32.reshape(B, T, nh, dh), v32.reshape(B, T, nh, dh), ki32[:, :di].reshape(B, T, di))
    return q, k16, v16, gate, qi, ki16[:, :di].reshape(B, T, di), wi, new


def _dsa_layer_prompt(xn, B, T, w, j, dims):
    nh, dh, nhi, di = dims
    q, k16, v16, gate, qi, ki16, wi, new = _dsa_project(
        xn, B, T, np.arange(T), w, j, *dims, head_major=True)
    k_top = min(TOPK_MAX, T // 4)
    bias = indexer_bias_prompt(qi.reshape(B, T, nhi * di), wi.reshape(B, T, LANES), ki16, k_top)
    attn = attention_prompt(q, k16, v16, bias).reshape(B * T, nh * dh)
    return (attn, gate), new


def _dsa_layer_sample(xn, B, T, w, j, dims, ctx):
    nh, dh, nhi, di = dims
    cache_k, cache_v, cache_kidx, page_table = ctx
    n_pages = page_table.shape[1]
    page = cache_kidx.shape[2]
    past = n_pages * page
    q, k16, v16, gate, qi, ki16, wi, new = _dsa_project(
        xn, B, T, past + np.arange(T), w, j, *dims)
    k_top = min(TOPK_MAX, (past + T) // 4)
    cw = min(IDX_PAGES_PER_STEP, n_pages) * page
    qi_hq = jnp.transpose(qi.reshape(B, T, nhi, di), (0, 2, 1, 3)).reshape(B, nhi * T, di)
    token_last = lambda a, n: jnp.swapaxes(_pad_time(a, n), 1, 2)
    ki_new_t = token_last(ki16, cw)
    bias = indexer_bias_sample(qi_hq, wi.reshape(B, T, LANES), ki_new_t,
                               jnp.swapaxes(cache_kidx, 2, 3), j, page_table, k_top)
    q3 = q.reshape(B, T, nh * dh)
    own = (jnp.arange(nh * T)[:, None] // T) == (jnp.arange(nh * dh)[None, :] // dh)
    qbd = jnp.where(own[None], jnp.tile(q3, (1, nh, 1)), 0).astype(BF16)
    width = nh * dh
    n_new = ((T + LANES - 1) // LANES) * LANES
    attn = attention_sample(qbd, token_last(k16.reshape(B, T, width), n_new),
                            token_last(v16.reshape(B, T, width), n_new), bias,
                            jnp.transpose(cache_k, (0, 1, 3, 4, 2)),
                            jnp.transpose(cache_v, (0, 1, 3, 4, 2)), j, page_table, T)
    return (attn.reshape(B * T, width), gate), new


def _pool_layer(xn, B, T, buf, pos0, w, j):
    E = w["c_scale"].shape[-1]
    w_up = w["c_w_up"][j]
    xb, = project(xn,w_up[:, :E], (BF16,))
    z, = project(xn,w_up[:, E:], (BF16,))
    xb3 = xb.reshape(B, T, E)
    y = pool_mix(xb3, buf, w["c_w_grp"][j], w["c_scale"][j], pos0).reshape(B * T, E)
    nbuf = buf.shape[1]
    if T >= nbuf:
        new_buf = xb3[:, T - nbuf:].astype(F32)
    else:
        new_buf = jnp.concatenate([buf[:, T:].astype(F32), xb3.astype(F32)], axis=1)
    return (y, z), new_buf


def _run_group(x3, p4, pos0, mlstm_state, dsa_ctx, pool_state, w, dims):
    B, T, D = x3.shape
    depth = p4.shape[0]
    x = x3.reshape(B * T, D)
    p_all = p4.reshape(depth, B * T, p4.shape[-1])
    new_a, new_b, new_c = [], [], []
    xn = rmsnorm_cast(x, w["norm_g"][0])
    for i in range(depth):
        kind, j = i % 3, i // 3
        last = i == depth - 1
        nxt = dict(next_g=w["final_g"] if last else w["norm_g"][i + 1],
                   next_dtype=F32 if last else BF16)
        if kind == 0:
            state = (mlstm_state[0],) + tuple(s[j] for s in mlstm_state[1:])
            mix, st = _mlstm_layer(xn, B, T, state, w, j)
            new_a.append(st)
            x, xn = layer_tail("mlstm", mix, x, p_all, i, w["a_w_down"][j], w["ple_gate_w"][i],
                               w["ple_w"][i], skip=w["a_skip"][j], **nxt)
        elif kind == 1:
            if dsa_ctx is None:
                mix, st = _dsa_layer_prompt(xn, B, T, w, j, dims)
            else:
                mix, st = _dsa_layer_sample(xn, B, T, w, j, dims, dsa_ctx)
            new_b.append(st)
            x, xn = layer_tail("gated", mix, x, p_all, i, w["b_w_out"][j], w["ple_gate_w"][i],
                               w["ple_w"][i], **nxt)
        else:
            mix, st = _pool_layer(xn, B, T, pool_state[j], pos0, w, j)
            new_c.append(st)
            x, xn = layer_tail("gated", mix, x, p_all, i, w["c_w_down"][j], w["ple_gate_w"][i],
                               w["ple_w"][i], **nxt)
    a_states = tuple(jnp.stack([s[r] for s in new_a]) for r in range(4))
    b_states = tuple(jnp.stack([s[r] for s in new_b]) for r in range(3))
    c_state = jnp.stack(new_c)
    return xn.reshape(B, T, D), a_states, b_states, c_state


def kernel(x_prompt, x_sample, state_mlstm_C, state_mlstm_n, state_mlstm_m, state_mlstm_conv, state_pool,
           cache_k, cache_v, cache_kidx, page_table, p_prompt, p_sample,
           norm_g, final_g, ple_w, ple_gate_w,
           a_w_up, a_conv_w, a_conv_b, a_w_q, a_w_k, a_w_v, a_w_if, a_b_if, a_w_o, a_b_o, a_norm_g,
           a_skip, a_w_down, b_w_in, b_w_out, c_w_up, c_w_grp, c_scale, c_w_down):
    D = x_prompt.shape[-1]
    nh, dh = cache_k.shape[3], cache_k.shape[4]
    di = cache_kidx.shape[-1]
    aw = nh * dh
    nhi = (b_w_in.shape[-1] - 4 * aw - di) // (di + 1)
    dims = (nh, dh, nhi, di)
    nha = state_mlstm_C.shape[2]
    bf = lambda a: a.astype(BF16)

    def pad_cols(a, n):
        return jnp.pad(a, [(0, 0)] * (a.ndim - 1) + [(0, n - a.shape[-1])])

    o1, o2, o3, o4, o5, o6 = np.cumsum([aw, aw, aw, aw, nhi * di, di])
    w = dict(
        norm_g=norm_g, final_g=final_g, ple_w=bf(ple_w), ple_gate_w=bf(ple_gate_w),
        a_w_up=bf(a_w_up), a_conv_w=a_conv_w, a_conv_b=a_conv_b,
        a_w_q=bf(a_w_q), a_w_k=bf(a_w_k), a_w_v=bf(a_w_v), a_w_o=bf(a_w_o), a_b_o=a_b_o,
        a_w_if_pad=pad_cols(a_w_if, LANES), a_b_if_pad=pad_cols(a_b_if, LANES),
        a_norm_g=a_norm_g, a_skip=a_skip, a_w_down=bf(a_w_down),
        b_wq=bf(b_w_in[..., :o1]), b_wk=bf(b_w_in[..., o1:o2]), b_wv=bf(b_w_in[..., o2:o3]),
        b_wg=bf(b_w_in[..., o3:o4]), b_wqi=bf(b_w_in[..., o4:o5]),
        b_wki_pad=bf(pad_cols(b_w_in[..., o5:o6], LANES)),
        b_wwi_pad=bf(pad_cols(b_w_in[..., o6:], LANES)),
        b_w_out=bf(b_w_out), c_w_up=bf(c_w_up), c_w_grp=bf(c_w_grp), c_scale=c_scale,
        c_w_down=bf(c_w_down),
    )

    Bp = x_prompt.shape[0]
    na, nc = state_mlstm_C.shape[0], state_pool.shape[0]
    zeros_a = (jnp.zeros((na, Bp) + state_mlstm_C.shape[2:], F32),
               jnp.zeros((na, Bp) + state_mlstm_n.shape[2:], F32),
               jnp.zeros((na, Bp) + state_mlstm_m.shape[2:], F32),
               jnp.zeros((na, Bp) + state_mlstm_conv.shape[2:], F32))
    zeros_c = jnp.zeros((nc, Bp) + state_pool.shape[2:], F32)
    y_p, a_p, b_p, c_p = _run_group(x_prompt, p_prompt, 0, zeros_a, None, zeros_c, w, dims)

    past = page_table.shape[1] * cache_k.shape[2]
    y_s, a_s, b_s, c_s = _run_group(
        x_sample, p_sample, past,
        (state_mlstm_C, state_mlstm_n, state_mlstm_m, state_mlstm_conv),
        (cache_k, cache_v, cache_kidx, page_table), state_pool, w, dims)
    return (y_p, y_s, *a_p, *b_p, c_p, *a_s, *b_s, c_s)
```
